```python
import math
import jax, jax.numpy as jnp
from jax import lax
import numpy as np

D_MODEL = 2048
BATCH = 2
SEQ = 4096
DEPTH = 4
DEC_BATCH = 8
DEC_SEQ = 8
PAST_LEN = 16384
PAGE_SIZE = 128

N_MIXERS = 2
N_ATTN = (DEPTH + 1) // 2
N_CONV = DEPTH // 2
N_HEADS = 8
HEAD_DIM = D_MODEL // (2 * N_HEADS)
V_DIM = 2 * HEAD_DIM
D_FF = 4 * D_MODEL
CONV_WIDTH = 3
N_META = 16
N_BUCKETS = 32
MAX_DISTANCE = 128
Q_BLOCK = 128
LN_EPS = 1e-5
SUBLN_EPS = 1e-5
DEEPNORM_ALPHA = (2 * DEPTH) ** 0.25
DEEPNORM_BETA = (8 * DEPTH) ** -0.25

kernel_name = "diffattn_shortconv_deepnorm_hybrid_step"


def lambda_init(layer):
    return 0.8 - 0.6 * math.exp(-0.3 * layer)


def layer_norm(x, g, b):
    xf = x.astype(jnp.float32)
    mu = jnp.mean(xf, axis=-1, keepdims=True)
    var = jnp.mean(jnp.square(xf - mu), axis=-1, keepdims=True)
    y = (xf - mu) * lax.rsqrt(var + LN_EPS)
    return (y * g.astype(jnp.float32) + b.astype(jnp.float32)).astype(x.dtype)


def rel_bucket(q_pos, k_pos):
    n = jnp.maximum(q_pos[:, None] - k_pos[None, :], 0)
    max_exact = N_BUCKETS // 2
    nf = jnp.maximum(n, 1).astype(jnp.float32)
    large = max_exact + (jnp.log(nf / max_exact) / math.log(MAX_DISTANCE / max_exact)
                         * (N_BUCKETS - max_exact)).astype(jnp.int32)
    large = jnp.minimum(large, N_BUCKETS - 1)
    return jnp.where(n < max_exact, n, large)


def rel_bias_for(rel_bias, q_pos, k_pos):
    return jnp.transpose(rel_bias.astype(jnp.float32)[rel_bucket(q_pos, k_pos)], (2, 0, 1))


def diff_qkv(x, w_qkv):
    b, l = x.shape[0], x.shape[1]
    q, k, v = jnp.split(jnp.einsum('bld,de->ble', x, w_qkv), 3, axis=-1)
    return (q.reshape(b, l, N_HEADS, 2, HEAD_DIM),
            k.reshape(b, l, N_HEADS, 2, HEAD_DIM),
            v.reshape(b, l, N_HEADS, V_DIM))


def diff_lambda(lam_params, lam_init):
    lp = lam_params.astype(jnp.float32)
    return jnp.exp(jnp.sum(lp[0] * lp[1])) - jnp.exp(jnp.sum(lp[2] * lp[3])) + lam_init


def diff_attend(q, ks, vs, bias, mask, lam):
    logits = jnp.concatenate(
        [jnp.einsum('bqhmd,bkhmd->bhmqk', q, k, preferred_element_type=jnp.float32) for k in ks],
        axis=-1) * (HEAD_DIM ** -0.5)
    logits = jnp.where(mask, logits + bias[None, :, None], -jnp.inf)
    p = jax.nn.softmax(logits, axis=-1)
    w = (p[:, :, 0] - lam * p[:, :, 1]).astype(vs[0].dtype)
    start = 0
    out = None
    for v in vs:
        n = v.shape[1]
        part = jnp.einsum('bhqk,bkhe->bqhe', w[..., start:start + n], v)
        out = part if out is None else out + part
        start += n
    return out


def diff_out(o, subln_g, w_o, lam_init):
    of = o.astype(jnp.float32)
    of = (of * lax.rsqrt(jnp.mean(of * of, axis=-1, keepdims=True) + SUBLN_EPS)
          * subln_g.astype(jnp.float32) * (1.0 - lam_init))
    b, l = o.shape[0], o.shape[1]
    return jnp.einsum('ble,ed->bld', of.reshape(b, l, D_MODEL).astype(o.dtype), w_o)


def diff_attn_prompt(x, w_qkv, lam_params, subln_g, w_o, rel_bias, lam_init):
    b, l, _ = x.shape
    q, k, v = diff_qkv(x, w_qkv)
    lam = diff_lambda(lam_params, lam_init)
    n_blk = -(-l // Q_BLOCK)
    l_pad = n_blk * Q_BLOCK
    pad = l_pad - l
    qp = jnp.pad(q, ((0, 0), (0, pad), (0, 0), (0, 0), (0, 0)))
    kp = jnp.pad(k, ((0, 0), (0, pad), (0, 0), (0, 0), (0, 0)))
    vp = jnp.pad(v, ((0, 0), (0, pad), (0, 0), (0, 0)))
    k_pos = jnp.arange(l_pad, dtype=jnp.int32)
    q_blocks = jnp.transpose(qp.reshape(b, n_blk, Q_BLOCK, N_HEADS, 2, HEAD_DIM), (1, 0, 2, 3, 4, 5))
    starts = jnp.arange(n_blk, dtype=jnp.int32) * Q_BLOCK

    def one_block(args):
        qb, start = args
        q_pos = start + jnp.arange(Q_BLOCK, dtype=jnp.int32)
        mask = k_pos[None, :] <= q_pos[:, None]
        bias = rel_bias_for(rel_bias, q_pos, k_pos)
        return diff_attend(qb, [kp], [vp], bias, mask, lam)

    o = lax.map(one_block, (q_blocks, starts))
    o = jnp.transpose(o, (1, 0, 2, 3, 4)).reshape(b, l_pad, N_HEADS, V_DIM)[:, :l]
    return diff_out(o, subln_g, w_o, lam_init), k, v


def diff_attn_sample(x, cache_k, cache_v, layer_idx, page_table, w_qkv, lam_params, subln_g, w_o,
                     rel_bias, lam_init):
    bd, s, _ = x.shape
    q, k_new, v_new = diff_qkv(x, w_qkv)
    lam = diff_lambda(lam_params, lam_init)
    past = page_table.shape[1] * PAGE_SIZE
    k_past = cache_k[layer_idx, page_table].reshape(bd, past, N_HEADS, 2, HEAD_DIM)
    v_past = cache_v[layer_idx, page_table].reshape(bd, past, N_HEADS, V_DIM)
    q_pos = past + jnp.arange(s, dtype=jnp.int32)
    k_pos = jnp.arange(past + s, dtype=jnp.int32)
    mask = k_pos[None, :] <= q_pos[:, None]
    bias = rel_bias_for(rel_bias, q_pos, k_pos)
    o = diff_attend(q, [k_past.astype(q.dtype), k_new], [v_past.astype(v_new.dtype), v_new], bias, mask, lam)
    return diff_out(o, subln_g, w_o, lam_init), k_new, v_new


def short_conv(x, buf, w_in, w_conv, w_out):
    bg, cg, h = jnp.split(jnp.einsum('bld,de->ble', x, w_in), 3, axis=-1)
    u = cg * h
    ext = jnp.concatenate([buf.astype(u.dtype), u], axis=1)
    conv = lax.conv_general_dilated(ext, w_conv[:, None, :].astype(u.dtype), window_strides=(1,),
                                    padding='VALID', dimension_numbers=('NWC', 'WIO', 'NWC'),
                                    feature_group_count=D_MODEL)
    y = jnp.einsum('bld,de->ble', bg * conv, w_out)
    return y, ext[:, -(CONV_WIDTH - 1):]


def sq_relu_mlp(x, w_up, w_down):
    h = jnp.square(jax.nn.relu(jnp.einsum('bld,df->blf', x, w_up)))
    return jnp.einsum('blf,fd->bld', h, w_down)


def setup_inputs(seed: int = 0) -> dict:
    key = jax.random.key(seed)
    ks = jax.random.split(key, 24)
    n_pages = PAST_LEN // PAGE_SIZE
    n_used = DEC_BATCH * n_pages
    n_pool = n_used + max(1, n_used // 4)
    s = D_MODEL ** -0.5
    nrm = jax.random.normal
    x_prompt = nrm(ks[0], (BATCH, SEQ, D_MODEL), jnp.float32)
    x_sample = nrm(ks[1], (DEC_BATCH, DEC_SEQ, D_MODEL), jnp.float32)
    cache_k = nrm(ks[2], (N_ATTN, n_pool, PAGE_SIZE, N_HEADS, 2, HEAD_DIM), jnp.float32)
    cache_v = nrm(ks[3], (N_ATTN, n_pool, PAGE_SIZE, N_HEADS, V_DIM), jnp.float32)
    state_conv = nrm(ks[4], (N_CONV, DEC_BATCH, CONV_WIDTH - 1, D_MODEL), jnp.float32)
    page_table = jax.random.permutation(ks[5], n_pool)[:n_used].reshape(DEC_BATCH, n_pages).astype(jnp.int32)
    meta_tokens = nrm(ks[6], (N_META, D_MODEL), jnp.float32)
    rel_bias = 0.5 * nrm(ks[7], (N_BUCKETS, N_HEADS), jnp.float32)
    w_qk = nrm(ks[8], (N_ATTN, D_MODEL, 2 * D_MODEL), jnp.float32) * s
    w_v = nrm(ks[9], (N_ATTN, D_MODEL, D_MODEL), jnp.float32) * (s * DEEPNORM_BETA)
    attn_w_qkv = jnp.concatenate([w_qk, w_v], axis=-1)
    attn_lambda = 0.1 * nrm(ks[10], (N_ATTN, 4, HEAD_DIM), jnp.float32)
    attn_subln = 1.0 + 0.02 * nrm(ks[11], (N_ATTN, V_DIM), jnp.float32)
    attn_w_o = nrm(ks[12], (N_ATTN, D_MODEL, D_MODEL), jnp.float32) * (s * DEEPNORM_BETA)
    conv_w_in = nrm(ks[13], (N_CONV, D_MODEL, 3 * D_MODEL), jnp.float32) * s
    conv_w = nrm(ks[14], (N_CONV, CONV_WIDTH, D_MODEL), jnp.float32) * (CONV_WIDTH ** -0.5)
    conv_w_out = nrm(ks[15], (N_CONV, D_MODEL, D_MODEL), jnp.float32) * (s * DEEPNORM_BETA)
    mlp_w_up = nrm(ks[16], (DEPTH, D_MODEL, D_FF), jnp.float32) * s
    mlp_w_down = nrm(ks[17], (DEPTH, D_FF, D_MODEL), jnp.float32) * ((D_FF ** -0.5) * DEEPNORM_BETA)
    ln_mix_g = 1.0 + 0.02 * nrm(ks[18], (DEPTH, D_MODEL), jnp.float32)
    ln_mix_b = 0.02 * nrm(ks[19], (DEPTH, D_MODEL), jnp.float32)
    ln_mlp_g = 1.0 + 0.02 * nrm(ks[20], (DEPTH, D_MODEL), jnp.float32)
    ln_mlp_b = 0.02 * nrm(ks[21], (DEPTH, D_MODEL), jnp.float32)
    return {"x_prompt": x_prompt, "x_sample": x_sample, "cache_k": cache_k, "cache_v": cache_v,
            "state_conv": state_conv, "page_table": page_table, "meta_tokens": meta_tokens,
            "rel_bias": rel_bias, "attn_w_qkv": attn_w_qkv, "attn_lambda": attn_lambda,
            "attn_subln": attn_subln, "attn_w_o": attn_w_o, "conv_w_in": conv_w_in, "conv_w": conv_w,
            "conv_w_out": conv_w_out, "mlp_w_up": mlp_w_up, "mlp_w_down": mlp_w_down,
            "ln_mix_g": ln_mix_g, "ln_mix_b": ln_mix_b, "ln_mlp_g": ln_mlp_g, "ln_mlp_b": ln_mlp_b}


def reference(x_prompt, x_sample, cache_k, cache_v, state_conv, page_table, meta_tokens, rel_bias,
              attn_w_qkv, attn_lambda, attn_subln, attn_w_o, conv_w_in, conv_w, conv_w_out,
              mlp_w_up, mlp_w_down, ln_mix_g, ln_mix_b, ln_mlp_g, ln_mlp_b):
    b = x_prompt.shape[0]
    hp = jnp.concatenate([jnp.broadcast_to(meta_tokens[None].astype(x_prompt.dtype), (b, N_META, D_MODEL)),
                          x_prompt], axis=1)
    hs = x_sample
    k_p, v_p, c_p, k_s, v_s, c_s = [], [], [], [], [], []
    for i in range(DEPTH):
        j = i // N_MIXERS
        if i % N_MIXERS == 0:
            lam0 = lambda_init(i)
            yp, kp, vp = diff_attn_prompt(hp, attn_w_qkv[j], attn_lambda[j], attn_subln[j], attn_w_o[j],
                                          rel_bias, lam0)
            ys, kn, vn = diff_attn_sample(hs, cache_k, cache_v, j, page_table, attn_w_qkv[j], attn_lambda[j],
                                          attn_subln[j], attn_w_o[j], rel_bias, lam0)
            k_p.append(kp); v_p.append(vp); k_s.append(kn); v_s.append(vn)
        else:
            zero_buf = jnp.zeros((b, CONV_WIDTH - 1, D_MODEL), hp.dtype)
            yp, cp = short_conv(hp, zero_buf, conv_w_in[j], conv_w[j], conv_w_out[j])
            ys, cs = short_conv(hs, state_conv[j], conv_w_in[j], conv_w[j], conv_w_out[j])
            c_p.append(cp); c_s.append(cs)
        hp = layer_norm(DEEPNORM_ALPHA * hp + yp, ln_mix_g[i], ln_mix_b[i])
        hs = layer_norm(DEEPNORM_ALPHA * hs + ys, ln_mix_g[i], ln_mix_b[i])
        hp = layer_norm(DEEPNORM_ALPHA * hp + sq_relu_mlp(hp, mlp_w_up[i], mlp_w_down[i]), ln_mlp_g[i], ln_mlp_b[i])
        hs = layer_norm(DEEPNORM_ALPHA * hs + sq_relu_mlp(hs, mlp_w_up[i], mlp_w_down[i]), ln_mlp_g[i], ln_mlp_b[i])
    y_prompt = hp[:, N_META:]
    y_sample = hs
    return (y_prompt, y_sample, jnp.stack(k_p), jnp.stack(v_p), jnp.stack(c_p),
            jnp.stack(k_s), jnp.stack(v_s), jnp.stack(c_s))
```

```python
import functools
import math

import numpy as np
import jax
import jax.numpy as jnp
from jax import lax
from jax.experimental import pallas as pl
from jax.experimental.pallas import tpu as pltpu

F32 = jnp.float32
BF16 = jnp.bfloat16

MAX_DISTANCE = 128
LN_EPS = 1e-5
SUBLN_EPS = 1e-5
N_MIXERS = 2
MASK_VALUE = -1e30

V7X_LANES = 128
V7X_SUBLANES = 8
V7X_VMEM_LIMIT_BYTES = 60 * 1024 * 1024

SMALL_ROWS = 128
FLASH_TQ = 512
FLASH_CK = 256
DECODE_PAGES = 4


def _lambda_init(layer):
    return 0.8 - 0.6 * math.exp(-0.3 * layer)


def _pick_tile(total, target, multiple):
    best = None
    for t in range(multiple, min(total, target) + 1, multiple):
        if total % t == 0:
            best = t
    assert best is not None, (total, target, multiple)
    return best


def _cparams(sem):
    return pltpu.CompilerParams(dimension_semantics=sem, vmem_limit_bytes=V7X_VMEM_LIMIT_BYTES)


def _bucket_of_distance(n, n_buckets):
    n = np.asarray(n, np.int64)
    max_exact = n_buckets // 2
    nf = np.maximum(n, 1).astype(np.float64)
    large = max_exact + (np.log(nf / max_exact) / math.log(MAX_DISTANCE / max_exact)
                         * (n_buckets - max_exact)).astype(np.int64)
    large = np.minimum(large, n_buckets - 1)
    return np.where(n < max_exact, n, large).astype(np.int32)


def _far_distance(n_buckets):
    tab = _bucket_of_distance(np.arange(4 * MAX_DISTANCE), n_buckets)
    not_last = np.nonzero(tab != n_buckets - 1)[0]
    return int(not_last[-1]) + 1


def _bucket_tile(dist, n_buckets):
    dist = np.asarray(dist, np.int64)
    return np.where(dist < 0, -1, _bucket_of_distance(np.maximum(dist, 0), n_buckets)).astype(np.int32)


def _bias_expand_kernel(rb_ref, idx_ref, o_ref, *, n_buckets):
    h = pl.program_id(0)
    idx = idx_ref[...]
    far = rb_ref[n_buckets - 1, h]
    out = jnp.zeros(idx.shape, F32)
    for b in range(n_buckets - 1):
        out = jnp.where(idx == b, rb_ref[b, h] - far, out)
    o_ref[...] = jnp.where(idx < 0, MASK_VALUE, out)


def _bias_expand(idx_np, rel_bias):
    rows, cols = idx_np.shape
    n_buckets, n_heads = rel_bias.shape
    return pl.pallas_call(
        functools.partial(_bias_expand_kernel, n_buckets=n_buckets),
        grid=(n_heads,),
        in_specs=[pl.BlockSpec(memory_space=pltpu.SMEM),
                  pl.BlockSpec((rows, cols), lambda h: (0, 0))],
        out_specs=pl.BlockSpec((None, rows, cols), lambda h: (h, 0, 0)),
        out_shape=jax.ShapeDtypeStruct((n_heads, rows, cols), F32),
        compiler_params=_cparams(("arbitrary",)),
        name="bias_expand",
    )(rel_bias, jnp.asarray(idx_np))


def _qkv_kernel(x_ref, w_ref, o16_ref, kv32_ref, *, n_q_tiles, q_scale):
    j = pl.program_id(1)
    acc = jnp.dot(x_ref[...].astype(BF16), w_ref[...].astype(BF16), preferred_element_type=F32)
    scale = jnp.where(j < n_q_tiles, q_scale, 1.0).astype(F32)
    o16_ref[...] = (acc * scale).astype(BF16)

    @pl.when(j >= n_q_tiles)
    def _():
        kv32_ref[...] = acc


def _qkv_proj(x, w_qkv, layer, q_scale):
    m, d = x.shape
    tm = _pick_tile(m, 1040, 16)
    tn = 512
    n_q_tiles = d // tn
    return pl.pallas_call(
        functools.partial(_qkv_kernel, n_q_tiles=n_q_tiles, q_scale=q_scale),
        grid=(m // tm, 3 * d // tn),
        in_specs=[pl.BlockSpec((tm, d), lambda i, j: (i, 0)),
                  pl.BlockSpec((None, d, tn), lambda i, j: (layer, 0, j))],
        out_specs=[pl.BlockSpec((tm, tn), lambda i, j: (i, j)),
                   pl.BlockSpec((tm, tn), lambda i, j: (i, jnp.maximum(j - n_q_tiles, 0)))],
        out_shape=[jax.ShapeDtypeStruct((m, 3 * d), BF16),
                   jax.ShapeDtypeStruct((m, 2 * d), F32)],
        compiler_params=_cparams(("arbitrary", "arbitrary")),
        name="qkv_proj",
    )(x, w_qkv)


def _convin_kernel(x_ref, wb_ref, wc_ref, wh_ref, bg_ref, u_ref):
    xb = x_ref[...].astype(BF16)
    bg_ref[...] = jnp.dot(xb, wb_ref[...].astype(BF16), preferred_element_type=F32)
    cg = jnp.dot(xb, wc_ref[...].astype(BF16), preferred_element_type=F32)
    hh = jnp.dot(xb, wh_ref[...].astype(BF16), preferred_element_type=F32)
    u_ref[...] = cg * hh


def _conv_in(x, w_in, layer):
    m, d = x.shape
    tm = _pick_tile(m, 1040, 16)
    tn = 256
    nt = d // tn
    wspec = lambda part: pl.BlockSpec((None, d, tn), lambda i, j: (layer, 0, part * nt + j))
    return pl.pallas_call(
        _convin_kernel,
        grid=(m // tm, nt),
        in_specs=[pl.BlockSpec((tm, d), lambda i, j: (i, 0)), wspec(0), wspec(1), wspec(2)],
        out_specs=[pl.BlockSpec((tm, tn), lambda i, j: (i, j)),
                   pl.BlockSpec((tm, tn), lambda i, j: (i, j))],
        out_shape=[jax.ShapeDtypeStruct((m, d), F32), jax.ShapeDtypeStruct((m, d), F32)],
        compiler_params=_cparams(("arbitrary", "arbitrary")),
        name="conv_in",
    )(x, w_in, w_in, w_in)


def _convgate_kernel(u_ref, bg_ref, halo_ref, meta_ref, h1_ref, h2_ref, w_ref, z_ref, *,
                     seq_blocks, n_prompt_blocks, n_batch, n_meta, dec_seq, n_sample_rows):
    i = pl.program_id(0)
    u = u_ref[...]
    rows = u.shape[0]
    r = lax.broadcasted_iota(jnp.int32, (rows, 1), 0)

    halo = halo_ref[...]
    prev2, prev1 = halo[V7X_SUBLANES - 2:V7X_SUBLANES - 1], halo[V7X_SUBLANES - 1:V7X_SUBLANES]
    meta = meta_ref[...]
    for b in range(n_batch):
        starts_batch = i == b * seq_blocks
        prev2 = jnp.where(starts_batch, meta[(b + 1) * n_meta - 2:(b + 1) * n_meta - 1], prev2)
        prev1 = jnp.where(starts_batch, meta[(b + 1) * n_meta - 1:(b + 1) * n_meta], prev1)

    p1 = jnp.where(r == 0, prev1, pltpu.roll(u, 1, axis=0))
    p2 = jnp.where(r == 0, prev2, jnp.where(r == 1, prev1, pltpu.roll(u, 2, axis=0)))

    small = i == n_prompt_blocks
    n_meta_rows = n_batch * n_meta
    in_meta = r < n_meta_rows
    in_sample = jnp.logical_and(r >= n_meta_rows, r < n_meta_rows + n_sample_rows)
    pos = jnp.where(in_meta, r % n_meta, (r - n_meta_rows) % dec_seq)
    restart = jnp.logical_or(in_meta, in_sample)
    m1 = jnp.logical_and(small, jnp.logical_and(restart, pos == 0))
    m2 = jnp.logical_and(small, jnp.logical_and(restart, pos <= 1))
    p1 = jnp.where(m1, h1_ref[...], p1)
    p2 = jnp.where(m2, h2_ref[...], p2)

    w = w_ref[...]
    conv = w[0:1] * p2 + w[1:2] * p1 + w[2:3] * u
    z_ref[...] = (bg_ref[...] * conv).astype(BF16)


def _conv_gate(u, bg, hist1, hist2, w_conv, layer, *, seq, n_batch, n_meta, dec_seq, n_sample_rows):
    m, d = u.shape
    rows = SMALL_ROWS
    n_prompt_blocks = n_batch * seq // rows
    assert m == (n_prompt_blocks + 1) * rows and seq % rows == 0
    halo_per_block = rows // V7X_SUBLANES
    kern = functools.partial(_convgate_kernel, seq_blocks=seq // rows, n_prompt_blocks=n_prompt_blocks,
                             n_batch=n_batch, n_meta=n_meta, dec_seq=dec_seq, n_sample_rows=n_sample_rows)
    return pl.pallas_call(
        kern,
        grid=(m // rows,),
        in_specs=[pl.BlockSpec((rows, d), lambda i: (i, 0)),
                  pl.BlockSpec((rows, d), lambda i: (i, 0)),
                  pl.BlockSpec((V7X_SUBLANES, d), lambda i: (jnp.maximum(i * halo_per_block - 1, 0), 0)),
                  pl.BlockSpec((rows, d), lambda i: (n_prompt_blocks, 0)),
                  pl.BlockSpec((rows, d), lambda i: (0, 0)),
                  pl.BlockSpec((rows, d), lambda i: (0, 0)),
                  pl.BlockSpec((None, 3, d), lambda i: (layer, 0, 0))],
        out_specs=pl.BlockSpec((rows, d), lambda i: (i, 0)),
        out_shape=jax.ShapeDtypeStruct((m, d), BF16),
        compiler_params=_cparams(("arbitrary",)),
        name="conv_gate",
    )(u, bg, u, u, hist1, hist2, w_conv)


def _residual_layer_norm(x_ref, o_ref, g_ref, b_ref, alpha, rows_chunk):
    g = g_ref[...]
    b = b_ref[...]

    def body(c, carry):
        rs = pl.ds(pl.multiple_of(c * rows_chunk, rows_chunk), rows_chunk)
        y = alpha * x_ref[rs, :] + o_ref[rs, :]
        mu = jnp.mean(y, axis=-1, keepdims=True)
        yc = y - mu
        var = jnp.mean(yc * yc, axis=-1, keepdims=True)
        o_ref[rs, :] = yc * lax.rsqrt(var + LN_EPS) * g + b
        return carry

    lax.fori_loop(0, x_ref.shape[0] // rows_chunk, body, 0)


def _outproj_kernel(a_ref, w_ref, x_ref, g_ref, b_ref, o_ref, *, alpha, rows_chunk):
    k = pl.program_id(1)
    part = jnp.dot(a_ref[...], w_ref[...].astype(BF16), preferred_element_type=F32)

    @pl.when(k == 0)
    def _():
        o_ref[...] = part

    @pl.when(k > 0)
    def _():
        o_ref[...] += part

    @pl.when(k == pl.num_programs(1) - 1)
    def _():
        _residual_layer_norm(x_ref, o_ref, g_ref, b_ref, alpha, rows_chunk)


def _outproj_ln(a16, w, layer, x, g, b, alpha):
    m, d = x.shape
    tm = _pick_tile(m, 832, 16)
    tk = 512
    rows_chunk = _pick_tile(tm, 64, V7X_SUBLANES)
    return pl.pallas_call(
        functools.partial(_outproj_kernel, alpha=alpha, rows_chunk=rows_chunk),
        grid=(m // tm, d // tk),
        in_specs=[pl.BlockSpec((tm, tk), lambda i, k: (i, k)),
                  pl.BlockSpec((None, tk, d), lambda i, k: (layer, k, 0)),
                  pl.BlockSpec((tm, d), lambda i, k: (i, 0)),
                  pl.BlockSpec((1, d), lambda i, k: (0, 0)),
                  pl.BlockSpec((1, d), lambda i, k: (0, 0))],
        out_specs=pl.BlockSpec((tm, d), lambda i, k: (i, 0)),
        out_shape=jax.ShapeDtypeStruct((m, d), F32),
        compiler_params=_cparams(("arbitrary", "arbitrary")),
        name="outproj_ln",
    )(a16, w, x, g, b)


def _mlp_kernel(x_ref, wu_ref, wd_ref, g_ref, b_ref, o_ref, xb_ref, *, alpha, rows_chunk):
    f = pl.program_id(1)

    @pl.when(f == 0)
    def _():
        xb_ref[...] = x_ref[...].astype(BF16)

    h = jnp.dot(xb_ref[...], wu_ref[...].astype(BF16), preferred_element_type=F32)
    h = jnp.square(jnp.maximum(h, 0.0)).astype(BF16)
    part = jnp.dot(h, wd_ref[...].astype(BF16), preferred_element_type=F32)

    @pl.when(f == 0)
    def _():
        o_ref[...] = part

    @pl.when(f > 0)
    def _():
        o_ref[...] += part

    @pl.when(f == pl.num_programs(1) - 1)
    def _():
        _residual_layer_norm(x_ref, o_ref, g_ref, b_ref, alpha, rows_chunk)


def _mlp_ln(x, w_up, w_down, layer, g, b, alpha):
    m, d = x.shape
    d_ff = w_up.shape[-1]
    tm = _pick_tile(m, 832, 16)
    tf = 512
    rows_chunk = _pick_tile(tm, 64, V7X_SUBLANES)
    return pl.pallas_call(
        functools.partial(_mlp_kernel, alpha=alpha, rows_chunk=rows_chunk),
        grid=(m // tm, d_ff // tf),
        in_specs=[pl.BlockSpec((tm, d), lambda i, f: (i, 0)),
                  pl.BlockSpec((None, d, tf), lambda i, f: (layer, 0, f)),
                  pl.BlockSpec((None, tf, d), lambda i, f: (layer, f, 0)),
                  pl.BlockSpec((1, d), lambda i, f: (0, 0)),
                  pl.BlockSpec((1, d), lambda i, f: (0, 0))],
        out_specs=pl.BlockSpec((tm, d), lambda i, f: (i, 0)),
        out_shape=jax.ShapeDtypeStruct((m, d), F32),
        scratch_shapes=[pltpu.VMEM((tm, d), BF16)],
        compiler_params=_cparams(("arbitrary", "arbitrary")),
        name="mlp_ln",
    )(x, w_up, w_down, g, b)


def _diff_lambda(lam_ref, lam_init):
    lp = lam_ref[...]
    a = jnp.sum(lp[0:1] * lp[1:2], axis=-1, keepdims=True)
    b = jnp.sum(lp[2:3] * lp[3:4], axis=-1, keepdims=True)
    return jnp.exp(a) - jnp.exp(b) + lam_init


def _sub_layer_norm(o, g_ref, lam_init):
    return o * lax.rsqrt(jnp.mean(o * o, axis=-1, keepdims=True) + SUBLN_EPS) * (g_ref[...] * (1.0 - lam_init))


def _dot_nt(a, b):
    return lax.dot_general(a, b, (((1,), (1,)), ((), ())), preferred_element_type=F32)


def _flash_kernel(q_ref, k_ref, v_ref, km_ref, vm_ref, qm_ref, bn_ref, bm_ref, bmm_ref, lam_ref, g_ref,
                  o_ref, om_ref, m_sc, l_sc, acc_sc, *, head_dim, lam_init):
    qi = pl.program_id(2)
    tq = q_ref.shape[0]
    ck = bn_ref.shape[2]
    ratio = tq // ck
    n_meta = km_ref.shape[0]
    lam = _diff_lambda(lam_ref, lam_init)

    q = q_ref[...]
    q_maps = (q[:, :head_dim], q[:, head_dim:])

    def stacked_scores(queries, keys, bias):
        parts = [_dot_nt(queries[m], keys[:, m * head_dim:(m + 1) * head_dim]) for m in range(2)]
        if bias is not None:
            parts = [p + bias for p in parts]
        return jnp.concatenate(parts, axis=0)

    km = km_ref[...]
    vm = vm_ref[...]
    s = stacked_scores(q_maps, km, jnp.where(qi == 0, bm_ref[...], 0.0))
    m0 = jnp.max(s, axis=-1, keepdims=True)
    p = jnp.exp(s - m0)
    m_sc[...] = m0
    l_sc[...] = jnp.sum(p, axis=-1, keepdims=True)
    acc_sc[...] = jnp.dot(p.astype(BF16), vm, preferred_element_type=F32)

    def chunk(start, bias):
        kb = k_ref[pl.ds(start, ck), :]
        vb = v_ref[pl.ds(start, ck), :]
        s = stacked_scores(q_maps, kb, bias)
        m_old = m_sc[...]
        m_new = jnp.maximum(m_old, jnp.max(s, axis=-1, keepdims=True))
        alpha = jnp.exp(m_old - m_new)
        p = jnp.exp(s - m_new)
        l_sc[...] = alpha * l_sc[...] + jnp.sum(p, axis=-1, keepdims=True)
        acc_sc[...] = alpha * acc_sc[...] + jnp.dot(p.astype(BF16), vb, preferred_element_type=F32)
        m_sc[...] = m_new

    def far_body(kc, carry):
        chunk(pl.multiple_of(kc * ck, ck), None)
        return carry

    lax.fori_loop(0, jnp.maximum(ratio * qi - 1, 0), far_body, 0)

    for t in range(ratio + 1):
        start = pl.multiple_of((ratio * qi - 1 + t) * ck, ck)
        if t == 0:
            @pl.when(qi > 0)
            def _():
                chunk(start, bn_ref[0])
        else:
            chunk(start, bn_ref[t])

    on = acc_sc[...] * (1.0 / l_sc[...])
    o = on[:tq] - lam * on[tq:]
    o_ref[...] = _sub_layer_norm(o, g_ref, lam_init).astype(BF16)

    @pl.when(qi == 0)
    def _():
        qm = qm_ref[...]
        sm = stacked_scores((qm[:, :head_dim], qm[:, head_dim:]), km, bmm_ref[...])
        pm = jnp.exp(sm - jnp.max(sm, axis=-1, keepdims=True))
        onm = jnp.dot(pm.astype(BF16), vm, preferred_element_type=F32) * (1.0 / jnp.sum(pm, axis=-1, keepdims=True))
        om_ref[...] = _sub_layer_norm(onm[:n_meta] - lam * onm[n_meta:], g_ref, lam_init)


def _flash_prompt(qkv16, bias_near, bias_meta, bias_mm, lam_params, subln, layer, lam_init, *,
                  n_batch, seq, n_heads, head_dim, n_meta):
    m, d3 = qkv16.shape
    d = d3 // 3
    vd = 2 * head_dim
    tq = FLASH_TQ
    nq = seq // tq
    mp = n_batch * seq
    meta_blk = mp // n_meta
    kern = functools.partial(_flash_kernel, head_dim=head_dim, lam_init=lam_init)
    return pl.pallas_call(
        kern,
        grid=(n_batch, n_heads, nq),
        in_specs=[pl.BlockSpec((tq, vd), lambda b, h, qi: (b * nq + qi, h)),
                  pl.BlockSpec((seq, vd), lambda b, h, qi: (b, n_heads + h)),
                  pl.BlockSpec((seq, vd), lambda b, h, qi: (b, 2 * n_heads + h)),
                  pl.BlockSpec((n_meta, vd), lambda b, h, qi: (meta_blk + b, n_heads + h)),
                  pl.BlockSpec((n_meta, vd), lambda b, h, qi: (meta_blk + b, 2 * n_heads + h)),
                  pl.BlockSpec((n_meta, vd), lambda b, h, qi: (meta_blk + b, h)),
                  pl.BlockSpec((None,) + bias_near.shape[1:], lambda b, h, qi: (h, 0, 0, 0)),
                  pl.BlockSpec((None,) + bias_meta.shape[1:], lambda b, h, qi: (h, 0, 0)),
                  pl.BlockSpec((None,) + bias_mm.shape[1:], lambda b, h, qi: (h, 0, 0)),
                  pl.BlockSpec((None, 4, head_dim), lambda b, h, qi: (layer, 0, 0)),
                  pl.BlockSpec((None, 1, vd), lambda b, h, qi: (layer, 0, 0))],
        out_specs=[pl.BlockSpec((tq, vd), lambda b, h, qi: (b * nq + qi, h)),
                   pl.BlockSpec((n_meta, vd), lambda b, h, qi: (b, h))],
        out_shape=[jax.ShapeDtypeStruct((m, d), BF16),
                   jax.ShapeDtypeStruct((n_batch * n_meta, d), F32)],
        scratch_shapes=[pltpu.VMEM((2 * tq, 1), F32), pltpu.VMEM((2 * tq, 1), F32),
                        pltpu.VMEM((2 * tq, vd), F32)],
        compiler_params=_cparams(("arbitrary", "arbitrary", "arbitrary")),
        name="flash_prompt",
    )(qkv16, qkv16, qkv16, qkv16, qkv16, qkv16, bias_near, bias_meta, bias_mm, lam_params, subln)


def _decode_kernel(pt_ref, q_ref, kn_ref, vn_ref, *rest, n_heads, head_dim, n_pages_step, lam_init):
    kp_refs = rest[:n_pages_step]
    vp_refs = rest[n_pages_step:2 * n_pages_step]
    bd_ref, bnw_ref, lam_ref, g_ref, o_ref, qb_sc, m_sc, l_sc, acc_sc = rest[2 * n_pages_step:]
    st = pl.program_id(1)
    last = pl.num_programs(1) - 1
    dec_seq, d = q_ref.shape
    vd = 2 * head_dim
    grp = 2 * dec_seq

    def pv(p, v16):
        return jnp.concatenate(
            [jnp.dot(p[h * grp:(h + 1) * grp].astype(BF16), v16[:, h * vd:(h + 1) * vd],
                     preferred_element_type=F32) for h in range(n_heads)], axis=0)

    @pl.when(st == 0)
    def _():
        qt = jnp.concatenate([q_ref[...]] * (2 * n_heads), axis=0)
        r = lax.broadcasted_iota(jnp.int32, qt.shape, 0) // dec_seq
        c = lax.broadcasted_iota(jnp.int32, qt.shape, 1) // head_dim
        qb = jnp.where(r == c, qt, 0.0).astype(BF16)
        qb_sc[...] = qb
        pad = jnp.zeros((dec_seq, d), F32)
        kn = jnp.concatenate([kn_ref[...], pad], axis=0).astype(BF16)
        vn = jnp.concatenate([vn_ref[...], pad], axis=0).astype(BF16)
        s = _dot_nt(qb, kn) + bnw_ref[...]
        m0 = jnp.max(s, axis=-1, keepdims=True)
        p = jnp.exp(s - m0)
        m_sc[...] = m0
        l_sc[...] = jnp.sum(p, axis=-1, keepdims=True)
        acc_sc[...] = pv(p, vn)

    k16 = jnp.concatenate([r[...].astype(BF16) for r in kp_refs], axis=0)
    v16 = jnp.concatenate([r[...].astype(BF16) for r in vp_refs], axis=0)
    s = _dot_nt(qb_sc[...], k16) + jnp.where(st == last, bd_ref[...], 0.0)
    m_old = m_sc[...]
    m_new = jnp.maximum(m_old, jnp.max(s, axis=-1, keepdims=True))
    alpha = jnp.exp(m_old - m_new)
    p = jnp.exp(s - m_new)
    l_sc[...] = alpha * l_sc[...] + jnp.sum(p, axis=-1, keepdims=True)
    acc_sc[...] = alpha * acc_sc[...] + pv(p, v16)
    m_sc[...] = m_new

    @pl.when(st == last)
    def _():
        lam = _diff_lambda(lam_ref, lam_init)
        on = acc_sc[...] * (1.0 / l_sc[...])
        for h in range(n_heads):
            oh = on[h * grp:h * grp + dec_seq] - lam * on[h * grp + dec_seq:(h + 1) * grp]
            o_ref[:, h * vd:(h + 1) * vd] = _sub_layer_norm(oh, g_ref, lam_init)


def _decode_attn(q32, kv32, cache_k, cache_v, page_table, bias_past, bias_new, lam_params, subln, layer,
                 lam_init, *, new_row0, n_heads, head_dim):
    n_rows, d = q32.shape
    dec_batch, n_pages = page_table.shape
    dec_seq = n_rows // dec_batch
    n_attn, n_pool, page = cache_k.shape[:3]
    ck = cache_k.reshape(n_attn, n_pool, page, d)
    cv = cache_v.reshape(n_attn, n_pool, page, d)
    nps = DECODE_PAGES
    assert n_pages % nps == 0 and new_row0 % dec_seq == 0
    n_steps = n_pages // nps
    vd = 2 * head_dim
    rows = 2 * n_heads * dec_seq
    new_blk = new_row0 // dec_seq

    def page_spec(t):
        return pl.BlockSpec((None, None, page, d),
                            lambda bd, st, pt: (layer, pt[bd * n_pages + st * nps + t], 0, 0))

    grid_spec = pltpu.PrefetchScalarGridSpec(
        num_scalar_prefetch=1,
        grid=(dec_batch, n_steps),
        in_specs=[pl.BlockSpec((dec_seq, d), lambda bd, st, pt: (bd, 0)),
                  pl.BlockSpec((dec_seq, d), lambda bd, st, pt: (new_blk + bd, 0)),
                  pl.BlockSpec((dec_seq, d), lambda bd, st, pt: (new_blk + bd, 1))]
                 + [page_spec(t) for t in range(nps)] + [page_spec(t) for t in range(nps)]
                 + [pl.BlockSpec(bias_past.shape, lambda bd, st, pt: (0, 0)),
                    pl.BlockSpec(bias_new.shape, lambda bd, st, pt: (0, 0)),
                    pl.BlockSpec((None, 4, head_dim), lambda bd, st, pt: (layer, 0, 0)),
                    pl.BlockSpec((None, 1, vd), lambda bd, st, pt: (layer, 0, 0))],
        out_specs=pl.BlockSpec((dec_seq, d), lambda bd, st, pt: (bd, 0)),
        scratch_shapes=[pltpu.VMEM((rows, d), BF16), pltpu.VMEM((rows, 1), F32),
                        pltpu.VMEM((rows, 1), F32), pltpu.VMEM((rows, vd), F32)],
    )
    kern = functools.partial(_decode_kernel, n_heads=n_heads, head_dim=head_dim, n_pages_step=nps,
                             lam_init=lam_init)
    return pl.pallas_call(
        kern,
        grid_spec=grid_spec,
        out_shape=jax.ShapeDtypeStruct((n_rows, d), F32),
        compiler_params=_cparams(("arbitrary", "arbitrary")),
        name="decode_attn",
    )(page_table.reshape(-1), q32, kv32, kv32, *([ck] * nps), *([cv] * nps),
      bias_past, bias_new, lam_params, subln)


def kernel(x_prompt, x_sample, cache_k, cache_v, state_conv, page_table, meta_tokens, rel_bias,
           attn_w_qkv, attn_lambda, attn_subln, attn_w_o, conv_w_in, conv_w, conv_w_out,
           mlp_w_up, mlp_w_down, ln_mix_g, ln_mix_b, ln_mlp_g, ln_mlp_b):
    n_batch, seq, d = x_prompt.shape
    dec_batch, dec_seq, _ = x_sample.shape
    n_meta = meta_tokens.shape[0]
    n_buckets, n_heads = rel_bias.shape
    head_dim = cache_k.shape[-1]
    vd = cache_v.shape[-1]
    page = cache_k.shape[2]
    past = page_table.shape[1] * page
    depth = mlp_w_up.shape[0]
    assert d == n_heads * vd and vd == 2 * head_dim
    alpha = (2 * depth) ** 0.25

    mp = n_batch * seq
    n_meta_rows = n_batch * n_meta
    n_sample_rows = dec_batch * dec_seq
    assert n_meta_rows + n_sample_rows <= SMALL_ROWS and seq % FLASH_TQ == 0
    assert n_meta % V7X_SUBLANES == 0 and dec_seq % V7X_SUBLANES == 0
    meta0, samp0 = mp, mp + n_meta_rows
    mt = mp + SMALL_ROWS

    x = jnp.concatenate(
        [x_prompt.reshape(mp, d)] + [meta_tokens.astype(F32)] * n_batch
        + [x_sample.reshape(n_sample_rows, d), jnp.zeros((mt - samp0 - n_sample_rows, d), F32)], axis=0)

    far = _far_distance(n_buckets)
    tq, ck, nps = FLASH_TQ, FLASH_CK, DECODE_PAGES
    ratio = tq // ck
    assert ck + 1 >= far and n_meta + tq - (n_meta - 1) >= far and (nps - 1) * page + 1 >= far
    rr = np.arange(tq)[:, None]
    near_idx = np.stack([_bucket_tile(rr - (t - 1) * ck - np.arange(ck)[None, :], n_buckets)
                         for t in range(ratio + 1)])
    meta_idx = _bucket_tile(n_meta + rr - np.arange(n_meta)[None, :], n_buckets)
    mm_idx = _bucket_tile(np.arange(n_meta)[:, None] - np.arange(n_meta)[None, :], n_buckets)
    step_of_row = (np.arange(2 * dec_seq) % dec_seq)[:, None]
    past_idx = _bucket_tile(nps * page + step_of_row - np.arange(nps * page)[None, :], n_buckets)
    new_cols = np.arange(2 * dec_seq)[None, :]
    new_idx = np.where(new_cols < dec_seq, _bucket_tile(step_of_row - new_cols, n_buckets), -1).astype(np.int32)

    bias_near = _bias_expand(near_idx.reshape((ratio + 1) * tq, ck), rel_bias).reshape(
        n_heads, ratio + 1, tq, ck)
    bias_meta = _bias_expand(meta_idx, rel_bias)
    bias_mm = _bias_expand(mm_idx, rel_bias)
    bias_past = _bias_expand(past_idx, rel_bias).reshape(n_heads * 2 * dec_seq, nps * page)
    bias_new = _bias_expand(new_idx, rel_bias).reshape(n_heads * 2 * dec_seq, 2 * dec_seq)

    def conv_history(state):
        h1 = jnp.zeros((dec_batch, dec_seq, d), F32).at[:, 0].set(state[:, 1])
        h2 = jnp.zeros((dec_batch, dec_seq, d), F32).at[:, 0].set(state[:, 0]).at[:, 1].set(state[:, 1])
        pad_lo = jnp.zeros((n_meta_rows, d), F32)
        pad_hi = jnp.zeros((SMALL_ROWS - n_meta_rows - n_sample_rows, d), F32)
        cat = lambda h: jnp.concatenate([pad_lo, h.reshape(n_sample_rows, d), pad_hi], axis=0)
        return cat(h1), cat(h2)

    subln3 = attn_subln.reshape(attn_subln.shape[0], 1, vd)
    k_p, v_p, c_p, k_s, v_s, c_s = [], [], [], [], [], []
    for i in range(depth):
        j = i // N_MIXERS
        g_mix, b_mix = ln_mix_g[i][None], ln_mix_b[i][None]
        if i % N_MIXERS == 0:
            lam0 = _lambda_init(i)
            qkv16, kv32 = _qkv_proj(x, attn_w_qkv, j, head_dim ** -0.5)
            o16, o_meta = _flash_prompt(qkv16, bias_near, bias_meta, bias_mm, attn_lambda, subln3, j, lam0,
                                        n_batch=n_batch, seq=seq, n_heads=n_heads, head_dim=head_dim,
                                        n_meta=n_meta)
            q32 = qkv16[samp0:samp0 + n_sample_rows, :d].astype(F32)
            o_samp = _decode_attn(q32, kv32, cache_k, cache_v, page_table, bias_past, bias_new, attn_lambda,
                                  subln3, j, lam0, new_row0=samp0, n_heads=n_heads, head_dim=head_dim)
            o_small = jnp.concatenate(
                [o_meta, o_samp, jnp.zeros((mt - samp0 - n_sample_rows, d), F32)], axis=0).astype(BF16)
            a16 = lax.dynamic_update_slice(o16, o_small, (mp, 0))
            x = _outproj_ln(a16, attn_w_o, j, x, g_mix, b_mix, alpha)

            k32 = kv32[:, :d]
            v32 = kv32[:, d:]
            for arr, dst_p, dst_s, tail in ((k32, k_p, k_s, (n_heads, 2, head_dim)),
                                            (v32, v_p, v_s, (n_heads, vd))):
                prm = arr[:mp].reshape(n_batch, seq, d)
                met = arr[meta0:samp0].reshape(n_batch, n_meta, d)
                dst_p.append(jnp.concatenate([met, prm], axis=1).reshape((n_batch, n_meta + seq) + tail))
                dst_s.append(arr[samp0:samp0 + n_sample_rows].reshape((dec_batch, dec_seq) + tail))
        else:
            bg, u = _conv_in(x, conv_w_in, j)
            hist1, hist2 = conv_history(state_conv[j])
            z16 = _conv_gate(u, bg, hist1, hist2, conv_w, j, seq=seq, n_batch=n_batch, n_meta=n_meta,
                             dec_seq=dec_seq, n_sample_rows=n_sample_rows)
            x = _outproj_ln(z16, conv_w_out, j, x, g_mix, b_mix, alpha)
            c_p.append(u[:mp].reshape(n_batch, seq, d)[:, seq - 2:])
            c_s.append(u[samp0:samp0 + n_sample_rows].reshape(dec_batch, dec_seq, d)[:, dec_seq - 2:])
        x = _mlp_ln(x, mlp_w_up, mlp_w_down, i, ln_mlp_g[i][None], ln_mlp_b[i][None], alpha)

    y_prompt = x[:mp].reshape(n_batch, seq, d)
    y_sample = x[samp0:samp0 + n_sample_rows].reshape(dec_batch, dec_seq, d)
    return (y_prompt, y_sample, jnp.stack(k_p), jnp.stack(v_p), jnp.stack(c_p),
            jnp.stack(k_s), jnp.stack(v_s), jnp.stack(c_s))
```

```python
import functools
import math

import numpy as np
import jax
import jax.numpy as jnp
from jax import lax
from jax.experimental import pallas as pl
from jax.experimental.pallas import tpu as pltpu

F32 = jnp.float32
BF16 = jnp.bfloat16

MAX_DISTANCE = 128
LN_EPS = 1e-5
SUBLN_EPS = 1e-5
N_MIXERS = 2
MASK_VALUE = -1e30

V7X_LANES = 128
V7X_SUBLANES = 8
V7X_VMEM_LIMIT_BYTES = 60 * 1024 * 1024

SMALL_ROWS = 128
FLASH_TQ = 512
FLASH_CK = 256
DECODE_PAGES = 4


def _lambda_init(layer):
    return 0.8 - 0.6 * math.exp(-0.3 * layer)


def _pick_tile(total, target, multiple):
    best = None
    for t in range(multiple, min(total, target) + 1, multiple):
        if total % t == 0:
            best = t
    assert best is not None, (total, target, multiple)
    return best


def _cparams(sem):
    return pltpu.CompilerParams(dimension_semantics=sem, vmem_limit_bytes=V7X_VMEM_LIMIT_BYTES)


def _bucket_of_distance(n, n_buckets):
    n = np.asarray(n, np.int64)
    max_exact = n_buckets // 2
    nf = np.maximum(n, 1).astype(np.float64)
    large = max_exact + (np.log(nf / max_exact) / math.log(MAX_DISTANCE / max_exact)
                         * (n_buckets - max_exact)).astype(np.int64)
    large = np.minimum(large, n_buckets - 1)
    return np.where(n < max_exact, n, large).astype(np.int32)


def _far_distance(n_buckets):
    tab = _bucket_of_distance(np.arange(4 * MAX_DISTANCE), n_buckets)
    not_last = np.nonzero(tab != n_buckets - 1)[0]
    return int(not_last[-1]) + 1


def _bucket_tile(dist, n_buckets):
    dist = np.asarray(dist, np.int64)
    return np.where(dist < 0, -1, _bucket_of_distance(np.maximum(dist, 0), n_buckets)).astype(np.int32)


def _bias_expand_kernel(rb_ref, idx_ref, o_ref, *, n_buckets):
    h = pl.program_id(0)
    idx = idx_ref[...]
    far = rb_ref[n_buckets - 1, h]
    out = jnp.zeros(idx.shape, F32)
    for b in range(n_buckets - 1):
        out = jnp.where(idx == b, rb_ref[b, h] - far, out)
    o_ref[...] = jnp.where(idx < 0, MASK_VALUE, out)


def _bias_expand(idx_np, rel_bias):
    rows, cols = idx_np.shape
    n_buckets, n_heads = rel_bias.shape
    return pl.pallas_call(
        functools.partial(_bias_expand_kernel, n_buckets=n_buckets),
        grid=(n_heads,),
        in_specs=[pl.BlockSpec(memory_space=pltpu.SMEM),
                  pl.BlockSpec((rows, cols), lambda h: (0, 0))],
        out_specs=pl.BlockSpec((None, rows, cols), lambda h: (h, 0, 0)),
        out_shape=jax.ShapeDtypeStruct((n_heads, rows, cols), F32),
        compiler_params=_cparams(("arbitrary",)),
        name="bias_expand",
    )(rel_bias, jnp.asarray(idx_np))


def _qkv_kernel(x_ref, w_ref, o16_ref, kv32_ref, *, n_q_tiles, q_scale):
    j = pl.program_id(1)
    acc = jnp.dot(x_ref[...].astype(BF16), w_ref[...].astype(BF16), preferred_element_type=F32)
    scale = jnp.where(j < n_q_tiles, q_scale, 1.0).astype(F32)
    o16_ref[...] = (acc * scale).astype(BF16)

    @pl.when(j >= n_q_tiles)
    def _():
        kv32_ref[...] = acc


def _qkv_proj(x, w_qkv, layer, q_scale):
    m, d = x.shape
    tm = _pick_tile(m, 1040, 16)
    tn = 512
    n_q_tiles = d // tn
    return pl.pallas_call(
        functools.partial(_qkv_kernel, n_q_tiles=n_q_tiles, q_scale=q_scale),
        grid=(m // tm, 3 * d // tn),
        in_specs=[pl.BlockSpec((tm, d), lambda i, j: (i, 0)),
                  pl.BlockSpec((None, d, tn), lambda i, j: (layer, 0, j))],
        out_specs=[pl.BlockSpec((tm, tn), lambda i, j: (i, j)),
                   pl.BlockSpec((tm, tn), lambda i, j: (i, jnp.maximum(j - n_q_tiles, 0)))],
        out_shape=[jax.ShapeDtypeStruct((m, 3 * d), BF16),
                   jax.ShapeDtypeStruct((m, 2 * d), F32)],
        compiler_params=_cparams(("arbitrary", "arbitrary")),
        name="qkv_proj",
    )(x, w_qkv)


def _convin_kernel(x_ref, wb_ref, wc_ref, wh_ref, bg_ref, u_ref):
    xb = x_ref[...].astype(BF16)
    bg_ref[...] = jnp.dot(xb, wb_ref[...].astype(BF16), preferred_element_type=F32)
    cg = jnp.dot(xb, wc_ref[...].astype(BF16), preferred_element_type=F32)
    hh = jnp.dot(xb, wh_ref[...].astype(BF16), preferred_element_type=F32)
    u_ref[...] = cg * hh


def _conv_in(x, w_in, layer):
    m, d = x.shape
    tm = _pick_tile(m, 1040, 16)
    tn = 256
    nt = d // tn
    wspec = lambda part: pl.BlockSpec((None, d, tn), lambda i, j: (layer, 0, part * nt + j))
    return pl.pallas_call(
        _convin_kernel,
        grid=(m // tm, nt),
        in_specs=[pl.BlockSpec((tm, d), lambda i, j: (i, 0)), wspec(0), wspec(1), wspec(2)],
        out_specs=[pl.BlockSpec((tm, tn), lambda i, j: (i, j)),
                   pl.BlockSpec((tm, tn), lambda i, j: (i, j))],
        out_shape=[jax.ShapeDtypeStruct((m, d), F32), jax.ShapeDtypeStruct((m, d), F32)],
        compiler_params=_cparams(("arbitrary", "arbitrary")),
        name="conv_in",
    )(x, w_in, w_in, w_in)


def _convgate_kernel(u_ref, bg_ref, halo_ref, meta_ref, h1_ref, h2_ref, w_ref, z_ref, *,
                     seq_blocks, n_prompt_blocks, n_batch, n_meta, dec_seq, n_sample_rows):
    i = pl.program_id(0)
    u = u_ref[...]
    rows = u.shape[0]
    r = lax.broadcasted_iota(jnp.int32, (rows, 1), 0)

    halo = halo_ref[...]
    prev2, prev1 = halo[V7X_SUBLANES - 2:V7X_SUBLANES - 1], halo[V7X_SUBLANES - 1:V7X_SUBLANES]
    meta = meta_ref[...]
    for b in range(n_batch):
        starts_batch = i == b * seq_blocks
        prev2 = jnp.where(starts_batch, meta[(b + 1) * n_meta - 2:(b + 1) * n_meta - 1], prev2)
        prev1 = jnp.where(starts_batch, meta[(b + 1) * n_meta - 1:(b + 1) * n_meta], prev1)

    p1 = jnp.where(r == 0, prev1, pltpu.roll(u, 1, axis=0))
    p2 = jnp.where(r == 0, prev2, jnp.where(r == 1, prev1, pltpu.roll(u, 2, axis=0)))

    small = i == n_prompt_blocks
    n_meta_rows = n_batch * n_meta
    in_meta = r < n_meta_rows
    in_sample = jnp.logical_and(r >= n_meta_rows, r < n_meta_rows + n_sample_rows)
    pos = jnp.where(in_meta, r % n_meta, (r - n_meta_rows) % dec_seq)
    restart = jnp.logical_or(in_meta, in_sample)
    m1 = jnp.logical_and(small, jnp.logical_and(restart, pos == 0))
    m2 = jnp.logical_and(small, jnp.logical_and(restart, pos <= 1))
    p1 = jnp.where(m1, h1_ref[...], p1)
    p2 = jnp.where(m2, h2_ref[...], p2)

    w = w_ref[...]
    conv = w[0:1] * p2 + w[1:2] * p1 + w[2:3] * u
    z_ref[...] = (bg_ref[...] * conv).astype(BF16)


def _conv_gate(u, bg, hist1, hist2, w_conv, layer, *, seq, n_batch, n_meta, dec_seq, n_sample_rows):
    m, d = u.shape
    rows = SMALL_ROWS
    n_prompt_blocks = n_batch * seq // rows
    assert m == (n_prompt_blocks + 1) * rows and seq % rows == 0
    halo_per_block = rows // V7X_SUBLANES
    kern = functools.partial(_convgate_kernel, seq_blocks=seq // rows, n_prompt_blocks=n_prompt_blocks,
                             n_batch=n_batch, n_meta=n_meta, dec_seq=dec_seq, n_sample_rows=n_sample_rows)
    return pl.pallas_call(
        kern,
        grid=(m // rows,),
        in_specs=[pl.BlockSpec((rows, d), lambda i: (i, 0)),
                  pl.BlockSpec((rows, d), lambda i: (i, 0)),
                  pl.BlockSpec((V7X_SUBLANES, d), lambda i: (jnp.maximum(i * halo_per_block - 1, 0), 0)),
                  pl.BlockSpec((rows, d), lambda i: (n_prompt_blocks, 0)),
                  pl.BlockSpec((rows, d), lambda i: (0, 0)),
                  pl.BlockSpec((rows, d), lambda i: (0, 0)),
                  pl.BlockSpec((None, 3, d), lambda i: (layer, 0, 0))],
        out_specs=pl.BlockSpec((rows, d), lambda i: (i, 0)),
        out_shape=jax.ShapeDtypeStruct((m, d), BF16),
        compiler_params=_cparams(("arbitrary",)),
        name="conv_gate",
    )(u, bg, u, u, hist1, hist2, w_conv)


def _residual_layer_norm(x_ref, o_ref, g_ref, b_ref, alpha, rows_chunk):
    g = g_ref[...]
    b = b_ref[...]

    def body(c, carry):
        rs = pl.ds(pl.multiple_of(c * rows_chunk, rows_chunk), rows_chunk)
        y = alpha * x_ref[rs, :] + o_ref[rs, :]
        mu = jnp.mean(y, axis=-1, keepdims=True)
        yc = y - mu
        var = jnp.mean(yc * yc, axis=-1, keepdims=True)
        o_ref[rs, :] = yc * lax.rsqrt(var + LN_EPS) * g + b
        return carry

    lax.fori_loop(0, x_ref.shape[0] // rows_chunk, body, 0)


def _outproj_kernel(a_ref, w_ref, x_ref, g_ref, b_ref, o_ref, *, alpha, rows_chunk):
    k = pl.program_id(1)
    part = jnp.dot(a_ref[...], w_ref[...].astype(BF16), preferred_element_type=F32)

    @pl.when(k == 0)
    def _():
        o_ref[...] = part

    @pl.when(k > 0)
    def _():
        o_ref[...] += part

    @pl.when(k == pl.num_programs(1) - 1)
    def _():
        _residual_layer_norm(x_ref, o_ref, g_ref, b_ref, alpha, rows_chunk)


def _outproj_ln(a16, w, layer, x, g, b, alpha):
    m, d = x.shape
    tm = _pick_tile(m, 832, 16)
    tk = 512
    rows_chunk = _pick_tile(tm, 64, V7X_SUBLANES)
    return pl.pallas_call(
        functools.partial(_outproj_kernel, alpha=alpha, rows_chunk=rows_chunk),
        grid=(m // tm, d // tk),
        in_specs=[pl.BlockSpec((tm, tk), lambda i, k: (i, k)),
                  pl.BlockSpec((None, tk, d), lambda i, k: (layer, k, 0)),
                  pl.BlockSpec((tm, d), lambda i, k: (i, 0)),
                  pl.BlockSpec((1, d), lambda i, k: (0, 0)),
                  pl.BlockSpec((1, d), lambda i, k: (0, 0))],
        out_specs=pl.BlockSpec((tm, d), lambda i, k: (i, 0)),
        out_shape=jax.ShapeDtypeStruct((m, d), F32),
        compiler_params=_cparams(("arbitrary", "arbitrary")),
        name="outproj_ln",
    )(a16, w, x, g, b)


def _mlp_kernel(x_ref, wu_ref, wd_ref, g_ref, b_ref, o_ref, xb_ref, *, alpha, rows_chunk):
    f = pl.program_id(1)

    @pl.when(f == 0)
    def _():
        xb_ref[...] = x_ref[...].astype(BF16)

    h = jnp.dot(xb_ref[...], wu_ref[...].astype(BF16), preferred_element_type=F32)
    h = jnp.square(jnp.maximum(h, 0.0)).astype(BF16)
    part = jnp.dot(h, wd_ref[...].astype(BF16), preferred_element_type=F32)

    @pl.when(f == 0)
    def _():
        o_ref[...] = part

    @pl.when(f > 0)
    def _():
        o_ref[...] += part

    @pl.when(f == pl.num_programs(1) - 1)
    def _():
        _residual_layer_norm(x_ref, o_ref, g_ref, b_ref, alpha, rows_chunk)


def _mlp_ln(x, w_up, w_down, layer, g, b, alpha):
    m, d = x.shape
    d_ff = w_up.shape[-1]
    tm = _pick_tile(m, 832, 16)
    tf = 512
    rows_chunk = _pick_tile(tm, 64, V7X_SUBLANES)
    return pl.pallas_call(
        functools.partial(_mlp_kernel, alpha=alpha, rows_chunk=rows_chunk),
        grid=(m // tm, d_ff // tf),
        in_specs=[pl.BlockSpec((tm, d), lambda i, f: (i, 0)),
                  pl.BlockSpec((None, d, tf), lambda i, f: (layer, 0, f)),
                  pl.BlockSpec((None, tf, d), lambda i, f: (layer, f, 0)),
                  pl.BlockSpec((1, d), lambda i, f: (0, 0)),
                  pl.BlockSpec((1, d), lambda i, f: (0, 0))],
        out_specs=pl.BlockSpec((tm, d), lambda i, f: (i, 0)),
        out_shape=jax.ShapeDtypeStruct((m, d), F32),
        scratch_shapes=[pltpu.VMEM((tm, d), BF16)],
        compiler_params=_cparams(("arbitrary", "arbitrary")),
        name="mlp_ln",
    )(x, w_up, w_down, g, b)


def _diff_lambda(lam_ref, lam_init):
    lp = lam_ref[...]
    a = jnp.sum(lp[0:1] * lp[1:2], axis=-1, keepdims=True)
    b = jnp.sum(lp[2:3] * lp[3:4], axis=-1, keepdims=True)
    return jnp.exp(a) - jnp.exp(b) + lam_init


def _sub_layer_norm(o, g_ref, lam_init):
    return o * lax.rsqrt(jnp.mean(o * o, axis=-1, keepdims=True) + SUBLN_EPS) * (g_ref[...] * (1.0 - lam_init))


def _dot_nt(a, b):
    return lax.dot_general(a, b, (((1,), (1,)), ((), ())), preferred_element_type=F32)


def _flash_kernel(q_ref, k_ref, v_ref, km_ref, vm_ref, qm_ref, bn_ref, bm_ref, bmm_ref, lam_ref, g_ref,
                  o_ref, om_ref, vt_sc, m_sc, l_sc, acc_sc, *, head_dim, lam_init):
    qi = pl.program_id(2)
    tq = q_ref.shape[0]
    ck = bn_ref.shape[1]
    ratio = tq // ck
    n_chunks = k_ref.shape[0] // ck
    n_meta = km_ref.shape[0]
    lam = _diff_lambda(lam_ref, lam_init)

    @pl.when(qi == 0)
    def _():
        def body(c, carry):
            rs = pl.ds(pl.multiple_of(c * ck, ck), ck)
            vt_sc[c] = v_ref[rs, :].astype(F32).T.astype(BF16)
            return carry

        lax.fori_loop(0, n_chunks, body, 0)
        vm_pad = jnp.concatenate([vm_ref[...].astype(F32), jnp.zeros((ck - n_meta, 2 * head_dim), F32)], axis=0)
        vt_sc[n_chunks] = vm_pad.T.astype(BF16)

    q = q_ref[...]
    q_maps = (q[:, :head_dim], q[:, head_dim:])

    def scores_t(keys, bias_t):
        parts = [_dot_nt(keys[:, m * head_dim:(m + 1) * head_dim], q_maps[m]) for m in range(2)]
        if bias_t is not None:
            parts = [p + bias_t for p in parts]
        return jnp.concatenate(parts, axis=1)

    km = km_ref[...]
    s = scores_t(km, jnp.where(qi == 0, bm_ref[...], 0.0))
    m0 = jnp.max(s, axis=0, keepdims=True)
    p = jnp.exp(s - m0)
    m_sc[...] = m0
    l_sc[...] = jnp.sum(p, axis=0, keepdims=True)
    p_pad = jnp.concatenate([p, jnp.zeros((V7X_LANES - n_meta, 2 * tq), F32)], axis=0).astype(BF16)
    acc_sc[...] = jnp.dot(vt_sc[n_chunks][:, :V7X_LANES], p_pad, preferred_element_type=F32)

    def chunk(c, bias_t):
        kb = k_ref[pl.ds(pl.multiple_of(c * ck, ck), ck), :]
        s = scores_t(kb, bias_t)
        m_old = m_sc[...]
        m_new = jnp.maximum(m_old, jnp.max(s, axis=0, keepdims=True))
        alpha = jnp.exp(m_old - m_new)
        p = jnp.exp(s - m_new)
        l_sc[...] = alpha * l_sc[...] + jnp.sum(p, axis=0, keepdims=True)
        acc_sc[...] = alpha * acc_sc[...] + jnp.dot(vt_sc[c], p.astype(BF16), preferred_element_type=F32)
        m_sc[...] = m_new

    def far_body(c, carry):
        chunk(c, None)
        return carry

    lax.fori_loop(0, jnp.maximum(ratio * qi - 1, 0), far_body, 0)

    for t in range(ratio + 1):
        c = ratio * qi - 1 + t
        if t == 0:
            @pl.when(qi > 0)
            def _():
                chunk(c, bn_ref[0])
        else:
            chunk(c, bn_ref[t])

    on = acc_sc[...] * (1.0 / l_sc[...])
    o_t = on[:, :tq] - lam * on[:, tq:]
    o_t = o_t * lax.rsqrt(jnp.mean(o_t * o_t, axis=0, keepdims=True) + SUBLN_EPS)
    o_ref[...] = (o_t.T * (g_ref[...] * (1.0 - lam_init))).astype(BF16)

    @pl.when(qi == 0)
    def _():
        qm = qm_ref[...]
        vm = vm_ref[...]
        sm = jnp.concatenate([_dot_nt(qm[:, m * head_dim:(m + 1) * head_dim],
                                      km[:, m * head_dim:(m + 1) * head_dim]) + bmm_ref[...]
                              for m in range(2)], axis=0)
        pm = jnp.exp(sm - jnp.max(sm, axis=-1, keepdims=True))
        onm = jnp.dot(pm.astype(BF16), vm, preferred_element_type=F32) * (1.0 / jnp.sum(pm, axis=-1, keepdims=True))
        om_ref[...] = _sub_layer_norm(onm[:n_meta] - lam * onm[n_meta:], g_ref, lam_init)


def _flash_prompt(qkv16, bias_near, bias_meta, bias_mm, lam_params, subln, layer, lam_init, *,
                  n_batch, seq, n_heads, head_dim, n_meta):
    m, d3 = qkv16.shape
    d = d3 // 3
    vd = 2 * head_dim
    tq = FLASH_TQ
    nq = seq // tq
    mp = n_batch * seq
    meta_blk = mp // n_meta
    kern = functools.partial(_flash_kernel, head_dim=head_dim, lam_init=lam_init)
    return pl.pallas_call(
        kern,
        grid=(n_batch, n_heads, nq),
        in_specs=[pl.BlockSpec((tq, vd), lambda b, h, qi: (b * nq + qi, h)),
                  pl.BlockSpec((seq, vd), lambda b, h, qi: (b, n_heads + h)),
                  pl.BlockSpec((seq, vd), lambda b, h, qi: (b, 2 * n_heads + h)),
                  pl.BlockSpec((n_meta, vd), lambda b, h, qi: (meta_blk + b, n_heads + h)),
                  pl.BlockSpec((n_meta, vd), lambda b, h, qi: (meta_blk + b, 2 * n_heads + h)),
                  pl.BlockSpec((n_meta, vd), lambda b, h, qi: (meta_blk + b, h)),
                  pl.BlockSpec((None,) + bias_near.shape[1:], lambda b, h, qi: (h, 0, 0, 0)),
                  pl.BlockSpec((None,) + bias_meta.shape[1:], lambda b, h, qi: (h, 0, 0)),
                  pl.BlockSpec((None,) + bias_mm.shape[1:], lambda b, h, qi: (h, 0, 0)),
                  pl.BlockSpec((None, 4, head_dim), lambda b, h, qi: (layer, 0, 0)),
                  pl.BlockSpec((None, 1, vd), lambda b, h, qi: (layer, 0, 0))],
        out_specs=[pl.BlockSpec((tq, vd), lambda b, h, qi: (b * nq + qi, h)),
                   pl.BlockSpec((n_meta, vd), lambda b, h, qi: (b, h))],
        out_shape=[jax.ShapeDtypeStruct((m, d), BF16),
                   jax.ShapeDtypeStruct((n_batch * n_meta, d), F32)],
        scratch_shapes=[pltpu.VMEM((seq // FLASH_CK + 1, vd, FLASH_CK), BF16),
                        pltpu.VMEM((1, 2 * tq), F32), pltpu.VMEM((1, 2 * tq), F32),
                        pltpu.VMEM((vd, 2 * tq), F32)],
        compiler_params=_cparams(("arbitrary", "arbitrary", "arbitrary")),
        name="flash_prompt",
    )(qkv16, qkv16, qkv16, qkv16, qkv16, qkv16, bias_near, bias_meta, bias_mm, lam_params, subln)


def _decode_kernel(pt_ref, q_ref, kn_ref, vn_ref, *rest, n_heads, head_dim, n_pages_step, lam_init):
    kp_refs = rest[:n_pages_step]
    vp_refs = rest[n_pages_step:2 * n_pages_step]
    bd_ref, bnw_ref, lam_ref, g_ref, o_ref, qb_sc, k16_sc, v16_sc, m_sc, l_sc, acc_sc = rest[2 * n_pages_step:]
    st = pl.program_id(1)
    last = pl.num_programs(1) - 1
    dec_seq, d = q_ref.shape
    vd = 2 * head_dim
    grp = 2 * dec_seq

    def pv(p, v16):
        return jnp.concatenate(
            [jnp.dot(p[h * grp:(h + 1) * grp].astype(BF16), v16[:, h * vd:(h + 1) * vd],
                     preferred_element_type=F32) for h in range(n_heads)], axis=0)

    @pl.when(st == 0)
    def _():
        qt = jnp.concatenate([q_ref[...]] * (2 * n_heads), axis=0)
        r = lax.broadcasted_iota(jnp.int32, qt.shape, 0) // dec_seq
        c = lax.broadcasted_iota(jnp.int32, qt.shape, 1) // head_dim
        qb = jnp.where(r == c, qt, 0.0).astype(BF16)
        qb_sc[...] = qb
        pad = jnp.zeros((dec_seq, d), F32)
        kn = jnp.concatenate([kn_ref[...], pad], axis=0).astype(BF16)
        vn = jnp.concatenate([vn_ref[...], pad], axis=0).astype(BF16)
        s = _dot_nt(qb, kn) + bnw_ref[...]
        m0 = jnp.max(s, axis=-1, keepdims=True)
        p = jnp.exp(s - m0)
        m_sc[...] = m0
        l_sc[...] = jnp.sum(p, axis=-1, keepdims=True)
        acc_sc[...] = pv(p, vn)

    rows_per_token = 2 * n_heads
    page = kp_refs[0].shape[0] // rows_per_token
    for t in range(n_pages_step):
        rows = slice(t * page, (t + 1) * page)
        for hm in range(rows_per_token):
            k16_sc[rows, hm * head_dim:(hm + 1) * head_dim] = (
                kp_refs[t][pl.ds(hm, page, stride=rows_per_token), :].astype(BF16))
        for h in range(n_heads):
            for half in range(2):
                col = h * vd + half * head_dim
                v16_sc[rows, col:col + head_dim] = (
                    vp_refs[t][pl.ds(half * n_heads + h, page, stride=rows_per_token), :].astype(BF16))
    v16 = v16_sc[...]
    s = _dot_nt(qb_sc[...], k16_sc[...]) + jnp.where(st == last, bd_ref[...], 0.0)
    m_old = m_sc[...]
    m_new = jnp.maximum(m_old, jnp.max(s, axis=-1, keepdims=True))
    alpha = jnp.exp(m_old - m_new)
    p = jnp.exp(s - m_new)
    l_sc[...] = alpha * l_sc[...] + jnp.sum(p, axis=-1, keepdims=True)
    acc_sc[...] = alpha * acc_sc[...] + pv(p, v16)
    m_sc[...] = m_new

    @pl.when(st == last)
    def _():
        lam = _diff_lambda(lam_ref, lam_init)
        on = acc_sc[...] * (1.0 / l_sc[...])
        for h in range(n_heads):
            oh = on[h * grp:h * grp + dec_seq] - lam * on[h * grp + dec_seq:(h + 1) * grp]
            o_ref[:, h * vd:(h + 1) * vd] = _sub_layer_norm(oh, g_ref, lam_init)


def _decode_attn(q32, kv32, cache_k, cache_v, page_table, bias_past, bias_new, lam_params, subln, layer,
                 lam_init, *, new_row0, n_heads, head_dim):
    n_rows, d = q32.shape
    dec_batch, n_pages = page_table.shape
    dec_seq = n_rows // dec_batch
    n_attn, n_pool, page = cache_k.shape[:3]
    ck = cache_k.reshape(n_attn, n_pool, page * 2 * n_heads, head_dim)
    cv = cache_v.reshape(n_attn, n_pool, page, n_heads, 2, head_dim).transpose(0, 1, 2, 4, 3, 5).reshape(
        n_attn, n_pool, page * 2 * n_heads, head_dim)
    nps = DECODE_PAGES
    assert n_pages % nps == 0 and new_row0 % dec_seq == 0
    n_steps = n_pages // nps
    vd = 2 * head_dim
    rows = 2 * n_heads * dec_seq
    new_blk = new_row0 // dec_seq

    def page_spec(t, arr):
        return pl.BlockSpec((None, None) + arr.shape[2:],
                            lambda bd, st, pt: (layer, pt[bd * n_pages + st * nps + t], 0, 0))

    grid_spec = pltpu.PrefetchScalarGridSpec(
        num_scalar_prefetch=1,
        grid=(dec_batch, n_steps),
        in_specs=[pl.BlockSpec((dec_seq, d), lambda bd, st, pt: (bd, 0)),
                  pl.BlockSpec((dec_seq, d), lambda bd, st, pt: (new_blk + bd, 0)),
                  pl.BlockSpec((dec_seq, d), lambda bd, st, pt: (new_blk + bd, 1))]
                 + [page_spec(t, ck) for t in range(nps)] + [page_spec(t, cv) for t in range(nps)]
                 + [pl.BlockSpec(bias_past.shape, lambda bd, st, pt: (0, 0)),
                    pl.BlockSpec(bias_new.shape, lambda bd, st, pt: (0, 0)),
                    pl.BlockSpec((None, 4, head_dim), lambda bd, st, pt: (layer, 0, 0)),
                    pl.BlockSpec((None, 1, vd), lambda bd, st, pt: (layer, 0, 0))],
        out_specs=pl.BlockSpec((dec_seq, d), lambda bd, st, pt: (bd, 0)),
        scratch_shapes=[pltpu.VMEM((rows, d), BF16), pltpu.VMEM((nps * page, d), BF16),
                        pltpu.VMEM((nps * page, d), BF16), pltpu.VMEM((rows, 1), F32),
                        pltpu.VMEM((rows, 1), F32), pltpu.VMEM((rows, vd), F32)],
    )
    kern = functools.partial(_decode_kernel, n_heads=n_heads, head_dim=head_dim, n_pages_step=nps,
                             lam_init=lam_init)
    return pl.pallas_call(
        kern,
        grid_spec=grid_spec,
        out_shape=jax.ShapeDtypeStruct((n_rows, d), F32),
        compiler_params=_cparams(("arbitrary", "arbitrary")),
        name="decode_attn",
    )(page_table.reshape(-1), q32, kv32, kv32, *([ck] * nps), *([cv] * nps),
      bias_past, bias_new, lam_params, subln)


def kernel(x_prompt, x_sample, cache_k, cache_v, state_conv, page_table, meta_tokens, rel_bias,
           attn_w_qkv, attn_lambda, attn_subln, attn_w_o, conv_w_in, conv_w, conv_w_out,
           mlp_w_up, mlp_w_down, ln_mix_g, ln_mix_b, ln_mlp_g, ln_mlp_b):
    n_batch, seq, d = x_prompt.shape
    dec_batch, dec_seq, _ = x_sample.shape
    n_meta = meta_tokens.shape[0]
    n_buckets, n_heads = rel_bias.shape
    head_dim = cache_k.shape[-1]
    vd = cache_v.shape[-1]
    page = cache_k.shape[2]
    past = page_table.shape[1] * page
    depth = mlp_w_up.shape[0]
    assert d == n_heads * vd and vd == 2 * head_dim
    alpha = (2 * depth) ** 0.25

    mp = n_batch * seq
    n_meta_rows = n_batch * n_meta
    n_sample_rows = dec_batch * dec_seq
    assert n_meta_rows + n_sample_rows <= SMALL_ROWS and seq % FLASH_TQ == 0
    assert n_meta % V7X_SUBLANES == 0 and dec_seq % V7X_SUBLANES == 0
    meta0, samp0 = mp, mp + n_meta_rows
    mt = mp + SMALL_ROWS

    x = jnp.concatenate(
        [x_prompt.reshape(mp, d)] + [meta_tokens.astype(F32)] * n_batch
        + [x_sample.reshape(n_sample_rows, d), jnp.zeros((mt - samp0 - n_sample_rows, d), F32)], axis=0)

    far = _far_distance(n_buckets)
    tq, ck, nps = FLASH_TQ, FLASH_CK, DECODE_PAGES
    ratio = tq // ck
    assert ck + 1 >= far and n_meta + tq - (n_meta - 1) >= far and (nps - 1) * page + 1 >= far
    qq = np.arange(tq)[None, :]
    near_idx = np.stack([_bucket_tile(qq - (t - 1) * ck - np.arange(ck)[:, None], n_buckets)
                         for t in range(ratio + 1)])
    meta_idx = _bucket_tile(n_meta + qq - np.arange(n_meta)[:, None], n_buckets)
    mm_idx = _bucket_tile(np.arange(n_meta)[:, None] - np.arange(n_meta)[None, :], n_buckets)
    step_of_row = (np.arange(2 * dec_seq) % dec_seq)[:, None]
    past_idx = _bucket_tile(nps * page + step_of_row - np.arange(nps * page)[None, :], n_buckets)
    new_cols = np.arange(2 * dec_seq)[None, :]
    new_idx = np.where(new_cols < dec_seq, _bucket_tile(step_of_row - new_cols, n_buckets), -1).astype(np.int32)

    bias_near = _bias_expand(near_idx.reshape((ratio + 1) * ck, tq), rel_bias).reshape(
        n_heads, ratio + 1, ck, tq)
    bias_meta = _bias_expand(meta_idx, rel_bias)
    bias_mm = _bias_expand(mm_idx, rel_bias)
    bias_past = _bias_expand(past_idx, rel_bias).reshape(n_heads * 2 * dec_seq, nps * page)
    bias_new = _bias_expand(new_idx, rel_bias).reshape(n_heads * 2 * dec_seq, 2 * dec_seq)

    def conv_history(state):
        h1 = jnp.zeros((dec_batch, dec_seq, d), F32).at[:, 0].set(state[:, 1])
        h2 = jnp.zeros((dec_batch, dec_seq, d), F32).at[:, 0].set(state[:, 0]).at[:, 1].set(state[:, 1])
        pad_lo = jnp.zeros((n_meta_rows, d), F32)
        pad_hi = jnp.zeros((SMALL_ROWS - n_meta_rows - n_sample_rows, d), F32)
        cat = lambda h: jnp.concatenate([pad_lo, h.reshape(n_sample_rows, d), pad_hi], axis=0)
        return cat(h1), cat(h2)

    subln3 = attn_subln.reshape(attn_subln.shape[0], 1, vd)
    k_p, v_p, c_p, k_s, v_s, c_s = [], [], [], [], [], []
    for i in range(depth):
        j = i // N_MIXERS
        g_mix, b_mix = ln_mix_g[i][None], ln_mix_b[i][None]
        if i % N_MIXERS == 0:
            lam0 = _lambda_init(i)
            qkv16, kv32 = _qkv_proj(x, attn_w_qkv, j, head_dim ** -0.5)
            o16, o_meta = _flash_prompt(qkv16, bias_near, bias_meta, bias_mm, attn_lambda, subln3, j, lam0,
                                        n_batch=n_batch, seq=seq, n_heads=n_heads, head_dim=head_dim,
                                        n_meta=n_meta)
            q32 = qkv16[samp0:samp0 + n_sample_rows, :d].astype(F32)
            o_samp = _decode_attn(q32, kv32, cache_k, cache_v, page_table, bias_past, bias_new, attn_lambda,
                                  subln3, j, lam0, new_row0=samp0, n_heads=n_heads, head_dim=head_dim)
            o_small = jnp.concatenate(
                [o_meta, o_samp, jnp.zeros((mt - samp0 - n_sample_rows, d), F32)], axis=0).astype(BF16)
            a16 = lax.dynamic_update_slice(o16, o_small, (mp, 0))
            x = _outproj_ln(a16, attn_w_o, j, x, g_mix, b_mix, alpha)

            k32 = kv32[:, :d]
            v32 = kv32[:, d:]
            for arr, dst_p, dst_s, tail in ((k32, k_p, k_s, (n_heads, 2, head_dim)),
                                            (v32, v_p, v_s, (n_heads, vd))):
                prm = arr[:mp].reshape(n_batch, seq, d)
                met = arr[meta0:samp0].reshape(n_batch, n_meta, d)
                dst_p.append(jnp.concatenate([met, prm], axis=1).reshape((n_batch, n_meta + seq) + tail))
                dst_s.append(arr[samp0:samp0 + n_sample_rows].reshape((dec_batch, dec_seq) + tail))
        else:
            bg, u = _conv_in(x, conv_w_in, j)
            hist1, hist2 = conv_history(state_conv[j])
            z16 = _conv_gate(u, bg, hist1, hist2, conv_w, j, seq=seq, n_batch=n_batch, n_meta=n_meta,
                             dec_seq=dec_seq, n_sample_rows=n_sample_rows)
            x = _outproj_ln(z16, conv_w_out, j, x, g_mix, b_mix, alpha)
            c_p.append(u[:mp].reshape(n_batch, seq, d)[:, seq - 2:])
            c_s.append(u[samp0:samp0 + n_sample_rows].reshape(dec_batch, dec_seq, d)[:, dec_seq - 2:])
        x = _mlp_ln(x, mlp_w_up, mlp_w_down, i, ln_mlp_g[i][None], ln_mlp_b[i][None], alpha)

    y_prompt = x[:mp].reshape(n_batch, seq, d)
    y_sample = x[samp0:samp0 + n_sample_rows].reshape(dec_batch, dec_seq, d)
    return (y_prompt, y_sample, jnp.stack(k_p), jnp.stack(v_p), jnp.stack(c_p),
            jnp.stack(k_s), jnp.stack(v_s), jnp.stack(c_s))
```

```python
import functools
import math

import numpy as np
import jax
import jax.numpy as jnp
from jax import lax
from jax.experimental import pallas as pl
from jax.experimental.pallas import tpu as pltpu

F32 = jnp.float32
BF16 = jnp.bfloat16

MAX_DISTANCE = 128
LN_EPS = 1e-5
SUBLN_EPS = 1e-5
N_MIXERS = 2
MASK_VALUE = -1e30

V7X_LANES = 128
V7X_SUBLANES = 8
V7X_VMEM_LIMIT_BYTES = 60 * 1024 * 1024

SMALL_ROWS = 128
FLASH_TQ = 512
FLASH_CK = 256
DECODE_PAGES = 4


def _lambda_init(layer):
    return 0.8 - 0.6 * math.exp(-0.3 * layer)


def _pick_tile(total, target, multiple):
    best = None
    for t in range(multiple, min(total, target) + 1, multiple):
        if total % t == 0:
            best = t
    assert best is not None, (total, target, multiple)
    return best


def _cparams(sem):
    return pltpu.CompilerParams(dimension_semantics=sem, vmem_limit_bytes=V7X_VMEM_LIMIT_BYTES)


def _bucket_of_distance(n, n_buckets):
    n = np.asarray(n, np.int64)
    max_exact = n_buckets // 2
    nf = np.maximum(n, 1).astype(np.float64)
    large = max_exact + (np.log(nf / max_exact) / math.log(MAX_DISTANCE / max_exact)
                         * (n_buckets - max_exact)).astype(np.int64)
    large = np.minimum(large, n_buckets - 1)
    return np.where(n < max_exact, n, large).astype(np.int32)


def _far_distance(n_buckets):
    tab = _bucket_of_distance(np.arange(4 * MAX_DISTANCE), n_buckets)
    not_last = np.nonzero(tab != n_buckets - 1)[0]
    return int(not_last[-1]) + 1


def _bucket_tile(dist, n_buckets):
    dist = np.asarray(dist, np.int64)
    return np.where(dist < 0, -1, _bucket_of_distance(np.maximum(dist, 0), n_buckets)).astype(np.int32)


def _bias_expand_kernel(rb_ref, idx_ref, o_ref, *, n_buckets):
    h = pl.program_id(0)
    idx = idx_ref[...]
    far = rb_ref[n_buckets - 1, h]
    out = jnp.zeros(idx.shape, F32)
    for b in range(n_buckets - 1):
        out = jnp.where(idx == b, rb_ref[b, h] - far, out)
    o_ref[...] = jnp.where(idx < 0, MASK_VALUE, out)


def _bias_expand(idx_np, rel_bias):
    rows, cols = idx_np.shape
    n_buckets, n_heads = rel_bias.shape
    return pl.pallas_call(
        functools.partial(_bias_expand_kernel, n_buckets=n_buckets),
        grid=(n_heads,),
        in_specs=[pl.BlockSpec(memory_space=pltpu.SMEM),
                  pl.BlockSpec((rows, cols), lambda h: (0, 0))],
        out_specs=pl.BlockSpec((None, rows, cols), lambda h: (h, 0, 0)),
        out_shape=jax.ShapeDtypeStruct((n_heads, rows, cols), F32),
        compiler_params=_cparams(("arbitrary",)),
        name="bias_expand",
    )(rel_bias, jnp.asarray(idx_np))


def _qkv_kernel(x_ref, w_ref, o16_ref, kv32_ref, *, n_q_tiles, q_scale):
    j = pl.program_id(1)
    acc = jnp.dot(x_ref[...].astype(BF16), w_ref[...].astype(BF16), preferred_element_type=F32)
    scale = jnp.where(j < n_q_tiles, q_scale, 1.0).astype(F32)
    o16_ref[...] = (acc * scale).astype(BF16)

    @pl.when(j >= n_q_tiles)
    def _():
        kv32_ref[...] = acc


def _qkv_proj(x, w_qkv, layer, q_scale):
    m, d = x.shape
    tm = _pick_tile(m, 1040, 16)
    tn = 512
    n_q_tiles = d // tn
    return pl.pallas_call(
        functools.partial(_qkv_kernel, n_q_tiles=n_q_tiles, q_scale=q_scale),
        grid=(m // tm, 3 * d // tn),
        in_specs=[pl.BlockSpec((tm, d), lambda i, j: (i, 0)),
                  pl.BlockSpec((None, d, tn), lambda i, j: (layer, 0, j))],
        out_specs=[pl.BlockSpec((tm, tn), lambda i, j: (i, j)),
                   pl.BlockSpec((tm, tn), lambda i, j: (i, jnp.maximum(j - n_q_tiles, 0)))],
        out_shape=[jax.ShapeDtypeStruct((m, 3 * d), BF16),
                   jax.ShapeDtypeStruct((m, 2 * d), F32)],
        compiler_params=_cparams(("arbitrary", "arbitrary")),
        name="qkv_proj",
    )(x, w_qkv)


def _convin_kernel(x_ref, wb_ref, wc_ref, wh_ref, bg_ref, u_ref):
    xb = x_ref[...].astype(BF16)
    bg_ref[...] = jnp.dot(xb, wb_ref[...].astype(BF16), preferred_element_type=F32)
    cg = jnp.dot(xb, wc_ref[...].astype(BF16), preferred_element_type=F32)
    hh = jnp.dot(xb, wh_ref[...].astype(BF16), preferred_element_type=F32)
    u_ref[...] = cg * hh


def _conv_in(x, w_in, layer):
    m, d = x.shape
    tm = _pick_tile(m, 1040, 16)
    tn = 256
    nt = d // tn
    wspec = lambda part: pl.BlockSpec((None, d, tn), lambda i, j: (layer, 0, part * nt + j))
    return pl.pallas_call(
        _convin_kernel,
        grid=(m // tm, nt),
        in_specs=[pl.BlockSpec((tm, d), lambda i, j: (i, 0)), wspec(0), wspec(1), wspec(2)],
        out_specs=[pl.BlockSpec((tm, tn), lambda i, j: (i, j)),
                   pl.BlockSpec((tm, tn), lambda i, j: (i, j))],
        out_shape=[jax.ShapeDtypeStruct((m, d), F32), jax.ShapeDtypeStruct((m, d), F32)],
        compiler_params=_cparams(("arbitrary", "arbitrary")),
        name="conv_in",
    )(x, w_in, w_in, w_in)


def _convgate_kernel(u_ref, bg_ref, halo_ref, meta_ref, h1_ref, h2_ref, w_ref, z_ref, *,
                     seq_blocks, n_prompt_blocks, n_batch, n_meta, dec_seq, n_sample_rows):
    i = pl.program_id(0)
    u = u_ref[...]
    rows = u.shape[0]
    r = lax.broadcasted_iota(jnp.int32, (rows, 1), 0)

    halo = halo_ref[...]
    prev2, prev1 = halo[V7X_SUBLANES - 2:V7X_SUBLANES - 1], halo[V7X_SUBLANES - 1:V7X_SUBLANES]
    meta = meta_ref[...]
    for b in range(n_batch):
        starts_batch = i == b * seq_blocks
        prev2 = jnp.where(starts_batch, meta[(b + 1) * n_meta - 2:(b + 1) * n_meta - 1], prev2)
        prev1 = jnp.where(starts_batch, meta[(b + 1) * n_meta - 1:(b + 1) * n_meta], prev1)

    p1 = jnp.where(r == 0, prev1, pltpu.roll(u, 1, axis=0))
    p2 = jnp.where(r == 0, prev2, jnp.where(r == 1, prev1, pltpu.roll(u, 2, axis=0)))

    small = i == n_prompt_blocks
    n_meta_rows = n_batch * n_meta
    in_meta = r < n_meta_rows
    in_sample = jnp.logical_and(r >= n_meta_rows, r < n_meta_rows + n_sample_rows)
    pos = jnp.where(in_meta, r % n_meta, (r - n_meta_rows) % dec_seq)
    restart = jnp.logical_or(in_meta, in_sample)
    m1 = jnp.logical_and(small, jnp.logical_and(restart, pos == 0))
    m2 = jnp.logical_and(small, jnp.logical_and(restart, pos <= 1))
    p1 = jnp.where(m1, h1_ref[...], p1)
    p2 = jnp.where(m2, h2_ref[...], p2)

    w = w_ref[...]
    conv = w[0:1] * p2 + w[1:2] * p1 + w[2:3] * u
    z_ref[...] = (bg_ref[...] * conv).astype(BF16)


def _conv_gate(u, bg, hist1, hist2, w_conv, layer, *, seq, n_batch, n_meta, dec_seq, n_sample_rows):
    m, d = u.shape
    rows = SMALL_ROWS
    n_prompt_blocks = n_batch * seq // rows
    assert m == (n_prompt_blocks + 1) * rows and seq % rows == 0
    halo_per_block = rows // V7X_SUBLANES
    kern = functools.partial(_convgate_kernel, seq_blocks=seq // rows, n_prompt_blocks=n_prompt_blocks,
                             n_batch=n_batch, n_meta=n_meta, dec_seq=dec_seq, n_sample_rows=n_sample_rows)
    return pl.pallas_call(
        kern,
        grid=(m // rows,),
        in_specs=[pl.BlockSpec((rows, d), lambda i: (i, 0)),
                  pl.BlockSpec((rows, d), lambda i: (i, 0)),
                  pl.BlockSpec((V7X_SUBLANES, d), lambda i: (jnp.maximum(i * halo_per_block - 1, 0), 0)),
                  pl.BlockSpec((rows, d), lambda i: (n_prompt_blocks, 0)),
                  pl.BlockSpec((rows, d), lambda i: (0, 0)),
                  pl.BlockSpec((rows, d), lambda i: (0, 0)),
                  pl.BlockSpec((None, 3, d), lambda i: (layer, 0, 0))],
        out_specs=pl.BlockSpec((rows, d), lambda i: (i, 0)),
        out_shape=jax.ShapeDtypeStruct((m, d), BF16),
        compiler_params=_cparams(("arbitrary",)),
        name="conv_gate",
    )(u, bg, u, u, hist1, hist2, w_conv)


def _residual_layer_norm(x_ref, o_ref, g_ref, b_ref, alpha, rows_chunk):
    g = g_ref[...]
    b = b_ref[...]

    def body(c, carry):
        rs = pl.ds(pl.multiple_of(c * rows_chunk, rows_chunk), rows_chunk)
        y = alpha * x_ref[rs, :] + o_ref[rs, :]
        mu = jnp.mean(y, axis=-1, keepdims=True)
        yc = y - mu
        var = jnp.mean(yc * yc, axis=-1, keepdims=True)
        o_ref[rs, :] = yc * lax.rsqrt(var + LN_EPS) * g + b
        return carry

    lax.fori_loop(0, x_ref.shape[0] // rows_chunk, body, 0)


def _outproj_kernel(a_ref, w_ref, x_ref, g_ref, b_ref, o_ref, *, alpha, rows_chunk):
    k = pl.program_id(1)
    @pl.when(k == 0)
    def _():
        o_ref[...] = jnp.zeros_like(o_ref)

    o_ref[...] += jnp.dot(a_ref[...], w_ref[...].astype(BF16), preferred_element_type=F32)

    @pl.when(k == pl.num_programs(1) - 1)
    def _():
        _residual_layer_norm(x_ref, o_ref, g_ref, b_ref, alpha, rows_chunk)


def _outproj_ln(a16, w, layer, x, g, b, alpha):
    m, d = x.shape
    tm = _pick_tile(m, 832, 16)
    tk = 512
    rows_chunk = _pick_tile(tm, 64, V7X_SUBLANES)
    return pl.pallas_call(
        functools.partial(_outproj_kernel, alpha=alpha, rows_chunk=rows_chunk),
        grid=(m // tm, d // tk),
        in_specs=[pl.BlockSpec((tm, tk), lambda i, k: (i, k)),
                  pl.BlockSpec((None, tk, d), lambda i, k: (layer, k, 0)),
                  pl.BlockSpec((tm, d), lambda i, k: (i, 0)),
                  pl.BlockSpec((1, d), lambda i, k: (0, 0)),
                  pl.BlockSpec((1, d), lambda i, k: (0, 0))],
        out_specs=pl.BlockSpec((tm, d), lambda i, k: (i, 0)),
        out_shape=jax.ShapeDtypeStruct((m, d), F32),
        compiler_params=_cparams(("arbitrary", "arbitrary")),
        name="outproj_ln",
    )(a16, w, x, g, b)


def _mlp_kernel(x_ref, wu_ref, wd_ref, g_ref, b_ref, o_ref, xb_ref, *, alpha, rows_chunk):
    f = pl.program_id(1)

    @pl.when(f == 0)
    def _():
        xb_ref[...] = x_ref[...].astype(BF16)
        o_ref[...] = jnp.zeros_like(o_ref)

    h = jnp.dot(xb_ref[...], wu_ref[...].astype(BF16), preferred_element_type=F32)
    h = jnp.square(jnp.maximum(h, 0.0)).astype(BF16)
    o_ref[...] += jnp.dot(h, wd_ref[...].astype(BF16), preferred_element_type=F32)

    @pl.when(f == pl.num_programs(1) - 1)
    def _():
        _residual_layer_norm(x_ref, o_ref, g_ref, b_ref, alpha, rows_chunk)


def _mlp_ln(x, w_up, w_down, layer, g, b, alpha):
    m, d = x.shape
    d_ff = w_up.shape[-1]
    tm = _pick_tile(m, 832, 16)
    tf = 512
    rows_chunk = _pick_tile(tm, 64, V7X_SUBLANES)
    return pl.pallas_call(
        functools.partial(_mlp_kernel, alpha=alpha, rows_chunk=rows_chunk),
        grid=(m // tm, d_ff // tf),
        in_specs=[pl.BlockSpec((tm, d), lambda i, f: (i, 0)),
                  pl.BlockSpec((None, d, tf), lambda i, f: (layer, 0, f)),
                  pl.BlockSpec((None, tf, d), lambda i, f: (layer, f, 0)),
                  pl.BlockSpec((1, d), lambda i, f: (0, 0)),
                  pl.BlockSpec((1, d), lambda i, f: (0, 0))],
        out_specs=pl.BlockSpec((tm, d), lambda i, f: (i, 0)),
        out_shape=jax.ShapeDtypeStruct((m, d), F32),
        scratch_shapes=[pltpu.VMEM((tm, d), BF16)],
        compiler_params=_cparams(("arbitrary", "arbitrary")),
        name="mlp_ln",
    )(x, w_up, w_down, g, b)


def _diff_lambda(lam_ref, lam_init):
    lp = lam_ref[...]
    a = jnp.sum(lp[0:1] * lp[1:2], axis=-1, keepdims=True)
    b = jnp.sum(lp[2:3] * lp[3:4], axis=-1, keepdims=True)
    return jnp.exp(a) - jnp.exp(b) + lam_init


def _sub_layer_norm(o, g_ref, lam_init):
    return o * lax.rsqrt(jnp.mean(o * o, axis=-1, keepdims=True) + SUBLN_EPS) * (g_ref[...] * (1.0 - lam_init))


def _dot_nt(a, b):
    return lax.dot_general(a, b, (((1,), (1,)), ((), ())), preferred_element_type=F32)


def _flash_kernel(q_ref, k_ref, v_ref, km_ref, vm_ref, qm_ref, bn_ref, bm_ref, bmm_ref, lam_ref, g_ref,
                  o_ref, om_ref, vt_sc, m_sc, l_sc, acc_sc, *, head_dim, lam_init):
    qi = pl.program_id(2)
    tq = q_ref.shape[0]
    ck = bn_ref.shape[1]
    ratio = tq // ck
    n_chunks = k_ref.shape[0] // ck
    n_meta = km_ref.shape[0]
    lam = _diff_lambda(lam_ref, lam_init)

    @pl.when(qi == 0)
    def _():
        def body(c, carry):
            rs = pl.ds(pl.multiple_of(c * ck, ck), ck)
            vt_sc[c] = v_ref[rs, :].astype(F32).T.astype(BF16)
            return carry

        lax.fori_loop(0, n_chunks, body, 0)
        vm_pad = jnp.concatenate([vm_ref[...].astype(F32), jnp.zeros((ck - n_meta, 2 * head_dim), F32)], axis=0)
        vt_sc[n_chunks] = vm_pad.T.astype(BF16)

    q = q_ref[...]
    q_maps = (q[:, :head_dim], q[:, head_dim:])

    def scores_t(keys, bias_t):
        parts = [_dot_nt(keys[:, m * head_dim:(m + 1) * head_dim], q_maps[m]) for m in range(2)]
        if bias_t is not None:
            parts = [p + bias_t for p in parts]
        return jnp.concatenate(parts, axis=1)

    km = km_ref[...]
    s = scores_t(km, jnp.where(qi == 0, bm_ref[...], 0.0))
    m0 = jnp.max(s, axis=0, keepdims=True)
    p = jnp.exp(s - m0)
    m_sc[...] = m0
    l_sc[...] = jnp.sum(p, axis=0, keepdims=True)
    p_pad = jnp.concatenate([p, jnp.zeros((V7X_LANES - n_meta, 2 * tq), F32)], axis=0).astype(BF16)
    acc_sc[...] = jnp.dot(vt_sc[n_chunks][:, :V7X_LANES], p_pad, preferred_element_type=F32)

    def chunk(c, bias_t):
        kb = k_ref[pl.ds(pl.multiple_of(c * ck, ck), ck), :]
        s = scores_t(kb, bias_t)
        m_old = m_sc[...]
        m_new = jnp.maximum(m_old, jnp.max(s, axis=0, keepdims=True))
        alpha = jnp.exp(m_old - m_new)
        p = jnp.exp(s - m_new)
        l_sc[...] = alpha * l_sc[...] + jnp.sum(p, axis=0, keepdims=True)
        acc_sc[...] = alpha * acc_sc[...] + jnp.dot(vt_sc[c], p.astype(BF16), preferred_element_type=F32)
        m_sc[...] = m_new

    def far_body(j, carry):
        chunk(2 * j, None)
        chunk(2 * j + 1, None)
        return carry

    assert ratio % 2 == 0
    n_far = ratio * qi - 1
    lax.fori_loop(0, jnp.maximum((ratio // 2) * qi - 1, 0), far_body, 0)

    @pl.when(qi > 0)
    def _():
        chunk(n_far - 1, None)
        chunk(n_far, bn_ref[0])

    for t in range(1, ratio + 1):
        chunk(n_far + t, bn_ref[t])

    on = acc_sc[...] * (1.0 / l_sc[...])
    o_t = on[:, :tq] - lam * on[:, tq:]
    o_t = o_t * lax.rsqrt(jnp.mean(o_t * o_t, axis=0, keepdims=True) + SUBLN_EPS)
    o_ref[...] = (o_t.T * (g_ref[...] * (1.0 - lam_init))).astype(BF16)

    @pl.when(qi == 0)
    def _():
        qm = qm_ref[...]
        vm = vm_ref[...]
        sm = jnp.concatenate([_dot_nt(qm[:, m * head_dim:(m + 1) * head_dim],
                                      km[:, m * head_dim:(m + 1) * head_dim]) + bmm_ref[...]
                              for m in range(2)], axis=0)
        pm = jnp.exp(sm - jnp.max(sm, axis=-1, keepdims=True))
        onm = jnp.dot(pm.astype(BF16), vm, preferred_element_type=F32) * (1.0 / jnp.sum(pm, axis=-1, keepdims=True))
        om_ref[...] = _sub_layer_norm(onm[:n_meta] - lam * onm[n_meta:], g_ref, lam_init)


def _flash_prompt(qkv16, bias_near, bias_meta, bias_mm, lam_params, subln, layer, lam_init, *,
                  n_batch, seq, n_heads, head_dim, n_meta):
    m, d3 = qkv16.shape
    d = d3 // 3
    vd = 2 * head_dim
    tq = FLASH_TQ
    nq = seq // tq
    mp = n_batch * seq
    meta_blk = mp // n_meta
    kern = functools.partial(_flash_kernel, head_dim=head_dim, lam_init=lam_init)
    return pl.pallas_call(
        kern,
        grid=(n_batch, n_heads, nq),
        in_specs=[pl.BlockSpec((tq, vd), lambda b, h, qi: (b * nq + qi, h)),
                  pl.BlockSpec((seq, vd), lambda b, h, qi: (b, n_heads + h)),
                  pl.BlockSpec((seq, vd), lambda b, h, qi: (b, 2 * n_heads + h)),
                  pl.BlockSpec((n_meta, vd), lambda b, h, qi: (meta_blk + b, n_heads + h)),
                  pl.BlockSpec((n_meta, vd), lambda b, h, qi: (meta_blk + b, 2 * n_heads + h)),
                  pl.BlockSpec((n_meta, vd), lambda b, h, qi: (meta_blk + b, h)),
                  pl.BlockSpec((None,) + bias_near.shape[1:], lambda b, h, qi: (h, 0, 0, 0)),
                  pl.BlockSpec((None,) + bias_meta.shape[1:], lambda b, h, qi: (h, 0, 0)),
                  pl.BlockSpec((None,) + bias_mm.shape[1:], lambda b, h, qi: (h, 0, 0)),
                  pl.BlockSpec((None, 4, head_dim), lambda b, h, qi: (layer, 0, 0)),
                  pl.BlockSpec((None, 1, vd), lambda b, h, qi: (layer, 0, 0))],
        out_specs=[pl.BlockSpec((tq, vd), lambda b, h, qi: (b * nq + qi, h)),
                   pl.BlockSpec((n_meta, vd), lambda b, h, qi: (b, h))],
        out_shape=[jax.ShapeDtypeStruct((m, d), BF16),
                   jax.ShapeDtypeStruct((n_batch * n_meta, d), F32)],
        scratch_shapes=[pltpu.VMEM((seq // FLASH_CK + 1, vd, FLASH_CK), BF16),
                        pltpu.VMEM((1, 2 * tq), F32), pltpu.VMEM((1, 2 * tq), F32),
                        pltpu.VMEM((vd, 2 * tq), F32)],
        compiler_params=_cparams(("arbitrary", "arbitrary", "arbitrary")),
        name="flash_prompt",
    )(qkv16, qkv16, qkv16, qkv16, qkv16, qkv16, bias_near, bias_meta, bias_mm, lam_params, subln)


def _decode_kernel(pt_ref, q_ref, kn_ref, vn_ref, *rest, n_heads, head_dim, n_pages_step, lam_init):
    kp_refs = rest[:n_pages_step]
    vp_refs = rest[n_pages_step:2 * n_pages_step]
    bd_ref, bnw_ref, lam_ref, g_ref, o_ref, qc_sc, hm_sc, m_sc, l_sc, acc_sc = rest[2 * n_pages_step:]
    st = pl.program_id(1)
    last = pl.num_programs(1) - 1
    dec_seq, d = q_ref.shape
    vd = 2 * head_dim
    map_rows = n_heads * dec_seq
    n_groups = 2 * n_heads

    @pl.when(st == 0)
    def _():
        q = q_ref[...]
        for mp in range(2):
            qc_sc[mp] = jnp.concatenate(
                [q[:, (2 * h + mp) * head_dim:(2 * h + mp + 1) * head_dim] for h in range(n_heads)],
                axis=0).astype(BF16)
        r = lax.broadcasted_iota(jnp.int32, hm_sc.shape, 0)
        c = lax.broadcasted_iota(jnp.int32, hm_sc.shape, 1)
        hm_sc[...] = jnp.where((r // dec_seq) % n_heads == c % n_heads, 0.0, MASK_VALUE)

        qt = jnp.concatenate([q] * n_groups, axis=0)
        grp = lax.broadcasted_iota(jnp.int32, qt.shape, 0) // dec_seq
        col = lax.broadcasted_iota(jnp.int32, qt.shape, 1) // head_dim
        qb = jnp.where(col == (grp % n_heads) * 2 + grp // n_heads, qt, 0.0).astype(BF16)
        pad = jnp.zeros((dec_seq, d), F32)
        kn = jnp.concatenate([kn_ref[...], pad], axis=0).astype(BF16)
        vn = jnp.concatenate([vn_ref[...], pad], axis=0).astype(BF16)
        s = _dot_nt(qb, kn) + bnw_ref[...]
        m0 = jnp.max(s, axis=-1, keepdims=True)
        p = jnp.exp(s - m0)
        m_sc[...] = m0
        l_sc[...] = jnp.sum(p, axis=-1, keepdims=True)
        full = jnp.dot(p.astype(BF16), vn, preferred_element_type=F32)
        acc_sc[...] = jnp.concatenate(
            [full[g * dec_seq:(g + 1) * dec_seq, (g % n_heads) * vd:(g % n_heads + 1) * vd]
             for g in range(n_groups)], axis=0)

    th_rows = vp_refs[0].shape[0]
    s = jnp.concatenate(
        [_dot_nt(qc_sc[mp], jnp.concatenate([kp[pl.ds(mp, th_rows, stride=2), :] for kp in kp_refs],
                                            axis=0).astype(BF16)) for mp in range(2)], axis=0)
    s = s + jnp.tile(hm_sc[...], (1, s.shape[1] // hm_sc.shape[1])) + jnp.where(st == last, bd_ref[...], 0.0)
    v16 = jnp.concatenate([vp[...].astype(BF16) for vp in vp_refs], axis=0)
    m_old = m_sc[...]
    m_new = jnp.maximum(m_old, jnp.max(s, axis=-1, keepdims=True))
    alpha = jnp.exp(m_old - m_new)
    p = jnp.exp(s - m_new)
    l_sc[...] = alpha * l_sc[...] + jnp.sum(p, axis=-1, keepdims=True)
    acc_sc[...] = alpha * acc_sc[...] + jnp.dot(p.astype(BF16), v16, preferred_element_type=F32)
    m_sc[...] = m_new

    @pl.when(st == last)
    def _():
        lam = _diff_lambda(lam_ref, lam_init)
        on = acc_sc[...] * (1.0 / l_sc[...])
        for h in range(n_heads):
            oh = (on[h * dec_seq:(h + 1) * dec_seq]
                  - lam * on[map_rows + h * dec_seq:map_rows + (h + 1) * dec_seq])
            o_ref[:, h * vd:(h + 1) * vd] = _sub_layer_norm(oh, g_ref, lam_init)


def _decode_attn(q32, kv32, cache_k, cache_v, page_table, bias_past, bias_new, lam_params, subln, layer,
                 lam_init, *, new_row0, n_heads, head_dim):
    n_rows, d = q32.shape
    dec_batch, n_pages = page_table.shape
    dec_seq = n_rows // dec_batch
    n_attn, n_pool, page = cache_k.shape[:3]
    ck = cache_k.reshape(n_attn, n_pool, page * 2 * n_heads, head_dim)
    cv = cache_v.reshape(n_attn, n_pool, page * n_heads, 2 * head_dim)
    nps = DECODE_PAGES
    assert n_pages % nps == 0 and new_row0 % dec_seq == 0
    n_steps = n_pages // nps
    vd = 2 * head_dim
    rows = 2 * n_heads * dec_seq
    new_blk = new_row0 // dec_seq

    def page_spec(t, arr):
        return pl.BlockSpec((None, None) + arr.shape[2:],
                            lambda bd, st, pt: (layer, pt[bd * n_pages + st * nps + t], 0, 0))

    grid_spec = pltpu.PrefetchScalarGridSpec(
        num_scalar_prefetch=1,
        grid=(dec_batch, n_steps),
        in_specs=[pl.BlockSpec((dec_seq, d), lambda bd, st, pt: (bd, 0)),
                  pl.BlockSpec((dec_seq, d), lambda bd, st, pt: (new_blk + bd, 0)),
                  pl.BlockSpec((dec_seq, d), lambda bd, st, pt: (new_blk + bd, 1))]
                 + [page_spec(t, ck) for t in range(nps)] + [page_spec(t, cv) for t in range(nps)]
                 + [pl.BlockSpec(bias_past.shape, lambda bd, st, pt: (0, 0)),
                    pl.BlockSpec(bias_new.shape, lambda bd, st, pt: (0, 0)),
                    pl.BlockSpec((None, 4, head_dim), lambda bd, st, pt: (layer, 0, 0)),
                    pl.BlockSpec((None, 1, vd), lambda bd, st, pt: (layer, 0, 0))],
        out_specs=pl.BlockSpec((dec_seq, d), lambda bd, st, pt: (bd, 0)),
        scratch_shapes=[pltpu.VMEM((2, rows // 2, head_dim), BF16), pltpu.VMEM((rows, V7X_LANES), F32),
                        pltpu.VMEM((rows, 1), F32), pltpu.VMEM((rows, 1), F32),
                        pltpu.VMEM((rows, vd), F32)],
    )
    kern = functools.partial(_decode_kernel, n_heads=n_heads, head_dim=head_dim, n_pages_step=nps,
                             lam_init=lam_init)
    return pl.pallas_call(
        kern,
        grid_spec=grid_spec,
        out_shape=jax.ShapeDtypeStruct((n_rows, d), F32),
        compiler_params=_cparams(("arbitrary", "arbitrary")),
        name="decode_attn",
    )(page_table.reshape(-1), q32, kv32, kv32, *([ck] * nps), *([cv] * nps),
      bias_past, bias_new, lam_params, subln)


def kernel(x_prompt, x_sample, cache_k, cache_v, state_conv, page_table, meta_tokens, rel_bias,
           attn_w_qkv, attn_lambda, attn_subln, attn_w_o, conv_w_in, conv_w, conv_w_out,
           mlp_w_up, mlp_w_down, ln_mix_g, ln_mix_b, ln_mlp_g, ln_mlp_b):
    n_batch, seq, d = x_prompt.shape
    dec_batch, dec_seq, _ = x_sample.shape
    n_meta = meta_tokens.shape[0]
    n_buckets, n_heads = rel_bias.shape
    head_dim = cache_k.shape[-1]
    vd = cache_v.shape[-1]
    page = cache_k.shape[2]
    past = page_table.shape[1] * page
    depth = mlp_w_up.shape[0]
    assert d == n_heads * vd and vd == 2 * head_dim
    alpha = (2 * depth) ** 0.25

    mp = n_batch * seq
    n_meta_rows = n_batch * n_meta
    n_sample_rows = dec_batch * dec_seq
    assert n_meta_rows + n_sample_rows <= SMALL_ROWS and seq % FLASH_TQ == 0
    assert n_meta % V7X_SUBLANES == 0 and dec_seq % V7X_SUBLANES == 0
    meta0, samp0 = mp, mp + n_meta_rows
    mt = mp + SMALL_ROWS

    x = jnp.concatenate(
        [x_prompt.reshape(mp, d)] + [meta_tokens.astype(F32)] * n_batch
        + [x_sample.reshape(n_sample_rows, d), jnp.zeros((mt - samp0 - n_sample_rows, d), F32)], axis=0)

    far = _far_distance(n_buckets)
    tq, ck, nps = FLASH_TQ, FLASH_CK, DECODE_PAGES
    ratio = tq // ck
    assert ck + 1 >= far and n_meta + tq - (n_meta - 1) >= far and (nps - 1) * page + 1 >= far
    qq = np.arange(tq)[None, :]
    near_idx = np.stack([_bucket_tile(qq - (t - 1) * ck - np.arange(ck)[:, None], n_buckets)
                         for t in range(ratio + 1)])
    meta_idx = _bucket_tile(n_meta + qq - np.arange(n_meta)[:, None], n_buckets)
    mm_idx = _bucket_tile(np.arange(n_meta)[:, None] - np.arange(n_meta)[None, :], n_buckets)
    step_of_row = (np.arange(2 * dec_seq) % dec_seq)[:, None]
    past_token = (np.arange(nps * page * n_heads) // n_heads)[None, :]
    past_idx = _bucket_tile(nps * page + step_of_row - past_token, n_buckets)
    new_cols = np.arange(2 * dec_seq)[None, :]
    new_idx = np.where(new_cols < dec_seq, _bucket_tile(step_of_row - new_cols, n_buckets), -1).astype(np.int32)

    bias_near = _bias_expand(near_idx.reshape((ratio + 1) * ck, tq), rel_bias).reshape(
        n_heads, ratio + 1, ck, tq)
    bias_meta = _bias_expand(meta_idx, rel_bias)
    bias_mm = _bias_expand(mm_idx, rel_bias)
    def decode_rows(tile):
        lanes = tile.shape[-1]
        return tile.reshape(n_heads, 2, dec_seq, lanes).transpose(1, 0, 2, 3).reshape(2 * n_heads * dec_seq, lanes)

    bias_past = decode_rows(_bias_expand(past_idx, rel_bias))
    bias_new = decode_rows(_bias_expand(new_idx, rel_bias))

    def conv_history(state):
        h1 = jnp.zeros((dec_batch, dec_seq, d), F32).at[:, 0].set(state[:, 1])
        h2 = jnp.zeros((dec_batch, dec_seq, d), F32).at[:, 0].set(state[:, 0]).at[:, 1].set(state[:, 1])
        pad_lo = jnp.zeros((n_meta_rows, d), F32)
        pad_hi = jnp.zeros((SMALL_ROWS - n_meta_rows - n_sample_rows, d), F32)
        cat = lambda h: jnp.concatenate([pad_lo, h.reshape(n_sample_rows, d), pad_hi], axis=0)
        return cat(h1), cat(h2)

    subln3 = attn_subln.reshape(attn_subln.shape[0], 1, vd)
    k_p, v_p, c_p, k_s, v_s, c_s = [], [], [], [], [], []
    for i in range(depth):
        j = i // N_MIXERS
        g_mix, b_mix = ln_mix_g[i][None], ln_mix_b[i][None]
        if i % N_MIXERS == 0:
            lam0 = _lambda_init(i)
            qkv16, kv32 = _qkv_proj(x, attn_w_qkv, j, head_dim ** -0.5)
            o16, o_meta = _flash_prompt(qkv16, bias_near, bias_meta, bias_mm, attn_lambda, subln3, j, lam0,
                                        n_batch=n_batch, seq=seq, n_heads=n_heads, head_dim=head_dim,
                                        n_meta=n_meta)
            q32 = qkv16[samp0:samp0 + n_sample_rows, :d].astype(F32)
            o_samp = _decode_attn(q32, kv32, cache_k, cache_v, page_table, bias_past, bias_new, attn_lambda,
                                  subln3, j, lam0, new_row0=samp0, n_heads=n_heads, head_dim=head_dim)
            o_small = jnp.concatenate(
                [o_meta, o_samp, jnp.zeros((mt - samp0 - n_sample_rows, d), F32)], axis=0).astype(BF16)
            a16 = lax.dynamic_update_slice(o16, o_small, (mp, 0))
            x = _outproj_ln(a16, attn_w_o, j, x, g_mix, b_mix, alpha)

            k32 = kv32[:, :d]
            v32 = kv32[:, d:]
            for arr, dst_p, dst_s, tail in ((k32, k_p, k_s, (n_heads, 2, head_dim)),
                                            (v32, v_p, v_s, (n_heads, vd))):
                prm = arr[:mp].reshape(n_batch, seq, d)
                met = arr[meta0:samp0].reshape(n_batch, n_meta, d)
                dst_p.append(jnp.concatenate([met, prm], axis=1).reshape((n_batch, n_meta + seq) + tail))
                dst_s.append(arr[samp0:samp0 + n_sample_rows].reshape((dec_batch, dec_seq) + tail))
        else:
            bg, u = _conv_in(x, conv_w_in, j)
            hist1, hist2 = conv_history(state_conv[j])
            z16 = _conv_gate(u, bg, hist1, hist2, conv_w, j, seq=seq, n_batch=n_batch, n_meta=n_meta,
                             dec_seq=dec_seq, n_sample_rows=n_sample_rows)
            x = _outproj_ln(z16, conv_w_out, j, x, g_mix, b_mix, alpha)
            c_p.append(u[:mp].reshape(n_batch, seq, d)[:, seq - 2:])
            c_s.append(u[samp0:samp0 + n_sample_rows].reshape(dec_batch, dec_seq, d)[:, dec_seq - 2:])
        x = _mlp_ln(x, mlp_w_up, mlp_w_down, i, ln_mlp_g[i][None], ln_mlp_b[i][None], alpha)

    y_prompt = x[:mp].reshape(n_batch, seq, d)
    y_sample = x[samp0:samp0 + n_sample_rows].reshape(dec_batch, dec_seq, d)
    return (y_prompt, y_sample, jnp.stack(k_p), jnp.stack(v_p), jnp.stack(c_p),
            jnp.stack(k_s), jnp.stack(v_s), jnp.stack(c_s))
```

```python
import functools
import math

import numpy as np
import jax
import jax.numpy as jnp
from jax import lax
from jax.experimental import pallas as pl
from jax.experimental.pallas import tpu as pltpu

F32 = jnp.float32
BF16 = jnp.bfloat16

MAX_DISTANCE = 128
LN_EPS = 1e-5
SUBLN_EPS = 1e-5
N_MIXERS = 2
MASK_VALUE = -1e30
LOG2_E = math.log2(math.e)

V7X_LANES = 128
V7X_SUBLANES = 8
V7X_VMEM_LIMIT_BYTES = 60 * 1024 * 1024

SMALL_ROWS = 128
FLASH_TQ = 512
FLASH_CK = 512
DECODE_PAGES = 4


def _lambda_init(layer):
    return 0.8 - 0.6 * math.exp(-0.3 * layer)


def _pick_tile(total, target, multiple):
    best = None
    for t in range(multiple, min(total, target) + 1, multiple):
        if total % t == 0:
            best = t
    assert best is not None, (total, target, multiple)
    return best


def _cparams(sem):
    return pltpu.CompilerParams(dimension_semantics=sem, vmem_limit_bytes=V7X_VMEM_LIMIT_BYTES)


def _bucket_of_distance(n, n_buckets):
    n = np.asarray(n, np.int64)
    max_exact = n_buckets // 2
    nf = np.maximum(n, 1).astype(np.float64)
    large = max_exact + (np.log(nf / max_exact) / math.log(MAX_DISTANCE / max_exact)
                         * (n_buckets - max_exact)).astype(np.int64)
    large = np.minimum(large, n_buckets - 1)
    return np.where(n < max_exact, n, large).astype(np.int32)


def _far_distance(n_buckets):
    tab = _bucket_of_distance(np.arange(4 * MAX_DISTANCE), n_buckets)
    not_last = np.nonzero(tab != n_buckets - 1)[0]
    return int(not_last[-1]) + 1


def _bucket_tile(dist, n_buckets):
    dist = np.asarray(dist, np.int64)
    return np.where(dist < 0, -1, _bucket_of_distance(np.maximum(dist, 0), n_buckets)).astype(np.int32)


def _bias_expand_kernel(rb_ref, idx_ref, o_ref, *, n_buckets):
    h = pl.program_id(0)
    idx = idx_ref[...]
    far = rb_ref[n_buckets - 1, h]
    out = jnp.zeros(idx.shape, F32)
    for b in range(n_buckets - 1):
        out = jnp.where(idx == b, (rb_ref[b, h] - far) * LOG2_E, out)
    o_ref[...] = jnp.where(idx < 0, MASK_VALUE, out)


def _bias_expand(idx_np, rel_bias):
    rows, cols = idx_np.shape
    n_buckets, n_heads = rel_bias.shape
    return pl.pallas_call(
        functools.partial(_bias_expand_kernel, n_buckets=n_buckets),
        grid=(n_heads,),
        in_specs=[pl.BlockSpec(memory_space=pltpu.SMEM),
                  pl.BlockSpec((rows, cols), lambda h: (0, 0))],
        out_specs=pl.BlockSpec((None, rows, cols), lambda h: (h, 0, 0)),
        out_shape=jax.ShapeDtypeStruct((n_heads, rows, cols), F32),
        compiler_params=_cparams(("arbitrary",)),
        name="bias_expand",
    )(rel_bias, jnp.asarray(idx_np))


def _qkv_kernel(x_ref, w_ref, o16_ref, kv32_ref, xb_ref, *, n_q_tiles, q_scale):
    j = pl.program_id(1)

    @pl.when(j == 0)
    def _():
        xb_ref[...] = x_ref[...].astype(BF16)

    acc = jnp.dot(xb_ref[...], w_ref[...].astype(BF16), preferred_element_type=F32)
    scale = jnp.where(j < n_q_tiles, q_scale, 1.0).astype(F32)
    o16_ref[...] = (acc * scale).astype(BF16)

    @pl.when(j >= n_q_tiles)
    def _():
        kv32_ref[...] = acc


def _qkv_proj(x, w_qkv, layer, q_scale):
    m, d = x.shape
    tm = _pick_tile(m, 1040, 16)
    tn = 512
    n_q_tiles = d // tn
    return pl.pallas_call(
        functools.partial(_qkv_kernel, n_q_tiles=n_q_tiles, q_scale=q_scale),
        grid=(m // tm, 3 * d // tn),
        in_specs=[pl.BlockSpec((tm, d), lambda i, j: (i, 0)),
                  pl.BlockSpec((None, d, tn), lambda i, j: (layer, 0, j))],
        out_specs=[pl.BlockSpec((tm, tn), lambda i, j: (i, j)),
                   pl.BlockSpec((tm, tn), lambda i, j: (i, jnp.maximum(j - n_q_tiles, 0)))],
        out_shape=[jax.ShapeDtypeStruct((m, 3 * d), BF16),
                   jax.ShapeDtypeStruct((m, 2 * d), F32)],
        scratch_shapes=[pltpu.VMEM((tm, d), BF16)],
        compiler_params=_cparams(("arbitrary", "arbitrary")),
        name="qkv_proj",
    )(x, w_qkv)


def _convin_kernel(x_ref, wb_ref, wc_ref, wh_ref, bg_ref, u_ref, xb_ref):
    @pl.when(pl.program_id(1) == 0)
    def _():
        xb_ref[...] = x_ref[...].astype(BF16)

    xb = xb_ref[...]
    bg_ref[...] = jnp.dot(xb, wb_ref[...].astype(BF16), preferred_element_type=F32)
    cg = jnp.dot(xb, wc_ref[...].astype(BF16), preferred_element_type=F32)
    hh = jnp.dot(xb, wh_ref[...].astype(BF16), preferred_element_type=F32)
    u_ref[...] = cg * hh


def _conv_in(x, w_in, layer):
    m, d = x.shape
    tm = _pick_tile(m, 1040, 16)
    tn = 256
    nt = d // tn
    wspec = lambda part: pl.BlockSpec((None, d, tn), lambda i, j: (layer, 0, part * nt + j))
    return pl.pallas_call(
        _convin_kernel,
        grid=(m // tm, nt),
        in_specs=[pl.BlockSpec((tm, d), lambda i, j: (i, 0)), wspec(0), wspec(1), wspec(2)],
        out_specs=[pl.BlockSpec((tm, tn), lambda i, j: (i, j)),
                   pl.BlockSpec((tm, tn), lambda i, j: (i, j))],
        out_shape=[jax.ShapeDtypeStruct((m, d), F32), jax.ShapeDtypeStruct((m, d), F32)],
        scratch_shapes=[pltpu.VMEM((tm, d), BF16)],
        compiler_params=_cparams(("arbitrary", "arbitrary")),
        name="conv_in",
    )(x, w_in, w_in, w_in)


def _convgate_kernel(u_ref, bg_ref, halo_ref, meta_ref, h1_ref, h2_ref, w_ref, z_ref, *,
                     seq_blocks, n_prompt_blocks, n_batch, n_meta, dec_seq, n_sample_rows):
    i = pl.program_id(0)
    u = u_ref[...]
    rows = u.shape[0]
    r = lax.broadcasted_iota(jnp.int32, (rows, 1), 0)

    halo = halo_ref[...]
    prev2, prev1 = halo[V7X_SUBLANES - 2:V7X_SUBLANES - 1], halo[V7X_SUBLANES - 1:V7X_SUBLANES]
    meta = meta_ref[...]
    for b in range(n_batch):
        starts_batch = i == b * seq_blocks
        prev2 = jnp.where(starts_batch, meta[(b + 1) * n_meta - 2:(b + 1) * n_meta - 1], prev2)
        prev1 = jnp.where(starts_batch, meta[(b + 1) * n_meta - 1:(b + 1) * n_meta], prev1)

    p1 = jnp.where(r == 0, prev1, pltpu.roll(u, 1, axis=0))
    p2 = jnp.where(r == 0, prev2, jnp.where(r == 1, prev1, pltpu.roll(u, 2, axis=0)))

    small = i == n_prompt_blocks
    n_meta_rows = n_batch * n_meta
    in_meta = r < n_meta_rows
    in_sample = jnp.logical_and(r >= n_meta_rows, r < n_meta_rows + n_sample_rows)
    pos = jnp.where(in_meta, r % n_meta, (r - n_meta_rows) % dec_seq)
    restart = jnp.logical_or(in_meta, in_sample)
    m1 = jnp.logical_and(small, jnp.logical_and(restart, pos == 0))
    m2 = jnp.logical_and(small, jnp.logical_and(restart, pos <= 1))
    p1 = jnp.where(m1, h1_ref[...], p1)
    p2 = jnp.where(m2, h2_ref[...], p2)

    w = w_ref[...]
    conv = w[0:1] * p2 + w[1:2] * p1 + w[2:3] * u
    z_ref[...] = (bg_ref[...] * conv).astype(BF16)


def _conv_gate(u, bg, hist1, hist2, w_conv, layer, *, seq, n_batch, n_meta, dec_seq, n_sample_rows):
    m, d = u.shape
    rows = SMALL_ROWS
    n_prompt_blocks = n_batch * seq // rows
    assert m == (n_prompt_blocks + 1) * rows and seq % rows == 0
    halo_per_block = rows // V7X_SUBLANES
    kern = functools.partial(_convgate_kernel, seq_blocks=seq // rows, n_prompt_blocks=n_prompt_blocks,
                             n_batch=n_batch, n_meta=n_meta, dec_seq=dec_seq, n_sample_rows=n_sample_rows)
    return pl.pallas_call(
        kern,
        grid=(m // rows,),
        in_specs=[pl.BlockSpec((rows, d), lambda i: (i, 0)),
                  pl.BlockSpec((rows, d), lambda i: (i, 0)),
                  pl.BlockSpec((V7X_SUBLANES, d), lambda i: (jnp.maximum(i * halo_per_block - 1, 0), 0)),
                  pl.BlockSpec((rows, d), lambda i: (n_prompt_blocks, 0)),
                  pl.BlockSpec((rows, d), lambda i: (0, 0)),
                  pl.BlockSpec((rows, d), lambda i: (0, 0)),
                  pl.BlockSpec((None, 3, d), lambda i: (layer, 0, 0))],
        out_specs=pl.BlockSpec((rows, d), lambda i: (i, 0)),
        out_shape=jax.ShapeDtypeStruct((m, d), BF16),
        compiler_params=_cparams(("arbitrary",)),
        name="conv_gate",
    )(u, bg, u, u, hist1, hist2, w_conv)


def _residual_layer_norm(x_ref, o_ref, g_ref, b_ref, alpha, rows_chunk):
    g = g_ref[...]
    b = b_ref[...]

    def body(c, carry):
        rs = pl.ds(pl.multiple_of(c * rows_chunk, rows_chunk), rows_chunk)
        y = alpha * x_ref[rs, :] + o_ref[rs, :]
        mu = jnp.mean(y, axis=-1, keepdims=True)
        yc = y - mu
        var = jnp.mean(yc * yc, axis=-1, keepdims=True)
        o_ref[rs, :] = yc * lax.rsqrt(var + LN_EPS) * g + b
        return carry

    lax.fori_loop(0, x_ref.shape[0] // rows_chunk, body, 0)


def _outproj_kernel(a_ref, *rest, alpha, rows_chunk, n_main_last):
    small_ref = rest[0] if n_main_last is not None else None
    w_ref, x_ref, g_ref, b_ref, o_ref = rest[-5:]
    i = pl.program_id(0)
    k = pl.program_id(1)
    last_tile = pl.num_programs(0) - 1

    @pl.when(k == 0)
    def _():
        o_ref[...] = jnp.zeros_like(o_ref)

    w = w_ref[...].astype(BF16)
    if small_ref is None:
        o_ref[...] += jnp.dot(a_ref[...], w, preferred_element_type=F32)
    else:
        @pl.when(i < last_tile)
        def _():
            o_ref[...] += jnp.dot(a_ref[...], w, preferred_element_type=F32)

        @pl.when(i == last_tile)
        def _():
            a = jnp.concatenate([a_ref[:n_main_last], small_ref[...]], axis=0)
            o_ref[...] += jnp.dot(a, w, preferred_element_type=F32)

    @pl.when(k == pl.num_programs(1) - 1)
    def _():
        _residual_layer_norm(x_ref, o_ref, g_ref, b_ref, alpha, rows_chunk)


def _outproj_ln(a16, a16_small, w, layer, x, g, b, alpha):
    m, d = x.shape
    tm = _pick_tile(m, 832, 16)
    tk = 512
    rows_chunk = _pick_tile(tm, 64, V7X_SUBLANES)
    if a16_small is None:
        assert a16.shape[0] == m
        n_main_last, small_args, small_specs = None, [], []
    else:
        n_small = a16_small.shape[0]
        assert a16.shape[0] + n_small == m and n_small <= tm
        n_main_last = tm - n_small
        small_args = [a16_small]
        small_specs = [pl.BlockSpec((n_small, tk), lambda i, k: (0, k))]
    return pl.pallas_call(
        functools.partial(_outproj_kernel, alpha=alpha, rows_chunk=rows_chunk, n_main_last=n_main_last),
        grid=(m // tm, d // tk),
        in_specs=[pl.BlockSpec((tm, tk), lambda i, k: (i, k))] + small_specs + [
                  pl.BlockSpec((None, tk, d), lambda i, k: (layer, k, 0)),
                  pl.BlockSpec((tm, d), lambda i, k: (i, 0)),
                  pl.BlockSpec((1, d), lambda i, k: (0, 0)),
                  pl.BlockSpec((1, d), lambda i, k: (0, 0))],
        out_specs=pl.BlockSpec((tm, d), lambda i, k: (i, 0)),
        out_shape=jax.ShapeDtypeStruct((m, d), F32),
        compiler_params=_cparams(("arbitrary", "arbitrary")),
        name="outproj_ln",
    )(a16, *small_args, w, x, g, b)


def _mlp_kernel(x_ref, wu_ref, wd_ref, g_ref, b_ref, o_ref, *rest, alpha, rows_chunk, n_main_last):
    small_ref = rest[0] if n_main_last is not None else None
    xb_ref = rest[-1]
    f = pl.program_id(1)

    @pl.when(f == 0)
    def _():
        xb_ref[...] = x_ref[...].astype(BF16)
        o_ref[...] = jnp.zeros_like(o_ref)

    h = jnp.dot(xb_ref[...], wu_ref[...].astype(BF16), preferred_element_type=F32)
    h = jnp.square(jnp.maximum(h, 0.0)).astype(BF16)
    o_ref[...] += jnp.dot(h, wd_ref[...].astype(BF16), preferred_element_type=F32)

    @pl.when(f == pl.num_programs(1) - 1)
    def _():
        _residual_layer_norm(x_ref, o_ref, g_ref, b_ref, alpha, rows_chunk)
        if small_ref is not None:
            @pl.when(pl.program_id(0) == pl.num_programs(0) - 1)
            def _():
                small_ref[...] = o_ref[n_main_last:, :]


def _mlp_ln(x, w_up, w_down, layer, g, b, alpha, n_main=None):
    m, d = x.shape
    d_ff = w_up.shape[-1]
    tm = _pick_tile(m, 832, 16)
    tf = 512
    rows_chunk = _pick_tile(tm, 64, V7X_SUBLANES)
    out_specs = pl.BlockSpec((tm, d), lambda i, f: (i, 0))
    out_shape = jax.ShapeDtypeStruct((m, d), F32)
    n_main_last = None
    if n_main is not None:
        n_small = m - n_main
        assert 0 < n_small <= tm
        n_main_last = tm - n_small
        out_specs = [out_specs, pl.BlockSpec((n_small, d), lambda i, f: (0, 0))]
        out_shape = [jax.ShapeDtypeStruct((n_main, d), F32), jax.ShapeDtypeStruct((n_small, d), F32)]
    return pl.pallas_call(
        functools.partial(_mlp_kernel, alpha=alpha, rows_chunk=rows_chunk, n_main_last=n_main_last),
        grid=(m // tm, d_ff // tf),
        in_specs=[pl.BlockSpec((tm, d), lambda i, f: (i, 0)),
                  pl.BlockSpec((None, d, tf), lambda i, f: (layer, 0, f)),
                  pl.BlockSpec((None, tf, d), lambda i, f: (layer, f, 0)),
                  pl.BlockSpec((1, d), lambda i, f: (0, 0)),
                  pl.BlockSpec((1, d), lambda i, f: (0, 0))],
        out_specs=out_specs,
        out_shape=out_shape,
        scratch_shapes=[pltpu.VMEM((tm, d), BF16)],
        compiler_params=_cparams(("arbitrary", "arbitrary")),
        name="mlp_ln",
    )(x, w_up, w_down, g, b)


def _diff_lambda(lam_ref, lam_init):
    lp = lam_ref[...]
    a = jnp.sum(lp[0:1] * lp[1:2], axis=-1, keepdims=True)
    b = jnp.sum(lp[2:3] * lp[3:4], axis=-1, keepdims=True)
    return jnp.exp(a) - jnp.exp(b) + lam_init


def _sub_layer_norm(o, g_ref, lam_init):
    return o * lax.rsqrt(jnp.mean(o * o, axis=-1, keepdims=True) + SUBLN_EPS) * (g_ref[...] * (1.0 - lam_init))


def _dot_nt(a, b):
    return lax.dot_general(a, b, (((1,), (1,)), ((), ())), preferred_element_type=F32)


def _flash_kernel(q_ref, k_ref, v_ref, km_ref, vm_ref, qm_ref, bn_ref, bm_ref, bmm_ref, lam_ref, g_ref,
                  o_ref, om_ref, vt_sc, m_sc, l_sc, acc_sc, *, head_dim, lam_init):
    qi = pl.program_id(2)
    tq = q_ref.shape[0]
    ck = bn_ref.shape[1]
    ratio = tq // ck
    n_chunks = k_ref.shape[0] // ck
    n_meta = km_ref.shape[0]
    lam = _diff_lambda(lam_ref, lam_init)

    @pl.when(qi == 0)
    def _():
        def body(c, carry):
            rs = pl.ds(pl.multiple_of(c * ck, ck), ck)
            vt_sc[c] = v_ref[rs, :].astype(F32).T.astype(BF16)
            return carry

        lax.fori_loop(0, n_chunks, body, 0)
        vm_pad = jnp.concatenate([vm_ref[...].astype(F32), jnp.zeros((ck - n_meta, 2 * head_dim), F32)], axis=0)
        vt_sc[n_chunks] = vm_pad.T.astype(BF16)

    q = q_ref[...]
    q_maps = (q[:, :head_dim], q[:, head_dim:])

    def scores_t(keys, bias_t):
        parts = [_dot_nt(keys[:, m * head_dim:(m + 1) * head_dim], q_maps[m]) for m in range(2)]
        if bias_t is not None:
            parts = [p + bias_t for p in parts]
        return jnp.concatenate(parts, axis=1)

    km = km_ref[...]
    s = scores_t(km, jnp.where(qi == 0, bm_ref[...], 0.0))
    m0 = jnp.max(s, axis=0, keepdims=True)
    p = jnp.exp2(s - m0)
    m_sc[...] = m0
    l_sc[...] = jnp.sum(p, axis=0, keepdims=True)
    p_pad = jnp.concatenate([p, jnp.zeros((V7X_LANES - n_meta, 2 * tq), F32)], axis=0).astype(BF16)
    acc_sc[...] = jnp.dot(vt_sc[n_chunks][:, :V7X_LANES], p_pad, preferred_element_type=F32)

    def chunk(c, bias_t):
        kb = k_ref[pl.ds(pl.multiple_of(c * ck, ck), ck), :]
        s = scores_t(kb, bias_t)
        m_old = m_sc[...]
        m_new = jnp.maximum(m_old, jnp.max(s, axis=0, keepdims=True))
        alpha = jnp.exp2(m_old - m_new)
        p = jnp.exp2(s - m_new)
        l_sc[...] = alpha * l_sc[...] + jnp.sum(p, axis=0, keepdims=True)
        acc_sc[...] = alpha * acc_sc[...] + jnp.dot(vt_sc[c], p.astype(BF16), preferred_element_type=F32)
        m_sc[...] = m_new

    def far_body(c, carry):
        chunk(c, None)
        return carry

    n_far = ratio * qi - 1
    lax.fori_loop(0, jnp.maximum(n_far, 0), far_body, 0)

    @pl.when(qi > 0)
    def _():
        chunk(n_far, bn_ref[0])

    for t in range(1, ratio + 1):
        chunk(n_far + t, bn_ref[t])

    on = acc_sc[...] * (1.0 / l_sc[...])
    o_t = on[:, :tq] - lam * on[:, tq:]
    o_t = o_t * lax.rsqrt(jnp.mean(o_t * o_t, axis=0, keepdims=True) + SUBLN_EPS)
    o_ref[...] = (o_t.T * (g_ref[...] * (1.0 - lam_init))).astype(BF16)

    @pl.when(qi == 0)
    def _():
        qm = qm_ref[...]
        vm = vm_ref[...]
        sm = jnp.concatenate([_dot_nt(qm[:, m * head_dim:(m + 1) * head_dim],
                                      km[:, m * head_dim:(m + 1) * head_dim]) + bmm_ref[...]
                              for m in range(2)], axis=0)
        pm = jnp.exp2(sm - jnp.max(sm, axis=-1, keepdims=True))
        onm = jnp.dot(pm.astype(BF16), vm, preferred_element_type=F32) * (1.0 / jnp.sum(pm, axis=-1, keepdims=True))
        om_ref[...] = _sub_layer_norm(onm[:n_meta] - lam * onm[n_meta:], g_ref, lam_init)


def _flash_prompt(qkv16, bias_near, bias_meta, bias_mm, lam_params, subln, layer, lam_init, *,
                  n_batch, seq, n_heads, head_dim, n_meta):
    m, d3 = qkv16.shape
    d = d3 // 3
    vd = 2 * head_dim
    tq = FLASH_TQ
    nq = seq // tq
    mp = n_batch * seq
    meta_blk = mp // n_meta
    kern = functools.partial(_flash_kernel, head_dim=head_dim, lam_init=lam_init)
    return pl.pallas_call(
        kern,
        grid=(n_batch, n_heads, nq),
        in_specs=[pl.BlockSpec((tq, vd), lambda b, h, qi: (b * nq + qi, h)),
                  pl.BlockSpec((seq, vd), lambda b, h, qi: (b, n_heads + h)),
                  pl.BlockSpec((seq, vd), lambda b, h, qi: (b, 2 * n_heads + h)),
                  pl.BlockSpec((n_meta, vd), lambda b, h, qi: (meta_blk + b, n_heads + h)),
                  pl.BlockSpec((n_meta, vd), lambda b, h, qi: (meta_blk + b, 2 * n_heads + h)),
                  pl.BlockSpec((n_meta, vd), lambda b, h, qi: (meta_blk + b, h)),
                  pl.BlockSpec((None,) + bias_near.shape[1:], lambda b, h, qi: (h, 0, 0, 0)),
                  pl.BlockSpec((None,) + bias_meta.shape[1:], lambda b, h, qi: (h, 0, 0)),
                  pl.BlockSpec((None,) + bias_mm.shape[1:], lambda b, h, qi: (h, 0, 0)),
                  pl.BlockSpec((None, 4, head_dim), lambda b, h, qi: (layer, 0, 0)),
                  pl.BlockSpec((None, 1, vd), lambda b, h, qi: (layer, 0, 0))],
        out_specs=[pl.BlockSpec((tq, vd), lambda b, h, qi: (b * nq + qi, h)),
                   pl.BlockSpec((n_meta, vd), lambda b, h, qi: (b, h))],
        out_shape=[jax.ShapeDtypeStruct((mp, d), BF16),
                   jax.ShapeDtypeStruct((n_batch * n_meta, d), F32)],
        scratch_shapes=[pltpu.VMEM((seq // FLASH_CK + 1, vd, FLASH_CK), BF16),
                        pltpu.VMEM((1, 2 * tq), F32), pltpu.VMEM((1, 2 * tq), F32),
                        pltpu.VMEM((vd, 2 * tq), F32)],
        compiler_params=_cparams(("arbitrary", "arbitrary", "arbitrary")),
        name="flash_prompt",
    )(qkv16, qkv16, qkv16, qkv16, qkv16, qkv16, bias_near, bias_meta, bias_mm, lam_params, subln)


def _decode_kernel(pt_ref, q_ref, kn_ref, vn_ref, *rest, n_heads, head_dim, n_pages_step, lam_init):
    kp_refs = rest[:n_pages_step]
    vp_refs = rest[n_pages_step:2 * n_pages_step]
    bd_ref, bnw_ref, lam_ref, g_ref, o_ref, qc_sc, hm_sc, m_sc, l_sc, acc_sc = rest[2 * n_pages_step:]
    st = pl.program_id(1)
    last = pl.num_programs(1) - 1
    dec_seq, d = q_ref.shape
    vd = 2 * head_dim
    map_rows = n_heads * dec_seq
    n_groups = 2 * n_heads

    @pl.when(st == 0)
    def _():
        q = q_ref[...]
        for mp in range(2):
            qc_sc[mp] = jnp.concatenate(
                [q[:, (2 * h + mp) * head_dim:(2 * h + mp + 1) * head_dim] for h in range(n_heads)],
                axis=0).astype(BF16)
        r = lax.broadcasted_iota(jnp.int32, hm_sc.shape, 0)
        c = lax.broadcasted_iota(jnp.int32, hm_sc.shape, 1)
        hm_sc[...] = jnp.where((r // dec_seq) % n_heads == c % n_heads, 0.0, MASK_VALUE)

        qt = jnp.concatenate([q] * n_groups, axis=0)
        grp = lax.broadcasted_iota(jnp.int32, qt.shape, 0) // dec_seq
        col = lax.broadcasted_iota(jnp.int32, qt.shape, 1) // head_dim
        qb = jnp.where(col == (grp % n_heads) * 2 + grp // n_heads, qt, 0.0).astype(BF16)
        pad = jnp.zeros((dec_seq, d), F32)
        kn = jnp.concatenate([kn_ref[...], pad], axis=0).astype(BF16)
        vn = jnp.concatenate([vn_ref[...], pad], axis=0).astype(BF16)
        s = _dot_nt(qb, kn) + bnw_ref[...]
        m0 = jnp.max(s, axis=-1, keepdims=True)
        p = jnp.exp2(s - m0)
        m_sc[...] = m0
        l_sc[...] = jnp.sum(p, axis=-1, keepdims=True)
        full = jnp.dot(p.astype(BF16), vn, preferred_element_type=F32)
        acc_sc[...] = jnp.concatenate(
            [full[g * dec_seq:(g + 1) * dec_seq, (g % n_heads) * vd:(g % n_heads + 1) * vd]
             for g in range(n_groups)], axis=0)

    th_rows = vp_refs[0].shape[0]
    s = jnp.concatenate(
        [_dot_nt(qc_sc[mp], jnp.concatenate([kp[pl.ds(mp, th_rows, stride=2), :] for kp in kp_refs],
                                            axis=0).astype(BF16)) for mp in range(2)], axis=0)
    s = s + jnp.tile(hm_sc[...], (1, s.shape[1] // hm_sc.shape[1])) + jnp.where(st == last, bd_ref[...], 0.0)
    v16 = jnp.concatenate([vp[...].astype(BF16) for vp in vp_refs], axis=0)
    m_old = m_sc[...]
    m_new = jnp.maximum(m_old, jnp.max(s, axis=-1, keepdims=True))
    alpha = jnp.exp2(m_old - m_new)
    p = jnp.exp2(s - m_new)
    l_sc[...] = alpha * l_sc[...] + jnp.sum(p, axis=-1, keepdims=True)
    acc_sc[...] = alpha * acc_sc[...] + jnp.dot(p.astype(BF16), v16, preferred_element_type=F32)
    m_sc[...] = m_new

    @pl.when(st == last)
    def _():
        lam = _diff_lambda(lam_ref, lam_init)
        on = acc_sc[...] * (1.0 / l_sc[...])
        for h in range(n_heads):
            oh = (on[h * dec_seq:(h + 1) * dec_seq]
                  - lam * on[map_rows + h * dec_seq:map_rows + (h + 1) * dec_seq])
            o_ref[:, h * vd:(h + 1) * vd] = _sub_layer_norm(oh, g_ref, lam_init)


def _decode_attn(q32, kv32, cache_k, cache_v, page_table, bias_past, bias_new, lam_params, subln, layer,
                 lam_init, *, new_row0, n_heads, head_dim):
    n_rows, d = q32.shape
    dec_batch, n_pages = page_table.shape
    dec_seq = n_rows // dec_batch
    n_attn, n_pool, page = cache_k.shape[:3]
    ck = cache_k.reshape(n_attn, n_pool, page * 2 * n_heads, head_dim)
    cv = cache_v.reshape(n_attn, n_pool, page * n_heads, 2 * head_dim)
    nps = DECODE_PAGES
    assert n_pages % nps == 0 and new_row0 % dec_seq == 0
    n_steps = n_pages // nps
    vd = 2 * head_dim
    rows = 2 * n_heads * dec_seq
    new_blk = new_row0 // dec_seq

    def page_spec(t, arr):
        return pl.BlockSpec((None, None) + arr.shape[2:],
                            lambda bd, st, pt: (layer, pt[bd * n_pages + st * nps + t], 0, 0))

    grid_spec = pltpu.PrefetchScalarGridSpec(
        num_scalar_prefetch=1,
        grid=(dec_batch, n_steps),
        in_specs=[pl.BlockSpec((dec_seq, d), lambda bd, st, pt: (bd, 0)),
                  pl.BlockSpec((dec_seq, d), lambda bd, st, pt: (new_blk + bd, 0)),
                  pl.BlockSpec((dec_seq, d), lambda bd, st, pt: (new_blk + bd, 1))]
                 + [page_spec(t, ck) for t in range(nps)] + [page_spec(t, cv) for t in range(nps)]
                 + [pl.BlockSpec(bias_past.shape, lambda bd, st, pt: (0, 0)),
                    pl.BlockSpec(bias_new.shape, lambda bd, st, pt: (0, 0)),
                    pl.BlockSpec((None, 4, head_dim), lambda bd, st, pt: (layer, 0, 0)),
                    pl.BlockSpec((None, 1, vd), lambda bd, st, pt: (layer, 0, 0))],
        out_specs=pl.BlockSpec((dec_seq, d), lambda bd, st, pt: (bd, 0)),
        scratch_shapes=[pltpu.VMEM((2, rows // 2, head_dim), BF16), pltpu.VMEM((rows, V7X_LANES), F32),
                        pltpu.VMEM((rows, 1), F32), pltpu.VMEM((rows, 1), F32),
                        pltpu.VMEM((rows, vd), F32)],
    )
    kern = functools.partial(_decode_kernel, n_heads=n_heads, head_dim=head_dim, n_pages_step=nps,
                             lam_init=lam_init)
    return pl.pallas_call(
        kern,
        grid_spec=grid_spec,
        out_shape=jax.ShapeDtypeStruct((n_rows, d), F32),
        compiler_params=_cparams(("arbitrary", "arbitrary")),
        name="decode_attn",
    )(page_table.reshape(-1), q32, kv32, kv32, *([ck] * nps), *([cv] * nps),
      bias_past, bias_new, lam_params, subln)


def kernel(x_prompt, x_sample, cache_k, cache_v, state_conv, page_table, meta_tokens, rel_bias,
           attn_w_qkv, attn_lambda, attn_subln, attn_w_o, conv_w_in, conv_w, conv_w_out,
           mlp_w_up, mlp_w_down, ln_mix_g, ln_mix_b, ln_mlp_g, ln_mlp_b):
    n_batch, seq, d = x_prompt.shape
    dec_batch, dec_seq, _ = x_sample.shape
    n_meta = meta_tokens.shape[0]
    n_buckets, n_heads = rel_bias.shape
    head_dim = cache_k.shape[-1]
    vd = cache_v.shape[-1]
    page = cache_k.shape[2]
    depth = mlp_w_up.shape[0]
    assert d == n_heads * vd and vd == 2 * head_dim
    alpha = (2 * depth) ** 0.25

    mp = n_batch * seq
    n_meta_rows = n_batch * n_meta
    n_sample_rows = dec_batch * dec_seq
    assert n_meta_rows + n_sample_rows <= SMALL_ROWS and seq % FLASH_TQ == 0
    assert n_meta % V7X_SUBLANES == 0 and dec_seq % V7X_SUBLANES == 0
    meta0, samp0 = mp, mp + n_meta_rows
    mt = mp + SMALL_ROWS

    x = jnp.concatenate(
        [x_prompt.reshape(mp, d)] + [meta_tokens.astype(F32)] * n_batch
        + [x_sample.reshape(n_sample_rows, d), jnp.zeros((mt - samp0 - n_sample_rows, d), F32)], axis=0)

    far = _far_distance(n_buckets)
    tq, ck, nps = FLASH_TQ, FLASH_CK, DECODE_PAGES
    ratio = tq // ck
    assert ck + 1 >= far and n_meta + tq - (n_meta - 1) >= far and (nps - 1) * page + 1 >= far
    qq = np.arange(tq)[None, :]
    near_idx = np.stack([_bucket_tile(qq - (t - 1) * ck - np.arange(ck)[:, None], n_buckets)
                         for t in range(ratio + 1)])
    meta_idx = _bucket_tile(n_meta + qq - np.arange(n_meta)[:, None], n_buckets)
    mm_idx = _bucket_tile(np.arange(n_meta)[:, None] - np.arange(n_meta)[None, :], n_buckets)
    step_of_row = (np.arange(2 * dec_seq) % dec_seq)[:, None]
    past_token = (np.arange(nps * page * n_heads) // n_heads)[None, :]
    past_idx = _bucket_tile(nps * page + step_of_row - past_token, n_buckets)
    new_cols = np.arange(2 * dec_seq)[None, :]
    new_idx = np.where(new_cols < dec_seq, _bucket_tile(step_of_row - new_cols, n_buckets), -1).astype(np.int32)

    bias_near = _bias_expand(near_idx.reshape((ratio + 1) * ck, tq), rel_bias).reshape(
        n_heads, ratio + 1, ck, tq)
    bias_meta = _bias_expand(meta_idx, rel_bias)
    bias_mm = _bias_expand(mm_idx, rel_bias)

    def decode_rows(tile):
        lanes = tile.shape[-1]
        return tile.reshape(n_heads, 2, dec_seq, lanes).transpose(1, 0, 2, 3).reshape(2 * n_heads * dec_seq, lanes)

    bias_past = decode_rows(_bias_expand(past_idx, rel_bias))
    bias_new = decode_rows(_bias_expand(new_idx, rel_bias))

    def conv_history(state):
        h1 = jnp.zeros((dec_batch, dec_seq, d), F32).at[:, 0].set(state[:, 1])
        h2 = jnp.zeros((dec_batch, dec_seq, d), F32).at[:, 0].set(state[:, 0]).at[:, 1].set(state[:, 1])
        pad_lo = jnp.zeros((n_meta_rows, d), F32)
        pad_hi = jnp.zeros((SMALL_ROWS - n_meta_rows - n_sample_rows, d), F32)
        cat = lambda h: jnp.concatenate([pad_lo, h.reshape(n_sample_rows, d), pad_hi], axis=0)
        return cat(h1), cat(h2)

    subln3 = attn_subln.reshape(attn_subln.shape[0], 1, vd)
    kv_layers, c_p, c_s = [], [], []
    for i in range(depth):
        j = i // N_MIXERS
        g_mix, b_mix = ln_mix_g[i][None], ln_mix_b[i][None]
        if i % N_MIXERS == 0:
            lam0 = _lambda_init(i)
            qkv16, kv32 = _qkv_proj(x, attn_w_qkv, j, head_dim ** -0.5 * LOG2_E)
            o16, o_meta = _flash_prompt(qkv16, bias_near, bias_meta, bias_mm, attn_lambda, subln3, j, lam0,
                                        n_batch=n_batch, seq=seq, n_heads=n_heads, head_dim=head_dim,
                                        n_meta=n_meta)
            q32 = qkv16[samp0:samp0 + n_sample_rows, :d].astype(F32)
            o_samp = _decode_attn(q32, kv32, cache_k, cache_v, page_table, bias_past, bias_new, attn_lambda,
                                  subln3, j, lam0, new_row0=samp0, n_heads=n_heads, head_dim=head_dim)
            o_small = jnp.concatenate(
                [o_meta, o_samp, jnp.zeros((mt - samp0 - n_sample_rows, d), F32)], axis=0).astype(BF16)
            x = _outproj_ln(o16, o_small, attn_w_o, j, x, g_mix, b_mix, alpha)
            kv_layers.append(kv32)
        else:
            bg, u = _conv_in(x, conv_w_in, j)
            hist1, hist2 = conv_history(state_conv[j])
            z16 = _conv_gate(u, bg, hist1, hist2, conv_w, j, seq=seq, n_batch=n_batch, n_meta=n_meta,
                             dec_seq=dec_seq, n_sample_rows=n_sample_rows)
            x = _outproj_ln(z16, None, conv_w_out, j, x, g_mix, b_mix, alpha)
            c_p.append(jnp.stack([u[(b + 1) * seq - 2:(b + 1) * seq] for b in range(n_batch)]))
            c_s.append(u[samp0:samp0 + n_sample_rows].reshape(dec_batch, dec_seq, d)[:, dec_seq - 2:])
        last = i == depth - 1
        x = _mlp_ln(x, mlp_w_up, mlp_w_down, i, ln_mlp_g[i][None], ln_mlp_b[i][None], alpha,
                    n_main=mp if last else None)

    y_main, y_small = x
    y_prompt = y_main.reshape(n_batch, seq, d)
    y_sample = y_small[n_meta_rows:n_meta_rows + n_sample_rows].reshape(dec_batch, dec_seq, d)

    def new_cache(col0, tail):
        cols = slice(col0, col0 + d)
        prompt = jnp.concatenate(
            [jnp.concatenate(
                [jnp.concatenate([kv[meta0 + b * n_meta:meta0 + (b + 1) * n_meta, cols].reshape((1, 1, n_meta) + tail),
                                  kv[b * seq:(b + 1) * seq, cols].reshape((1, 1, seq) + tail)], axis=2)
                 for b in range(n_batch)], axis=1)
             for kv in kv_layers], axis=0)
        sample = jnp.stack([kv[samp0:samp0 + n_sample_rows, cols].reshape((dec_batch, dec_seq) + tail)
                            for kv in kv_layers])
        return prompt, sample

    k_p, k_s = new_cache(0, (n_heads, 2, head_dim))
    v_p, v_s = new_cache(d, (n_heads, vd))
    return (y_prompt, y_sample, k_p, v_p, jnp.stack(c_p), k_s, v_s, jnp.stack(c_s))
```

```python
import functools
import math

import numpy as np
import jax
import jax.numpy as jnp
from jax import lax
from jax.experimental import pallas as pl
from jax.experimental.pallas import tpu as pltpu

F32 = jnp.float32
BF16 = jnp.bfloat16

MAX_DISTANCE = 128
LN_EPS = 1e-5
SUBLN_EPS = 1e-5
N_MIXERS = 2
MASK_VALUE = -1e30
LOG2_E = math.log2(math.e)

V7X_LANES = 128
V7X_SUBLANES = 8
V7X_VMEM_LIMIT_BYTES = 60 * 1024 * 1024

SMALL_ROWS = 128
FLASH_TQ = 512
FLASH_CK = 512
FLASH_HEADS = 2
DECODE_PAGES = 4


def _lambda_init(layer):
    return 0.8 - 0.6 * math.exp(-0.3 * layer)


def _pick_tile(total, target, multiple):
    best = None
    for t in range(multiple, min(total, target) + 1, multiple):
        if total % t == 0:
            best = t
    assert best is not None, (total, target, multiple)
    return best


def _cparams(sem):
    return pltpu.CompilerParams(dimension_semantics=sem, vmem_limit_bytes=V7X_VMEM_LIMIT_BYTES)


def _bucket_of_distance(n, n_buckets):
    n = np.asarray(n, np.int64)
    max_exact = n_buckets // 2
    nf = np.maximum(n, 1).astype(np.float64)
    large = max_exact + (np.log(nf / max_exact) / math.log(MAX_DISTANCE / max_exact)
                         * (n_buckets - max_exact)).astype(np.int64)
    large = np.minimum(large, n_buckets - 1)
    return np.where(n < max_exact, n, large).astype(np.int32)


def _far_distance(n_buckets):
    tab = _bucket_of_distance(np.arange(4 * MAX_DISTANCE), n_buckets)
    not_last = np.nonzero(tab != n_buckets - 1)[0]
    return int(not_last[-1]) + 1


def _bucket_tile(dist, n_buckets):
    dist = np.asarray(dist, np.int64)
    return np.where(dist < 0, -1, _bucket_of_distance(np.maximum(dist, 0), n_buckets)).astype(np.int32)


def _bias_expand_kernel(rb_ref, idx_ref, o_ref, *, n_buckets):
    h = pl.program_id(0)
    idx = idx_ref[...]
    far = rb_ref[n_buckets - 1, h]
    out = jnp.zeros(idx.shape, F32)
    for b in range(n_buckets - 1):
        out = jnp.where(idx == b, (rb_ref[b, h] - far) * LOG2_E, out)
    o_ref[...] = jnp.where(idx < 0, MASK_VALUE, out)


def _bias_expand(idx_np, rel_bias):
    rows, cols = idx_np.shape
    n_buckets, n_heads = rel_bias.shape
    return pl.pallas_call(
        functools.partial(_bias_expand_kernel, n_buckets=n_buckets),
        grid=(n_heads,),
        in_specs=[pl.BlockSpec(memory_space=pltpu.SMEM),
                  pl.BlockSpec((rows, cols), lambda h: (0, 0))],
        out_specs=pl.BlockSpec((None, rows, cols), lambda h: (h, 0, 0)),
        out_shape=jax.ShapeDtypeStruct((n_heads, rows, cols), F32),
        compiler_params=_cparams(("arbitrary",)),
        name="bias_expand",
    )(rel_bias, jnp.asarray(idx_np))


def _qkv_kernel(x_ref, w_ref, o16_ref, kv32_ref, xb_ref, *, n_q_tiles, q_scale):
    j = pl.program_id(1)

    @pl.when(j == 0)
    def _():
        xb_ref[...] = x_ref[...].astype(BF16)

    acc = jnp.dot(xb_ref[...], w_ref[...].astype(BF16), preferred_element_type=F32)
    scale = jnp.where(j < n_q_tiles, q_scale, 1.0).astype(F32)
    o16_ref[...] = (acc * scale).astype(BF16)

    @pl.when(j >= n_q_tiles)
    def _():
        kv32_ref[...] = acc


def _qkv_proj(x, w_qkv, layer, q_scale):
    m, d = x.shape
    tm = _pick_tile(m, 1040, 16)
    tn = 512
    n_q_tiles = d // tn
    return pl.pallas_call(
        functools.partial(_qkv_kernel, n_q_tiles=n_q_tiles, q_scale=q_scale),
        grid=(m // tm, 3 * d // tn),
        in_specs=[pl.BlockSpec((tm, d), lambda i, j: (i, 0)),
                  pl.BlockSpec((None, d, tn), lambda i, j: (layer, 0, j))],
        out_specs=[pl.BlockSpec((tm, tn), lambda i, j: (i, j)),
                   pl.BlockSpec((tm, tn), lambda i, j: (i, jnp.maximum(j - n_q_tiles, 0)))],
        out_shape=[jax.ShapeDtypeStruct((m, 3 * d), BF16),
                   jax.ShapeDtypeStruct((m, 2 * d), F32)],
        scratch_shapes=[pltpu.VMEM((tm, d), BF16)],
        compiler_params=_cparams(("arbitrary", "arbitrary")),
        name="qkv_proj",
    )(x, w_qkv)


def _convin_kernel(x_ref, wb_ref, wc_ref, wh_ref, bg_ref, u_ref, xb_ref):
    @pl.when(pl.program_id(1) == 0)
    def _():
        xb_ref[...] = x_ref[...].astype(BF16)

    xb = xb_ref[...]
    bg_ref[...] = jnp.dot(xb, wb_ref[...].astype(BF16), preferred_element_type=F32)
    cg = jnp.dot(xb, wc_ref[...].astype(BF16), preferred_element_type=F32)
    hh = jnp.dot(xb, wh_ref[...].astype(BF16), preferred_element_type=F32)
    u_ref[...] = cg * hh


def _conv_in(x, w_in, layer):
    m, d = x.shape
    tm = _pick_tile(m, 1040, 16)
    tn = 256
    nt = d // tn
    wspec = lambda part: pl.BlockSpec((None, d, tn), lambda i, j: (layer, 0, part * nt + j))
    return pl.pallas_call(
        _convin_kernel,
        grid=(m // tm, nt),
        in_specs=[pl.BlockSpec((tm, d), lambda i, j: (i, 0)), wspec(0), wspec(1), wspec(2)],
        out_specs=[pl.BlockSpec((tm, tn), lambda i, j: (i, j)),
                   pl.BlockSpec((tm, tn), lambda i, j: (i, j))],
        out_shape=[jax.ShapeDtypeStruct((m, d), F32), jax.ShapeDtypeStruct((m, d), F32)],
        scratch_shapes=[pltpu.VMEM((tm, d), BF16)],
        compiler_params=_cparams(("arbitrary", "arbitrary")),
        name="conv_in",
    )(x, w_in, w_in, w_in)


def _convgate_kernel(u_ref, bg_ref, halo_ref, meta_ref, h1_ref, h2_ref, w_ref, z_ref, *,
                     seq_blocks, n_prompt_blocks, n_batch, n_meta, dec_seq, n_sample_rows):
    i = pl.program_id(0)
    u = u_ref[...]
    rows = u.shape[0]
    r = lax.broadcasted_iota(jnp.int32, (rows, 1), 0)

    halo = halo_ref[...]
    prev2, prev1 = halo[V7X_SUBLANES - 2:V7X_SUBLANES - 1], halo[V7X_SUBLANES - 1:V7X_SUBLANES]
    meta = meta_ref[...]
    for b in range(n_batch):
        starts_batch = i == b * seq_blocks
        prev2 = jnp.where(starts_batch, meta[(b + 1) * n_meta - 2:(b + 1) * n_meta - 1], prev2)
        prev1 = jnp.where(starts_batch, meta[(b + 1) * n_meta - 1:(b + 1) * n_meta], prev1)

    p1 = jnp.where(r == 0, prev1, pltpu.roll(u, 1, axis=0))
    p2 = jnp.where(r == 0, prev2, jnp.where(r == 1, prev1, pltpu.roll(u, 2, axis=0)))

    small = i == n_prompt_blocks
    n_meta_rows = n_batch * n_meta
    in_meta = r < n_meta_rows
    in_sample = jnp.logical_and(r >= n_meta_rows, r < n_meta_rows + n_sample_rows)
    pos = jnp.where(in_meta, r % n_meta, (r - n_meta_rows) % dec_seq)
    restart = jnp.logical_or(in_meta, in_sample)
    m1 = jnp.logical_and(small, jnp.logical_and(restart, pos == 0))
    m2 = jnp.logical_and(small, jnp.logical_and(restart, pos <= 1))
    p1 = jnp.where(m1, h1_ref[...], p1)
    p2 = jnp.where(m2, h2_ref[...], p2)

    w = w_ref[...]
    conv = w[0:1] * p2 + w[1:2] * p1 + w[2:3] * u
    z_ref[...] = (bg_ref[...] * conv).astype(BF16)


def _conv_gate(u, bg, hist1, hist2, w_conv, layer, *, seq, n_batch, n_meta, dec_seq, n_sample_rows):
    m, d = u.shape
    rows = SMALL_ROWS
    n_prompt_blocks = n_batch * seq // rows
    assert m == (n_prompt_blocks + 1) * rows and seq % rows == 0
    halo_per_block = rows // V7X_SUBLANES
    kern = functools.partial(_convgate_kernel, seq_blocks=seq // rows, n_prompt_blocks=n_prompt_blocks,
                             n_batch=n_batch, n_meta=n_meta, dec_seq=dec_seq, n_sample_rows=n_sample_rows)
    return pl.pallas_call(
        kern,
        grid=(m // rows,),
        in_specs=[pl.BlockSpec((rows, d), lambda i: (i, 0)),
                  pl.BlockSpec((rows, d), lambda i: (i, 0)),
                  pl.BlockSpec((V7X_SUBLANES, d), lambda i: (jnp.maximum(i * halo_per_block - 1, 0), 0)),
                  pl.BlockSpec((rows, d), lambda i: (n_prompt_blocks, 0)),
                  pl.BlockSpec((rows, d), lambda i: (0, 0)),
                  pl.BlockSpec((rows, d), lambda i: (0, 0)),
                  pl.BlockSpec((None, 3, d), lambda i: (layer, 0, 0))],
        out_specs=pl.BlockSpec((rows, d), lambda i: (i, 0)),
        out_shape=jax.ShapeDtypeStruct((m, d), BF16),
        compiler_params=_cparams(("arbitrary",)),
        name="conv_gate",
    )(u, bg, u, u, hist1, hist2, w_conv)


def _residual_layer_norm(x_ref, o_ref, g_ref, b_ref, alpha, rows_chunk):
    g = g_ref[...]
    b = b_ref[...]

    def body(c, carry):
        rs = pl.ds(pl.multiple_of(c * rows_chunk, rows_chunk), rows_chunk)
        y = alpha * x_ref[rs, :] + o_ref[rs, :]
        mu = jnp.mean(y, axis=-1, keepdims=True)
        yc = y - mu
        var = jnp.mean(yc * yc, axis=-1, keepdims=True)
        o_ref[rs, :] = yc * lax.rsqrt(var + LN_EPS) * g + b
        return carry

    lax.fori_loop(0, x_ref.shape[0] // rows_chunk, body, 0)


def _outproj_kernel(a_ref, *rest, alpha, rows_chunk, n_main_last):
    small_ref = rest[0] if n_main_last is not None else None
    w_ref, x_ref, g_ref, b_ref, o_ref = rest[-5:]
    i = pl.program_id(0)
    k = pl.program_id(1)
    last_tile = pl.num_programs(0) - 1

    @pl.when(k == 0)
    def _():
        o_ref[...] = jnp.zeros_like(o_ref)

    w = w_ref[...].astype(BF16)
    if small_ref is None:
        o_ref[...] += jnp.dot(a_ref[...], w, preferred_element_type=F32)
    else:
        @pl.when(i < last_tile)
        def _():
            o_ref[...] += jnp.dot(a_ref[...], w, preferred_element_type=F32)

        @pl.when(i == last_tile)
        def _():
            a = jnp.concatenate([a_ref[:n_main_last], small_ref[...]], axis=0)
            o_ref[...] += jnp.dot(a, w, preferred_element_type=F32)

    @pl.when(k == pl.num_programs(1) - 1)
    def _():
        _residual_layer_norm(x_ref, o_ref, g_ref, b_ref, alpha, rows_chunk)


def _outproj_ln(a16, a16_small, w, layer, x, g, b, alpha):
    m, d = x.shape
    tm = _pick_tile(m, 832, 16)
    tk = 512
    rows_chunk = _pick_tile(tm, 64, V7X_SUBLANES)
    if a16_small is None:
        assert a16.shape[0] == m
        n_main_last, small_args, small_specs = None, [], []
    else:
        n_small = a16_small.shape[0]
        assert a16.shape[0] + n_small == m and n_small <= tm
        n_main_last = tm - n_small
        small_args = [a16_small]
        small_specs = [pl.BlockSpec((n_small, tk), lambda i, k: (0, k))]
    return pl.pallas_call(
        functools.partial(_outproj_kernel, alpha=alpha, rows_chunk=rows_chunk, n_main_last=n_main_last),
        grid=(m // tm, d // tk),
        in_specs=[pl.BlockSpec((tm, tk), lambda i, k: (i, k))] + small_specs + [
                  pl.BlockSpec((None, tk, d), lambda i, k: (layer, k, 0)),
                  pl.BlockSpec((tm, d), lambda i, k: (i, 0)),
                  pl.BlockSpec((1, d), lambda i, k: (0, 0)),
                  pl.BlockSpec((1, d), lambda i, k: (0, 0))],
        out_specs=pl.BlockSpec((tm, d), lambda i, k: (i, 0)),
        out_shape=jax.ShapeDtypeStruct((m, d), F32),
        compiler_params=_cparams(("arbitrary", "arbitrary")),
        name="outproj_ln",
    )(a16, *small_args, w, x, g, b)


def _mlp_kernel(x_ref, wu_ref, wd_ref, g_ref, b_ref, o_ref, *rest, alpha, rows_chunk, n_main_last):
    small_ref = rest[0] if n_main_last is not None else None
    xb_ref = rest[-1]
    f = pl.program_id(1)

    @pl.when(f == 0)
    def _():
        xb_ref[...] = x_ref[...].astype(BF16)
        o_ref[...] = jnp.zeros_like(o_ref)

    h = jnp.dot(xb_ref[...], wu_ref[...].astype(BF16), preferred_element_type=F32)
    h = jnp.square(jnp.maximum(h, 0.0)).astype(BF16)
    o_ref[...] += jnp.dot(h, wd_ref[...].astype(BF16), preferred_element_type=F32)

    @pl.when(f == pl.num_programs(1) - 1)
    def _():
        _residual_layer_norm(x_ref, o_ref, g_ref, b_ref, alpha, rows_chunk)
        if small_ref is not None:
            @pl.when(pl.program_id(0) == pl.num_programs(0) - 1)
            def _():
                small_ref[...] = o_ref[n_main_last:, :]


def _mlp_ln(x, w_up, w_down, layer, g, b, alpha, n_main=None):
    m, d = x.shape
    d_ff = w_up.shape[-1]
    tm = _pick_tile(m, 832, 16)
    tf = 512
    rows_chunk = _pick_tile(tm, 64, V7X_SUBLANES)
    out_specs = pl.BlockSpec((tm, d), lambda i, f: (i, 0))
    out_shape = jax.ShapeDtypeStruct((m, d), F32)
    n_main_last = None
    if n_main is not None:
        n_small = m - n_main
        assert 0 < n_small <= tm
        n_main_last = tm - n_small
        out_specs = [out_specs, pl.BlockSpec((n_small, d), lambda i, f: (0, 0))]
        out_shape = [jax.ShapeDtypeStruct((n_main, d), F32), jax.ShapeDtypeStruct((n_small, d), F32)]
    return pl.pallas_call(
        functools.partial(_mlp_kernel, alpha=alpha, rows_chunk=rows_chunk, n_main_last=n_main_last),
        grid=(m // tm, d_ff // tf),
        in_specs=[pl.BlockSpec((tm, d), lambda i, f: (i, 0)),
                  pl.BlockSpec((None, d, tf), lambda i, f: (layer, 0, f)),
                  pl.BlockSpec((None, tf, d), lambda i, f: (layer, f, 0)),
                  pl.BlockSpec((1, d), lambda i, f: (0, 0)),
                  pl.BlockSpec((1, d), lambda i, f: (0, 0))],
        out_specs=out_specs,
        out_shape=out_shape,
        scratch_shapes=[pltpu.VMEM((tm, d), BF16)],
        compiler_params=_cparams(("arbitrary", "arbitrary")),
        name="mlp_ln",
    )(x, w_up, w_down, g, b)


def _diff_lambda(lam_ref, lam_init):
    lp = lam_ref[...]
    a = jnp.sum(lp[0:1] * lp[1:2], axis=-1, keepdims=True)
    b = jnp.sum(lp[2:3] * lp[3:4], axis=-1, keepdims=True)
    return jnp.exp(a) - jnp.exp(b) + lam_init


def _sub_layer_norm(o, g_ref, lam_init):
    return o * lax.rsqrt(jnp.mean(o * o, axis=-1, keepdims=True) + SUBLN_EPS) * (g_ref[...] * (1.0 - lam_init))


def _dot_nt(a, b):
    return lax.dot_general(a, b, (((1,), (1,)), ((), ())), preferred_element_type=F32)


def _flash_kernel(q_ref, k_ref, v_ref, km_ref, vm_ref, qm_ref, bn_ref, bm_ref, bmm_ref, lam_ref, g_ref,
                  o_ref, om_ref, vt_sc, s0_sc, s1_sc, m_sc, l_sc, acc_sc, *, head_dim, lam_init):
    qi = pl.program_id(2)
    tq = q_ref.shape[0]
    n_local, _, ck, _ = bn_ref.shape
    ratio = tq // ck
    n_chunks = k_ref.shape[0] // ck
    n_meta = km_ref.shape[0]
    vd = 2 * head_dim
    heads = range(n_local)
    lam = _diff_lambda(lam_ref, lam_init)
    cols = lambda hh: slice(hh * vd, (hh + 1) * vd)

    @pl.when(qi == 0)
    def _():
        for hh in heads:
            def body(c, carry):
                rs = pl.ds(pl.multiple_of(c * ck, ck), ck)
                vt_sc[hh, c] = v_ref[rs, cols(hh)].astype(F32).T.astype(BF16)
                return carry

            lax.fori_loop(0, n_chunks, body, 0)
            vm_pad = jnp.concatenate([vm_ref[:, cols(hh)].astype(F32), jnp.zeros((ck - n_meta, vd), F32)], axis=0)
            vt_sc[hh, n_chunks] = vm_pad.T.astype(BF16)

    def scores_t(hh, keys, bias_t):
        parts = [_dot_nt(keys[:, m * head_dim:(m + 1) * head_dim],
                         q_ref[:, hh * vd + m * head_dim:hh * vd + (m + 1) * head_dim]) for m in range(2)]
        if bias_t is not None:
            parts = [p + bias_t for p in parts]
        return jnp.concatenate(parts, axis=1)

    for hh in heads:
        s = scores_t(hh, km_ref[:, cols(hh)], jnp.where(qi == 0, bm_ref[hh], 0.0))
        m0 = jnp.max(s, axis=0, keepdims=True)
        p = jnp.exp2(s - m0)
        m_sc[hh] = m0
        l_sc[hh] = jnp.sum(p, axis=0, keepdims=True)
        p_pad = jnp.concatenate([p, jnp.zeros((V7X_LANES - n_meta, 2 * tq), F32)], axis=0).astype(BF16)
        acc_sc[hh] = jnp.dot(vt_sc[hh, n_chunks][:, :V7X_LANES], p_pad, preferred_element_type=F32)

    assert ratio == 1
    s_bufs = (s0_sc, s1_sc)

    def scores_into(c, tile, s_buf):
        for hh in heads:
            kb = k_ref[pl.ds(pl.multiple_of(c * ck, ck), ck), cols(hh)]
            s_buf[hh] = scores_t(hh, kb, None if tile is None else bn_ref[hh, tile])

    def update_from(c, s_buf):
        for hh in heads:
            s = s_buf[hh]
            m_old = m_sc[hh]
            m_new = jnp.maximum(m_old, jnp.max(s, axis=0, keepdims=True))
            alpha = jnp.exp2(m_old - m_new)
            p = jnp.exp2(s - m_new)
            l_sc[hh] = alpha * l_sc[hh] + jnp.sum(p, axis=0, keepdims=True)
            acc_sc[hh] = alpha * acc_sc[hh] + jnp.dot(vt_sc[hh, c], p.astype(BF16), preferred_element_type=F32)
            m_sc[hh] = m_new

    def by_parity(c, fn):
        for par in range(2):
            @pl.when(jnp.bitwise_and(c, 1) == par)
            def _():
                fn(s_bufs[par], s_bufs[1 - par])

    def advance(c, next_tile):
        def fn(cur, nxt):
            scores_into(c + 1, next_tile, nxt)
            update_from(c, cur)
        by_parity(c, fn)

    @pl.when(qi == 0)
    def _():
        scores_into(0, 1, s_bufs[0])

    @pl.when(qi == 1)
    def _():
        scores_into(0, 0, s_bufs[0])

    @pl.when(qi >= 2)
    def _():
        scores_into(0, None, s_bufs[0])

        def far_body(c, carry):
            advance(c, None)
            return carry

        lax.fori_loop(0, qi - 2, far_body, 0)
        advance(qi - 2, 0)

    @pl.when(qi >= 1)
    def _():
        advance(qi - 1, 1)

    by_parity(qi, lambda cur, nxt: update_from(qi, cur))

    for hh in heads:
        on = acc_sc[hh] * (1.0 / l_sc[hh])
        o_t = on[:, :tq] - lam * on[:, tq:]
        o_t = o_t * lax.rsqrt(jnp.mean(o_t * o_t, axis=0, keepdims=True) + SUBLN_EPS)
        o_ref[:, cols(hh)] = (o_t.T * (g_ref[...] * (1.0 - lam_init))).astype(BF16)

    @pl.when(qi == 0)
    def _():
        for hh in heads:
            qm = qm_ref[:, cols(hh)]
            km = km_ref[:, cols(hh)]
            sm = jnp.concatenate([_dot_nt(qm[:, m * head_dim:(m + 1) * head_dim],
                                          km[:, m * head_dim:(m + 1) * head_dim]) + bmm_ref[hh]
                                  for m in range(2)], axis=0)
            pm = jnp.exp2(sm - jnp.max(sm, axis=-1, keepdims=True))
            onm = (jnp.dot(pm.astype(BF16), vm_ref[:, cols(hh)], preferred_element_type=F32)
                   * (1.0 / jnp.sum(pm, axis=-1, keepdims=True)))
            om_ref[:, cols(hh)] = _sub_layer_norm(onm[:n_meta] - lam * onm[n_meta:], g_ref, lam_init)


def _flash_prompt(qkv16, bias_near, bias_meta, bias_mm, lam_params, subln, layer, lam_init, *,
                  n_batch, seq, n_heads, head_dim, n_meta):
    m, d3 = qkv16.shape
    d = d3 // 3
    vd = 2 * head_dim
    tq = FLASH_TQ
    hp = FLASH_HEADS
    assert n_heads % hp == 0
    ng = n_heads // hp
    nq = seq // tq
    mp = n_batch * seq
    meta_blk = mp // n_meta
    kern = functools.partial(_flash_kernel, head_dim=head_dim, lam_init=lam_init)
    return pl.pallas_call(
        kern,
        grid=(n_batch, ng, nq),
        in_specs=[pl.BlockSpec((tq, hp * vd), lambda b, h, qi: (b * nq + qi, h)),
                  pl.BlockSpec((seq, hp * vd), lambda b, h, qi: (b, ng + h)),
                  pl.BlockSpec((seq, hp * vd), lambda b, h, qi: (b, 2 * ng + h)),
                  pl.BlockSpec((n_meta, hp * vd), lambda b, h, qi: (meta_blk + b, ng + h)),
                  pl.BlockSpec((n_meta, hp * vd), lambda b, h, qi: (meta_blk + b, 2 * ng + h)),
                  pl.BlockSpec((n_meta, hp * vd), lambda b, h, qi: (meta_blk + b, h)),
                  pl.BlockSpec((hp,) + bias_near.shape[1:], lambda b, h, qi: (h, 0, 0, 0)),
                  pl.BlockSpec((hp,) + bias_meta.shape[1:], lambda b, h, qi: (h, 0, 0)),
                  pl.BlockSpec((hp,) + bias_mm.shape[1:], lambda b, h, qi: (h, 0, 0)),
                  pl.BlockSpec((None, 4, head_dim), lambda b, h, qi: (layer, 0, 0)),
                  pl.BlockSpec((None, 1, vd), lambda b, h, qi: (layer, 0, 0))],
        out_specs=[pl.BlockSpec((tq, hp * vd), lambda b, h, qi: (b * nq + qi, h)),
                   pl.BlockSpec((n_meta, hp * vd), lambda b, h, qi: (b, h))],
        out_shape=[jax.ShapeDtypeStruct((mp, d), BF16),
                   jax.ShapeDtypeStruct((n_batch * n_meta, d), F32)],
        scratch_shapes=[pltpu.VMEM((hp, seq // FLASH_CK + 1, vd, FLASH_CK), BF16),
                        pltpu.VMEM((hp, FLASH_CK, 2 * tq), F32), pltpu.VMEM((hp, FLASH_CK, 2 * tq), F32),
                        pltpu.VMEM((hp, 1, 2 * tq), F32), pltpu.VMEM((hp, 1, 2 * tq), F32),
                        pltpu.VMEM((hp, vd, 2 * tq), F32)],
        compiler_params=_cparams(("arbitrary", "arbitrary", "arbitrary")),
        name="flash_prompt",
    )(qkv16, qkv16, qkv16, qkv16, qkv16, qkv16, bias_near, bias_meta, bias_mm, lam_params, subln)


def _decode_kernel(pt_ref, q_ref, kn_ref, vn_ref, *rest, n_heads, head_dim, n_pages_step, lam_init):
    kp_refs = rest[:n_pages_step]
    vp_refs = rest[n_pages_step:2 * n_pages_step]
    bd_ref, bnw_ref, lam_ref, g_ref, o_ref, qc_sc, hm_sc, m_sc, l_sc, acc_sc = rest[2 * n_pages_step:]
    st = pl.program_id(1)
    last = pl.num_programs(1) - 1
    dec_seq, d = q_ref.shape
    vd = 2 * head_dim
    map_rows = n_heads * dec_seq
    n_groups = 2 * n_heads

    @pl.when(st == 0)
    def _():
        q = q_ref[...]
        for mp in range(2):
            qc_sc[mp] = jnp.concatenate(
                [q[:, (2 * h + mp) * head_dim:(2 * h + mp + 1) * head_dim] for h in range(n_heads)],
                axis=0).astype(BF16)
        r = lax.broadcasted_iota(jnp.int32, hm_sc.shape, 0)
        c = lax.broadcasted_iota(jnp.int32, hm_sc.shape, 1)
        hm_sc[...] = jnp.where((r // dec_seq) % n_heads == c % n_heads, 0.0, MASK_VALUE)

        qt = jnp.concatenate([q] * n_groups, axis=0)
        grp = lax.broadcasted_iota(jnp.int32, qt.shape, 0) // dec_seq
        col = lax.broadcasted_iota(jnp.int32, qt.shape, 1) // head_dim
        qb = jnp.where(col == (grp % n_heads) * 2 + grp // n_heads, qt, 0.0).astype(BF16)
        pad = jnp.zeros((dec_seq, d), F32)
        kn = jnp.concatenate([kn_ref[...], pad], axis=0).astype(BF16)
        vn = jnp.concatenate([vn_ref[...], pad], axis=0).astype(BF16)
        s = _dot_nt(qb, kn) + bnw_ref[...]
        m0 = jnp.max(s, axis=-1, keepdims=True)
        p = jnp.exp2(s - m0)
        m_sc[...] = m0
        l_sc[...] = jnp.sum(p, axis=-1, keepdims=True)
        full = jnp.dot(p.astype(BF16), vn, preferred_element_type=F32)
        acc_sc[...] = jnp.concatenate(
            [full[g * dec_seq:(g + 1) * dec_seq, (g % n_heads) * vd:(g % n_heads + 1) * vd]
             for g in range(n_groups)], axis=0)

    th_rows = vp_refs[0].shape[0]
    s = jnp.concatenate(
        [_dot_nt(qc_sc[mp], jnp.concatenate([kp[pl.ds(mp, th_rows, stride=2), :] for kp in kp_refs],
                                            axis=0).astype(BF16)) for mp in range(2)], axis=0)
    s = s + jnp.tile(hm_sc[...], (1, s.shape[1] // hm_sc.shape[1])) + jnp.where(st == last, bd_ref[...], 0.0)
    v16 = jnp.concatenate([vp[...].astype(BF16) for vp in vp_refs], axis=0)
    m_old = m_sc[...]
    m_new = jnp.maximum(m_old, jnp.max(s, axis=-1, keepdims=True))
    alpha = jnp.exp2(m_old - m_new)
    p = jnp.exp2(s - m_new)
    l_sc[...] = alpha * l_sc[...] + jnp.sum(p, axis=-1, keepdims=True)
    acc_sc[...] = alpha * acc_sc[...] + jnp.dot(p.astype(BF16), v16, preferred_element_type=F32)
    m_sc[...] = m_new

    @pl.when(st == last)
    def _():
        lam = _diff_lambda(lam_ref, lam_init)
        on = acc_sc[...] * (1.0 / l_sc[...])
        for h in range(n_heads):
            oh = (on[h * dec_seq:(h + 1) * dec_seq]
                  - lam * on[map_rows + h * dec_seq:map_rows + (h + 1) * dec_seq])
            o_ref[:, h * vd:(h + 1) * vd] = _sub_layer_norm(oh, g_ref, lam_init)


def _decode_attn(q32, kv32, cache_k, cache_v, page_table, bias_past, bias_new, lam_params, subln, layer,
                 lam_init, *, new_row0, n_heads, head_dim):
    n_rows, d = q32.shape
    dec_batch, n_pages = page_table.shape
    dec_seq = n_rows // dec_batch
    n_attn, n_pool, page = cache_k.shape[:3]
    ck = cache_k.reshape(n_attn, n_pool, page * 2 * n_heads, head_dim)
    cv = cache_v.reshape(n_attn, n_pool, page * n_heads, 2 * head_dim)
    nps = DECODE_PAGES
    assert n_pages % nps == 0 and new_row0 % dec_seq == 0
    n_steps = n_pages // nps
    vd = 2 * head_dim
    rows = 2 * n_heads * dec_seq
    new_blk = new_row0 // dec_seq

    def page_spec(t, arr):
        return pl.BlockSpec((None, None) + arr.shape[2:],
                            lambda bd, st, pt: (layer, pt[bd * n_pages + st * nps + t], 0, 0))

    grid_spec = pltpu.PrefetchScalarGridSpec(
        num_scalar_prefetch=1,
        grid=(dec_batch, n_steps),
        in_specs=[pl.BlockSpec((dec_seq, d), lambda bd, st, pt: (bd, 0)),
                  pl.BlockSpec((dec_seq, d), lambda bd, st, pt: (new_blk + bd, 0)),
                  pl.BlockSpec((dec_seq, d), lambda bd, st, pt: (new_blk + bd, 1))]
                 + [page_spec(t, ck) for t in range(nps)] + [page_spec(t, cv) for t in range(nps)]
                 + [pl.BlockSpec(bias_past.shape, lambda bd, st, pt: (0, 0)),
                    pl.BlockSpec(bias_new.shape, lambda bd, st, pt: (0, 0)),
                    pl.BlockSpec((None, 4, head_dim), lambda bd, st, pt: (layer, 0, 0)),
                    pl.BlockSpec((None, 1, vd), lambda bd, st, pt: (layer, 0, 0))],
        out_specs=pl.BlockSpec((dec_seq, d), lambda bd, st, pt: (bd, 0)),
        scratch_shapes=[pltpu.VMEM((2, rows // 2, head_dim), BF16), pltpu.VMEM((rows, V7X_LANES), F32),
                        pltpu.VMEM((rows, 1), F32), pltpu.VMEM((rows, 1), F32),
                        pltpu.VMEM((rows, vd), F32)],
    )
    kern = functools.partial(_decode_kernel, n_heads=n_heads, head_dim=head_dim, n_pages_step=nps,
                             lam_init=lam_init)
    return pl.pallas_call(
        kern,
        grid_spec=grid_spec,
        out_shape=jax.ShapeDtypeStruct((n_rows, d), F32),
        compiler_params=_cparams(("arbitrary", "arbitrary")),
        name="decode_attn",
    )(page_table.reshape(-1), q32, kv32, kv32, *([ck] * nps), *([cv] * nps),
      bias_past, bias_new, lam_params, subln)


def kernel(x_prompt, x_sample, cache_k, cache_v, state_conv, page_table, meta_tokens, rel_bias,
           attn_w_qkv, attn_lambda, attn_subln, attn_w_o, conv_w_in, conv_w, conv_w_out,
           mlp_w_up, mlp_w_down, ln_mix_g, ln_mix_b, ln_mlp_g, ln_mlp_b):
    n_batch, seq, d = x_prompt.shape
    dec_batch, dec_seq, _ = x_sample.shape
    n_meta = meta_tokens.shape[0]
    n_buckets, n_heads = rel_bias.shape
    head_dim = cache_k.shape[-1]
    vd = cache_v.shape[-1]
    page = cache_k.shape[2]
    depth = mlp_w_up.shape[0]
    assert d == n_heads * vd and vd == 2 * head_dim
    alpha = (2 * depth) ** 0.25

    mp = n_batch * seq
    n_meta_rows = n_batch * n_meta
    n_sample_rows = dec_batch * dec_seq
    assert n_meta_rows + n_sample_rows <= SMALL_ROWS and seq % FLASH_TQ == 0
    assert n_meta % V7X_SUBLANES == 0 and dec_seq % V7X_SUBLANES == 0
    meta0, samp0 = mp, mp + n_meta_rows
    mt = mp + SMALL_ROWS

    x = jnp.concatenate(
        [x_prompt.reshape(mp, d)] + [meta_tokens.astype(F32)] * n_batch
        + [x_sample.reshape(n_sample_rows, d), jnp.zeros((mt - samp0 - n_sample_rows, d), F32)], axis=0)

    far = _far_distance(n_buckets)
    tq, ck, nps = FLASH_TQ, FLASH_CK, DECODE_PAGES
    ratio = tq // ck
    assert ck + 1 >= far and n_meta + tq - (n_meta - 1) >= far and (nps - 1) * page + 1 >= far
    qq = np.arange(tq)[None, :]
    near_idx = np.stack([_bucket_tile(qq - (t - 1) * ck - np.arange(ck)[:, None], n_buckets)
                         for t in range(ratio + 1)])
    meta_idx = _bucket_tile(n_meta + qq - np.arange(n_meta)[:, None], n_buckets)
    mm_idx = _bucket_tile(np.arange(n_meta)[:, None] - np.arange(n_meta)[None, :], n_buckets)
    step_of_row = (np.arange(2 * dec_seq) % dec_seq)[:, None]
    past_token = (np.arange(nps * page * n_heads) // n_heads)[None, :]
    past_idx = _bucket_tile(nps * page + step_of_row - past_token, n_buckets)
    new_cols = np.arange(2 * dec_seq)[None, :]
    new_idx = np.where(new_cols < dec_seq, _bucket_tile(step_of_row - new_cols, n_buckets), -1).astype(np.int32)

    bias_near = _bias_expand(near_idx.reshape((ratio + 1) * ck, tq), rel_bias).reshape(
        n_heads, ratio + 1, ck, tq)
    bias_meta = _bias_expand(meta_idx, rel_bias)
    bias_mm = _bias_expand(mm_idx, rel_bias)

    def decode_rows(tile):
        lanes = tile.shape[-1]
        return tile.reshape(n_heads, 2, dec_seq, lanes).transpose(1, 0, 2, 3).reshape(2 * n_heads * dec_seq, lanes)

    bias_past = decode_rows(_bias_expand(past_idx, rel_bias))
    bias_new = decode_rows(_bias_expand(new_idx, rel_bias))

    def conv_history(state):
        h1 = jnp.zeros((dec_batch, dec_seq, d), F32).at[:, 0].set(state[:, 1])
        h2 = jnp.zeros((dec_batch, dec_seq, d), F32).at[:, 0].set(state[:, 0]).at[:, 1].set(state[:, 1])
        pad_lo = jnp.zeros((n_meta_rows, d), F32)
        pad_hi = jnp.zeros((SMALL_ROWS - n_meta_rows - n_sample_rows, d), F32)
        cat = lambda h: jnp.concatenate([pad_lo, h.reshape(n_sample_rows, d), pad_hi], axis=0)
        return cat(h1), cat(h2)

    subln3 = attn_subln.reshape(attn_subln.shape[0], 1, vd)
    kv_layers, c_p, c_s = [], [], []
    for i in range(depth):
        j = i // N_MIXERS
        g_mix, b_mix = ln_mix_g[i][None], ln_mix_b[i][None]
        if i % N_MIXERS == 0:
            lam0 = _lambda_init(i)
            qkv16, kv32 = _qkv_proj(x, attn_w_qkv, j, head_dim ** -0.5 * LOG2_E)
            o16, o_meta = _flash_prompt(qkv16, bias_near, bias_meta, bias_mm, attn_lambda, subln3, j, lam0,
                                        n_batch=n_batch, seq=seq, n_heads=n_heads, head_dim=head_dim,
                                        n_meta=n_meta)
            q32 = qkv16[samp0:samp0 + n_sample_rows, :d].astype(F32)
            o_samp = _decode_attn(q32, kv32, cache_k, cache_v, page_table, bias_past, bias_new, attn_lambda,
                                  subln3, j, lam0, new_row0=samp0, n_heads=n_heads, head_dim=head_dim)
            o_small = jnp.concatenate(
                [o_meta, o_samp, jnp.zeros((mt - samp0 - n_sample_rows, d), F32)], axis=0).astype(BF16)
            x = _outproj_ln(o16, o_small, attn_w_o, j, x, g_mix, b_mix, alpha)
            kv_layers.append(kv32)
        else:
            bg, u = _conv_in(x, conv_w_in, j)
            hist1, hist2 = conv_history(state_conv[j])
            z16 = _conv_gate(u, bg, hist1, hist2, conv_w, j, seq=seq, n_batch=n_batch, n_meta=n_meta,
                             dec_seq=dec_seq, n_sample_rows=n_sample_rows)
            x = _outproj_ln(z16, None, conv_w_out, j, x, g_mix, b_mix, alpha)
            c_p.append(jnp.stack([u[(b + 1) * seq - 2:(b + 1) * seq] for b in range(n_batch)]))
            c_s.append(u[samp0:samp0 + n_sample_rows].reshape(dec_batch, dec_seq, d)[:, dec_seq - 2:])
        last = i == depth - 1
        x = _mlp_ln(x, mlp_w_up, mlp_w_down, i, ln_mlp_g[i][None], ln_mlp_b[i][None], alpha,
                    n_main=mp if last else None)

    y_main, y_small = x
    y_prompt = y_main.reshape(n_batch, seq, d)
    y_sample = y_small[n_meta_rows:n_meta_rows + n_sample_rows].reshape(dec_batch, dec_seq, d)

    def new_cache(col0, tail):
        cols = slice(col0, col0 + d)
        prompt = jnp.concatenate(
            [jnp.concatenate(
                [jnp.concatenate([kv[meta0 + b * n_meta:meta0 + (b + 1) * n_meta, cols].reshape((1, 1, n_meta) + tail),
                                  kv[b * seq:(b + 1) * seq, cols].reshape((1, 1, seq) + tail)], axis=2)
                 for b in range(n_batch)], axis=1)
             for kv in kv_layers], axis=0)
        sample = jnp.stack([kv[samp0:samp0 + n_sample_rows, cols].reshape((dec_batch, dec_seq) + tail)
                            for kv in kv_layers])
        return prompt, sample

    k_p, k_s = new_cache(0, (n_heads, 2, head_dim))
    v_p, v_s = new_cache(d, (n_heads, vd))
    return (y_prompt, y_sample, k_p, v_p, jnp.stack(c_p), k_s, v_s, jnp.stack(c_s))
```

```python
import functools
import math

import numpy as np
import jax
import jax.numpy as jnp
from jax import lax
from jax.experimental import pallas as pl
from jax.experimental.pallas import tpu as pltpu

F32 = jnp.float32
BF16 = jnp.bfloat16

MAX_DISTANCE = 128
LN_EPS = 1e-5
SUBLN_EPS = 1e-5
N_MIXERS = 2
MASK_VALUE = -1e30
LOG2_E = math.log2(math.e)

V7X_LANES = 128
V7X_SUBLANES = 8
V7X_VMEM_LIMIT_BYTES = 60 * 1024 * 1024

SMALL_ROWS = 128
FLASH_TQ = 512
FLASH_CK = 512
FLASH_HEADS = 2
DECODE_PAGES = 8
DECODE_GROUP = 2


def _lambda_init(layer):
    return 0.8 - 0.6 * math.exp(-0.3 * layer)


def _pick_tile(total, target, multiple):
    best = None
    for t in range(multiple, min(total, target) + 1, multiple):
        if total % t == 0:
            best = t
    assert best is not None, (total, target, multiple)
    return best


def _cparams(sem):
    return pltpu.CompilerParams(dimension_semantics=sem, vmem_limit_bytes=V7X_VMEM_LIMIT_BYTES)


def _bucket_of_distance(n, n_buckets):
    n = np.asarray(n, np.int64)
    max_exact = n_buckets // 2
    nf = np.maximum(n, 1).astype(np.float64)
    large = max_exact + (np.log(nf / max_exact) / math.log(MAX_DISTANCE / max_exact)
                         * (n_buckets - max_exact)).astype(np.int64)
    large = np.minimum(large, n_buckets - 1)
    return np.where(n < max_exact, n, large).astype(np.int32)


def _far_distance(n_buckets):
    tab = _bucket_of_distance(np.arange(4 * MAX_DISTANCE), n_buckets)
    not_last = np.nonzero(tab != n_buckets - 1)[0]
    return int(not_last[-1]) + 1


def _bucket_tile(dist, n_buckets):
    dist = np.asarray(dist, np.int64)
    return np.where(dist < 0, -1, _bucket_of_distance(np.maximum(dist, 0), n_buckets)).astype(np.int32)


def _bias_expand_kernel(rb_ref, idx_ref, o_ref, *, n_buckets):
    h = pl.program_id(0)
    idx = idx_ref[...]
    far = rb_ref[n_buckets - 1, h]
    out = jnp.zeros(idx.shape, F32)
    for b in range(n_buckets - 1):
        out = jnp.where(idx == b, (rb_ref[b, h] - far) * LOG2_E, out)
    o_ref[...] = jnp.where(idx < 0, MASK_VALUE, out)


def _bias_expand(idx_np, rel_bias):
    rows, cols = idx_np.shape
    n_buckets, n_heads = rel_bias.shape
    return pl.pallas_call(
        functools.partial(_bias_expand_kernel, n_buckets=n_buckets),
        grid=(n_heads,),
        in_specs=[pl.BlockSpec(memory_space=pltpu.SMEM),
                  pl.BlockSpec((rows, cols), lambda h: (0, 0))],
        out_specs=pl.BlockSpec((None, rows, cols), lambda h: (h, 0, 0)),
        out_shape=jax.ShapeDtypeStruct((n_heads, rows, cols), F32),
        compiler_params=_cparams(("arbitrary",)),
        name="bias_expand",
    )(rel_bias, jnp.asarray(idx_np))


def _qkv_kernel(x_ref, w_ref, o16_ref, kv32_ref, xb_ref, *, n_q_tiles, q_scale):
    j = pl.program_id(1)

    @pl.when(j == 0)
    def _():
        xb_ref[...] = x_ref[...].astype(BF16)

    acc = jnp.dot(xb_ref[...], w_ref[...].astype(BF16), preferred_element_type=F32)
    scale = jnp.where(j < n_q_tiles, q_scale, 1.0).astype(F32)
    o16_ref[...] = (acc * scale).astype(BF16)

    @pl.when(j >= n_q_tiles)
    def _():
        kv32_ref[...] = acc


def _qkv_proj(x, w_qkv, layer, q_scale):
    m, d = x.shape
    tm = _pick_tile(m, 1040, 16)
    tn = 512
    n_q_tiles = d // tn
    return pl.pallas_call(
        functools.partial(_qkv_kernel, n_q_tiles=n_q_tiles, q_scale=q_scale),
        grid=(m // tm, 3 * d // tn),
        in_specs=[pl.BlockSpec((tm, d), lambda i, j: (i, 0)),
                  pl.BlockSpec((None, d, tn), lambda i, j: (layer, 0, j))],
        out_specs=[pl.BlockSpec((tm, tn), lambda i, j: (i, j)),
                   pl.BlockSpec((tm, tn), lambda i, j: (i, jnp.maximum(j - n_q_tiles, 0)))],
        out_shape=[jax.ShapeDtypeStruct((m, 3 * d), BF16),
                   jax.ShapeDtypeStruct((m, 2 * d), F32)],
        scratch_shapes=[pltpu.VMEM((tm, d), BF16)],
        compiler_params=_cparams(("arbitrary", "arbitrary")),
        name="qkv_proj",
    )(x, w_qkv)


def _convin_kernel(x_ref, wb_ref, wc_ref, wh_ref, bg_ref, u_ref, xb_ref):
    @pl.when(pl.program_id(1) == 0)
    def _():
        xb_ref[...] = x_ref[...].astype(BF16)

    xb = xb_ref[...]
    bg_ref[...] = jnp.dot(xb, wb_ref[...].astype(BF16), preferred_element_type=F32)
    cg = jnp.dot(xb, wc_ref[...].astype(BF16), preferred_element_type=F32)
    hh = jnp.dot(xb, wh_ref[...].astype(BF16), preferred_element_type=F32)
    u_ref[...] = cg * hh


def _conv_in(x, w_in, layer):
    m, d = x.shape
    tm = _pick_tile(m, 1040, 16)
    tn = 256
    nt = d // tn
    wspec = lambda part: pl.BlockSpec((None, d, tn), lambda i, j: (layer, 0, part * nt + j))
    return pl.pallas_call(
        _convin_kernel,
        grid=(m // tm, nt),
        in_specs=[pl.BlockSpec((tm, d), lambda i, j: (i, 0)), wspec(0), wspec(1), wspec(2)],
        out_specs=[pl.BlockSpec((tm, tn), lambda i, j: (i, j)),
                   pl.BlockSpec((tm, tn), lambda i, j: (i, j))],
        out_shape=[jax.ShapeDtypeStruct((m, d), F32), jax.ShapeDtypeStruct((m, d), F32)],
        scratch_shapes=[pltpu.VMEM((tm, d), BF16)],
        compiler_params=_cparams(("arbitrary", "arbitrary")),
        name="conv_in",
    )(x, w_in, w_in, w_in)


def _convgate_kernel(u_ref, bg_ref, halo_ref, meta_ref, h1_ref, h2_ref, w_ref, z_ref, *,
                     seq_blocks, n_prompt_blocks, n_batch, n_meta, dec_seq, n_sample_rows):
    i = pl.program_id(0)
    u = u_ref[...]
    rows = u.shape[0]
    r = lax.broadcasted_iota(jnp.int32, (rows, 1), 0)

    halo = halo_ref[...]
    prev2, prev1 = halo[V7X_SUBLANES - 2:V7X_SUBLANES - 1], halo[V7X_SUBLANES - 1:V7X_SUBLANES]
    meta = meta_ref[...]
    for b in range(n_batch):
        starts_batch = i == b * seq_blocks
        prev2 = jnp.where(starts_batch, meta[(b + 1) * n_meta - 2:(b + 1) * n_meta - 1], prev2)
        prev1 = jnp.where(starts_batch, meta[(b + 1) * n_meta - 1:(b + 1) * n_meta], prev1)

    p1 = jnp.where(r == 0, prev1, pltpu.roll(u, 1, axis=0))
    p2 = jnp.where(r == 0, prev2, jnp.where(r == 1, prev1, pltpu.roll(u, 2, axis=0)))

    small = i == n_prompt_blocks
    n_meta_rows = n_batch * n_meta
    in_meta = r < n_meta_rows
    in_sample = jnp.logical_and(r >= n_meta_rows, r < n_meta_rows + n_sample_rows)
    pos = jnp.where(in_meta, r % n_meta, (r - n_meta_rows) % dec_seq)
    restart = jnp.logical_or(in_meta, in_sample)
    m1 = jnp.logical_and(small, jnp.logical_and(restart, pos == 0))
    m2 = jnp.logical_and(small, jnp.logical_and(restart, pos <= 1))
    p1 = jnp.where(m1, h1_ref[...], p1)
    p2 = jnp.where(m2, h2_ref[...], p2)

    w = w_ref[...]
    conv = w[0:1] * p2 + w[1:2] * p1 + w[2:3] * u
    z_ref[...] = (bg_ref[...] * conv).astype(BF16)


def _conv_gate(u, bg, hist1, hist2, w_conv, layer, *, seq, n_batch, n_meta, dec_seq, n_sample_rows):
    m, d = u.shape
    rows = SMALL_ROWS
    n_prompt_blocks = n_batch * seq // rows
    assert m == (n_prompt_blocks + 1) * rows and seq % rows == 0
    halo_per_block = rows // V7X_SUBLANES
    kern = functools.partial(_convgate_kernel, seq_blocks=seq // rows, n_prompt_blocks=n_prompt_blocks,
                             n_batch=n_batch, n_meta=n_meta, dec_seq=dec_seq, n_sample_rows=n_sample_rows)
    return pl.pallas_call(
        kern,
        grid=(m // rows,),
        in_specs=[pl.BlockSpec((rows, d), lambda i: (i, 0)),
                  pl.BlockSpec((rows, d), lambda i: (i, 0)),
                  pl.BlockSpec((V7X_SUBLANES, d), lambda i: (jnp.maximum(i * halo_per_block - 1, 0), 0)),
                  pl.BlockSpec((rows, d), lambda i: (n_prompt_blocks, 0)),
                  pl.BlockSpec((rows, d), lambda i: (0, 0)),
                  pl.BlockSpec((rows, d), lambda i: (0, 0)),
                  pl.BlockSpec((None, 3, d), lambda i: (layer, 0, 0))],
        out_specs=pl.BlockSpec((rows, d), lambda i: (i, 0)),
        out_shape=jax.ShapeDtypeStruct((m, d), BF16),
        compiler_params=_cparams(("arbitrary",)),
        name="conv_gate",
    )(u, bg, u, u, hist1, hist2, w_conv)


def _residual_layer_norm(x_ref, o_ref, g_ref, b_ref, alpha, rows_chunk):
    g = g_ref[...]
    b = b_ref[...]

    def body(c, carry):
        rs = pl.ds(pl.multiple_of(c * rows_chunk, rows_chunk), rows_chunk)
        y = alpha * x_ref[rs, :] + o_ref[rs, :]
        mu = jnp.mean(y, axis=-1, keepdims=True)
        yc = y - mu
        var = jnp.mean(yc * yc, axis=-1, keepdims=True)
        o_ref[rs, :] = yc * lax.rsqrt(var + LN_EPS) * g + b
        return carry

    lax.fori_loop(0, x_ref.shape[0] // rows_chunk, body, 0)


def _outproj_kernel(a_ref, *rest, alpha, rows_chunk, n_main_last):
    small_ref = rest[0] if n_main_last is not None else None
    w_ref, x_ref, g_ref, b_ref, o_ref = rest[-5:]
    i = pl.program_id(0)
    k = pl.program_id(1)
    last_tile = pl.num_programs(0) - 1

    @pl.when(k == 0)
    def _():
        o_ref[...] = jnp.zeros_like(o_ref)

    w = w_ref[...].astype(BF16)
    if small_ref is None:
        o_ref[...] += jnp.dot(a_ref[...], w, preferred_element_type=F32)
    else:
        @pl.when(i < last_tile)
        def _():
            o_ref[...] += jnp.dot(a_ref[...], w, preferred_element_type=F32)

        @pl.when(i == last_tile)
        def _():
            a = jnp.concatenate([a_ref[:n_main_last], small_ref[...]], axis=0)
            o_ref[...] += jnp.dot(a, w, preferred_element_type=F32)

    @pl.when(k == pl.num_programs(1) - 1)
    def _():
        _residual_layer_norm(x_ref, o_ref, g_ref, b_ref, alpha, rows_chunk)


def _outproj_ln(a16, a16_small, w, layer, x, g, b, alpha):
    m, d = x.shape
    tm = _pick_tile(m, 832, 16)
    tk = 512
    rows_chunk = _pick_tile(tm, 64, V7X_SUBLANES)
    if a16_small is None:
        assert a16.shape[0] == m
        n_main_last, small_args, small_specs = None, [], []
    else:
        n_small = a16_small.shape[0]
        assert a16.shape[0] + n_small == m and n_small <= tm
        n_main_last = tm - n_small
        small_args = [a16_small]
        small_specs = [pl.BlockSpec((n_small, tk), lambda i, k: (0, k))]
    return pl.pallas_call(
        functools.partial(_outproj_kernel, alpha=alpha, rows_chunk=rows_chunk, n_main_last=n_main_last),
        grid=(m // tm, d // tk),
        in_specs=[pl.BlockSpec((tm, tk), lambda i, k: (i, k))] + small_specs + [
                  pl.BlockSpec((None, tk, d), lambda i, k: (layer, k, 0)),
                  pl.BlockSpec((tm, d), lambda i, k: (i, 0)),
                  pl.BlockSpec((1, d), lambda i, k: (0, 0)),
                  pl.BlockSpec((1, d), lambda i, k: (0, 0))],
        out_specs=pl.BlockSpec((tm, d), lambda i, k: (i, 0)),
        out_shape=jax.ShapeDtypeStruct((m, d), F32),
        compiler_params=_cparams(("arbitrary", "arbitrary")),
        name="outproj_ln",
    )(a16, *small_args, w, x, g, b)


def _mlp_kernel(x_ref, wu_ref, wd_ref, g_ref, b_ref, o_ref, *rest, alpha, rows_chunk, n_main_last):
    small_ref = rest[0] if n_main_last is not None else None
    xb_ref = rest[-1]
    f = pl.program_id(1)

    @pl.when(f == 0)
    def _():
        xb_ref[...] = x_ref[...].astype(BF16)
        o_ref[...] = jnp.zeros_like(o_ref)

    h = jnp.dot(xb_ref[...], wu_ref[...].astype(BF16), preferred_element_type=F32)
    h = jnp.square(jnp.maximum(h, 0.0)).astype(BF16)
    o_ref[...] += jnp.dot(h, wd_ref[...].astype(BF16), preferred_element_type=F32)

    @pl.when(f == pl.num_programs(1) - 1)
    def _():
        _residual_layer_norm(x_ref, o_ref, g_ref, b_ref, alpha, rows_chunk)
        if small_ref is not None:
            @pl.when(pl.program_id(0) == pl.num_programs(0) - 1)
            def _():
                small_ref[...] = o_ref[n_main_last:, :]


def _mlp_ln(x, w_up, w_down, layer, g, b, alpha, n_main=None):
    m, d = x.shape
    d_ff = w_up.shape[-1]
    tm = _pick_tile(m, 832, 16)
    tf = 512
    rows_chunk = _pick_tile(tm, 64, V7X_SUBLANES)
    out_specs = pl.BlockSpec((tm, d), lambda i, f: (i, 0))
    out_shape = jax.ShapeDtypeStruct((m, d), F32)
    n_main_last = None
    if n_main is not None:
        n_small = m - n_main
        assert 0 < n_small <= tm
        n_main_last = tm - n_small
        out_specs = [out_specs, pl.BlockSpec((n_small, d), lambda i, f: (0, 0))]
        out_shape = [jax.ShapeDtypeStruct((n_main, d), F32), jax.ShapeDtypeStruct((n_small, d), F32)]
    return pl.pallas_call(
        functools.partial(_mlp_kernel, alpha=alpha, rows_chunk=rows_chunk, n_main_last=n_main_last),
        grid=(m // tm, d_ff // tf),
        in_specs=[pl.BlockSpec((tm, d), lambda i, f: (i, 0)),
                  pl.BlockSpec((None, d, tf), lambda i, f: (layer, 0, f)),
                  pl.BlockSpec((None, tf, d), lambda i, f: (layer, f, 0)),
                  pl.BlockSpec((1, d), lambda i, f: (0, 0)),
                  pl.BlockSpec((1, d), lambda i, f: (0, 0))],
        out_specs=out_specs,
        out_shape=out_shape,
        scratch_shapes=[pltpu.VMEM((tm, d), BF16)],
        compiler_params=_cparams(("arbitrary", "arbitrary")),
        name="mlp_ln",
    )(x, w_up, w_down, g, b)


def _diff_lambda(lam_ref, lam_init):
    lp = lam_ref[...]
    a = jnp.sum(lp[0:1] * lp[1:2], axis=-1, keepdims=True)
    b = jnp.sum(lp[2:3] * lp[3:4], axis=-1, keepdims=True)
    return jnp.exp(a) - jnp.exp(b) + lam_init


def _sub_layer_norm(o, g_ref, lam_init):
    return o * lax.rsqrt(jnp.mean(o * o, axis=-1, keepdims=True) + SUBLN_EPS) * (g_ref[...] * (1.0 - lam_init))


def _dot_nt(a, b):
    return lax.dot_general(a, b, (((1,), (1,)), ((), ())), preferred_element_type=F32)


def _flash_kernel(q_ref, k_ref, v_ref, km_ref, vm_ref, qm_ref, bn_ref, bm_ref, bmm_ref, lam_ref, g_ref,
                  o_ref, om_ref, vt_sc, s0_sc, s1_sc, m_sc, l_sc, acc_sc, *, head_dim, lam_init):
    qi = pl.program_id(2)
    tq = q_ref.shape[0]
    n_local, _, ck, _ = bn_ref.shape
    ratio = tq // ck
    n_chunks = k_ref.shape[0] // ck
    n_meta = km_ref.shape[0]
    vd = 2 * head_dim
    heads = range(n_local)
    lam = _diff_lambda(lam_ref, lam_init)
    cols = lambda hh: slice(hh * vd, (hh + 1) * vd)

    @pl.when(qi == 0)
    def _():
        for hh in heads:
            def body(c, carry):
                rs = pl.ds(pl.multiple_of(c * ck, ck), ck)
                vt_sc[hh, c] = v_ref[rs, cols(hh)].astype(F32).T.astype(BF16)
                return carry

            lax.fori_loop(0, n_chunks, body, 0)
            vm_pad = jnp.concatenate([vm_ref[:, cols(hh)].astype(F32), jnp.zeros((ck - n_meta, vd), F32)], axis=0)
            vt_sc[hh, n_chunks] = vm_pad.T.astype(BF16)

    def scores_t(hh, keys, bias_t):
        parts = [_dot_nt(keys[:, m * head_dim:(m + 1) * head_dim],
                         q_ref[:, hh * vd + m * head_dim:hh * vd + (m + 1) * head_dim]) for m in range(2)]
        if bias_t is not None:
            parts = [p + bias_t for p in parts]
        return jnp.concatenate(parts, axis=1)

    for hh in heads:
        s = scores_t(hh, km_ref[:, cols(hh)], jnp.where(qi == 0, bm_ref[hh], 0.0))
        m0 = jnp.max(s, axis=0, keepdims=True)
        p = jnp.exp2(s - m0)
        m_sc[hh] = m0
        l_sc[hh] = jnp.sum(p, axis=0, keepdims=True)
        p_pad = jnp.concatenate([p, jnp.zeros((V7X_LANES - n_meta, 2 * tq), F32)], axis=0).astype(BF16)
        acc_sc[hh] = jnp.dot(vt_sc[hh, n_chunks][:, :V7X_LANES], p_pad, preferred_element_type=F32)

    assert ratio == 1
    s_bufs = (s0_sc, s1_sc)

    def scores_into(c, tile, s_buf):
        for hh in heads:
            kb = k_ref[pl.ds(pl.multiple_of(c * ck, ck), ck), cols(hh)]
            s_buf[hh] = scores_t(hh, kb, None if tile is None else bn_ref[hh, tile])

    def update_from(c, s_buf):
        for hh in heads:
            s = s_buf[hh]
            m_old = m_sc[hh]
            m_new = jnp.maximum(m_old, jnp.max(s, axis=0, keepdims=True))
            alpha = jnp.exp2(m_old - m_new)
            p = jnp.exp2(s - m_new)
            l_sc[hh] = alpha * l_sc[hh] + jnp.sum(p, axis=0, keepdims=True)
            acc_sc[hh] = alpha * acc_sc[hh] + jnp.dot(vt_sc[hh, c], p.astype(BF16), preferred_element_type=F32)
            m_sc[hh] = m_new

    def by_parity(c, fn):
        for par in range(2):
            @pl.when(jnp.bitwise_and(c, 1) == par)
            def _():
                fn(s_bufs[par], s_bufs[1 - par])

    def advance(c, next_tile):
        def fn(cur, nxt):
            scores_into(c + 1, next_tile, nxt)
            update_from(c, cur)
        by_parity(c, fn)

    @pl.when(qi == 0)
    def _():
        scores_into(0, 1, s_bufs[0])

    @pl.when(qi == 1)
    def _():
        scores_into(0, 0, s_bufs[0])

    @pl.when(qi >= 2)
    def _():
        scores_into(0, None, s_bufs[0])

        def far_body(c, carry):
            advance(c, None)
            return carry

        lax.fori_loop(0, qi - 2, far_body, 0)
        advance(qi - 2, 0)

    @pl.when(qi >= 1)
    def _():
        advance(qi - 1, 1)

    by_parity(qi, lambda cur, nxt: update_from(qi, cur))

    for hh in heads:
        on = acc_sc[hh] * (1.0 / l_sc[hh])
        o_t = on[:, :tq] - lam * on[:, tq:]
        o_t = o_t * lax.rsqrt(jnp.mean(o_t * o_t, axis=0, keepdims=True) + SUBLN_EPS)
        o_ref[:, cols(hh)] = (o_t.T * (g_ref[...] * (1.0 - lam_init))).astype(BF16)

    @pl.when(qi == 0)
    def _():
        for hh in heads:
            qm = qm_ref[:, cols(hh)]
            km = km_ref[:, cols(hh)]
            sm = jnp.concatenate([_dot_nt(qm[:, m * head_dim:(m + 1) * head_dim],
                                          km[:, m * head_dim:(m + 1) * head_dim]) + bmm_ref[hh]
                                  for m in range(2)], axis=0)
            pm = jnp.exp2(sm - jnp.max(sm, axis=-1, keepdims=True))
            onm = (jnp.dot(pm.astype(BF16), vm_ref[:, cols(hh)], preferred_element_type=F32)
                   * (1.0 / jnp.sum(pm, axis=-1, keepdims=True)))
            om_ref[:, cols(hh)] = _sub_layer_norm(onm[:n_meta] - lam * onm[n_meta:], g_ref, lam_init)


def _flash_prompt(qkv16, bias_near, bias_meta, bias_mm, lam_params, subln, layer, lam_init, *,
                  n_batch, seq, n_heads, head_dim, n_meta):
    m, d3 = qkv16.shape
    d = d3 // 3
    vd = 2 * head_dim
    tq = FLASH_TQ
    hp = FLASH_HEADS
    assert n_heads % hp == 0
    ng = n_heads // hp
    nq = seq // tq
    mp = n_batch * seq
    meta_blk = mp // n_meta
    kern = functools.partial(_flash_kernel, head_dim=head_dim, lam_init=lam_init)
    return pl.pallas_call(
        kern,
        grid=(n_batch, ng, nq),
        in_specs=[pl.BlockSpec((tq, hp * vd), lambda b, h, qi: (b * nq + qi, h)),
                  pl.BlockSpec((seq, hp * vd), lambda b, h, qi: (b, ng + h)),
                  pl.BlockSpec((seq, hp * vd), lambda b, h, qi: (b, 2 * ng + h)),
                  pl.BlockSpec((n_meta, hp * vd), lambda b, h, qi: (meta_blk + b, ng + h)),
                  pl.BlockSpec((n_meta, hp * vd), lambda b, h, qi: (meta_blk + b, 2 * ng + h)),
                  pl.BlockSpec((n_meta, hp * vd), lambda b, h, qi: (meta_blk + b, h)),
                  pl.BlockSpec((hp,) + bias_near.shape[1:], lambda b, h, qi: (h, 0, 0, 0)),
                  pl.BlockSpec((hp,) + bias_meta.shape[1:], lambda b, h, qi: (h, 0, 0)),
                  pl.BlockSpec((hp,) + bias_mm.shape[1:], lambda b, h, qi: (h, 0, 0)),
                  pl.BlockSpec((None, 4, head_dim), lambda b, h, qi: (layer, 0, 0)),
                  pl.BlockSpec((None, 1, vd), lambda b, h, qi: (layer, 0, 0))],
        out_specs=[pl.BlockSpec((tq, hp * vd), lambda b, h, qi: (b * nq + qi, h)),
                   pl.BlockSpec((n_meta, hp * vd), lambda b, h, qi: (b, h))],
        out_shape=[jax.ShapeDtypeStruct((mp, d), BF16),
                   jax.ShapeDtypeStruct((n_batch * n_meta, d), F32)],
        scratch_shapes=[pltpu.VMEM((hp, seq // FLASH_CK + 1, vd, FLASH_CK), BF16),
                        pltpu.VMEM((hp, FLASH_CK, 2 * tq), F32), pltpu.VMEM((hp, FLASH_CK, 2 * tq), F32),
                        pltpu.VMEM((hp, 1, 2 * tq), F32), pltpu.VMEM((hp, 1, 2 * tq), F32),
                        pltpu.VMEM((hp, vd, 2 * tq), F32)],
        compiler_params=_cparams(("arbitrary", "arbitrary", "arbitrary")),
        name="flash_prompt",
    )(qkv16, qkv16, qkv16, qkv16, qkv16, qkv16, bias_near, bias_meta, bias_mm, lam_params, subln)


def _decode_kernel(pt_ref, q_ref, kn_ref, vn_ref, *rest, n_heads, head_dim, n_pages_step, lam_init):
    kp_refs = rest[:n_pages_step]
    vp_refs = rest[n_pages_step:2 * n_pages_step]
    bd_ref, bnw_ref, lam_ref, g_ref, o_ref, qc_sc, hm_sc, m_sc, l_sc, acc_sc = rest[2 * n_pages_step:]
    st = pl.program_id(1)
    last = pl.num_programs(1) - 1
    dec_seq, d = q_ref.shape
    vd = 2 * head_dim
    map_rows = n_heads * dec_seq
    n_groups = 2 * n_heads

    @pl.when(st == 0)
    def _():
        q = q_ref[...]
        for mp in range(2):
            qc_sc[mp] = jnp.concatenate(
                [q[:, (2 * h + mp) * head_dim:(2 * h + mp + 1) * head_dim] for h in range(n_heads)],
                axis=0).astype(BF16)
        r = lax.broadcasted_iota(jnp.int32, hm_sc.shape, 0)
        c = lax.broadcasted_iota(jnp.int32, hm_sc.shape, 1)
        hm_sc[...] = jnp.where((r // dec_seq) % n_heads == c % n_heads, 0.0, MASK_VALUE)

        qt = jnp.concatenate([q] * n_groups, axis=0)
        grp = lax.broadcasted_iota(jnp.int32, qt.shape, 0) // dec_seq
        col = lax.broadcasted_iota(jnp.int32, qt.shape, 1) // head_dim
        qb = jnp.where(col == (grp % n_heads) * 2 + grp // n_heads, qt, 0.0).astype(BF16)
        pad = jnp.zeros((dec_seq, d), F32)
        kn = jnp.concatenate([kn_ref[...], pad], axis=0).astype(BF16)
        vn = jnp.concatenate([vn_ref[...], pad], axis=0).astype(BF16)
        s = _dot_nt(qb, kn) + bnw_ref[...]
        m0 = jnp.max(s, axis=-1, keepdims=True)
        p = jnp.exp2(s - m0)
        m_sc[...] = m0
        l_sc[...] = jnp.sum(p, axis=-1, keepdims=True)
        full = jnp.dot(p.astype(BF16), vn, preferred_element_type=F32)
        acc_sc[...] = jnp.concatenate(
            [full[g * dec_seq:(g + 1) * dec_seq, (g % n_heads) * vd:(g % n_heads + 1) * vd]
             for g in range(n_groups)], axis=0)

    th_rows = vp_refs[0].shape[0]
    lanes = DECODE_GROUP * th_rows
    head_mask = jnp.tile(hm_sc[...], (1, lanes // hm_sc.shape[1]))

    def scores(g):
        pages = kp_refs[g * DECODE_GROUP:(g + 1) * DECODE_GROUP]
        s = jnp.concatenate(
            [_dot_nt(qc_sc[mp], jnp.concatenate([kp[pl.ds(mp, th_rows, stride=2), :] for kp in pages],
                                                axis=0).astype(BF16)) for mp in range(2)], axis=0)
        return s + head_mask + jnp.where(st == last, bd_ref[:, g * lanes:(g + 1) * lanes], 0.0)

    n_groups_step = n_pages_step // DECODE_GROUP
    group_scores = [scores(g) for g in range(n_groups_step)]
    for g in range(n_groups_step):
        s = group_scores[g]
        v16 = jnp.concatenate([vp[...].astype(BF16) for vp in vp_refs[g * DECODE_GROUP:(g + 1) * DECODE_GROUP]],
                              axis=0)
        m_old = m_sc[...]
        m_new = jnp.maximum(m_old, jnp.max(s, axis=-1, keepdims=True))
        alpha = jnp.exp2(m_old - m_new)
        p = jnp.exp2(s - m_new)
        l_sc[...] = alpha * l_sc[...] + jnp.sum(p, axis=-1, keepdims=True)
        acc_sc[...] = alpha * acc_sc[...] + jnp.dot(p.astype(BF16), v16, preferred_element_type=F32)
        m_sc[...] = m_new

    @pl.when(st == last)
    def _():
        lam = _diff_lambda(lam_ref, lam_init)
        on = acc_sc[...] * (1.0 / l_sc[...])
        for h in range(n_heads):
            oh = (on[h * dec_seq:(h + 1) * dec_seq]
                  - lam * on[map_rows + h * dec_seq:map_rows + (h + 1) * dec_seq])
            o_ref[:, h * vd:(h + 1) * vd] = _sub_layer_norm(oh, g_ref, lam_init)


def _decode_attn(q32, kv32, cache_k, cache_v, page_table, bias_past, bias_new, lam_params, subln, layer,
                 lam_init, *, new_row0, n_heads, head_dim):
    n_rows, d = q32.shape
    dec_batch, n_pages = page_table.shape
    dec_seq = n_rows // dec_batch
    n_attn, n_pool, page = cache_k.shape[:3]
    ck = cache_k.reshape(n_attn, n_pool, page * 2 * n_heads, head_dim)
    cv = cache_v.reshape(n_attn, n_pool, page * n_heads, 2 * head_dim)
    nps = DECODE_PAGES
    assert n_pages % nps == 0 and nps % DECODE_GROUP == 0 and new_row0 % dec_seq == 0
    n_steps = n_pages // nps
    vd = 2 * head_dim
    rows = 2 * n_heads * dec_seq
    new_blk = new_row0 // dec_seq

    def page_spec(t, arr):
        return pl.BlockSpec((None, None) + arr.shape[2:],
                            lambda bd, st, pt: (layer, pt[bd * n_pages + st * nps + t], 0, 0))

    grid_spec = pltpu.PrefetchScalarGridSpec(
        num_scalar_prefetch=1,
        grid=(dec_batch, n_steps),
        in_specs=[pl.BlockSpec((dec_seq, d), lambda bd, st, pt: (bd, 0)),
                  pl.BlockSpec((dec_seq, d), lambda bd, st, pt: (new_blk + bd, 0)),
                  pl.BlockSpec((dec_seq, d), lambda bd, st, pt: (new_blk + bd, 1))]
                 + [page_spec(t, ck) for t in range(nps)] + [page_spec(t, cv) for t in range(nps)]
                 + [pl.BlockSpec(bias_past.shape, lambda bd, st, pt: (0, 0)),
                    pl.BlockSpec(bias_new.shape, lambda bd, st, pt: (0, 0)),
                    pl.BlockSpec((None, 4, head_dim), lambda bd, st, pt: (layer, 0, 0)),
                    pl.BlockSpec((None, 1, vd), lambda bd, st, pt: (layer, 0, 0))],
        out_specs=pl.BlockSpec((dec_seq, d), lambda bd, st, pt: (bd, 0)),
        scratch_shapes=[pltpu.VMEM((2, rows // 2, head_dim), BF16), pltpu.VMEM((rows, V7X_LANES), F32),
                        pltpu.VMEM((rows, 1), F32), pltpu.VMEM((rows, 1), F32),
                        pltpu.VMEM((rows, vd), F32)],
    )
    kern = functools.partial(_decode_kernel, n_heads=n_heads, head_dim=head_dim, n_pages_step=nps,
                             lam_init=lam_init)
    return pl.pallas_call(
        kern,
        grid_spec=grid_spec,
        out_shape=jax.ShapeDtypeStruct((n_rows, d), F32),
        compiler_params=_cparams(("arbitrary", "arbitrary")),
        name="decode_attn",
    )(page_table.reshape(-1), q32, kv32, kv32, *([ck] * nps), *([cv] * nps),
      bias_past, bias_new, lam_params, subln)


def kernel(x_prompt, x_sample, cache_k, cache_v, state_conv, page_table, meta_tokens, rel_bias,
           attn_w_qkv, attn_lambda, attn_subln, attn_w_o, conv_w_in, conv_w, conv_w_out,
           mlp_w_up, mlp_w_down, ln_mix_g, ln_mix_b, ln_mlp_g, ln_mlp_b):
    n_batch, seq, d = x_prompt.shape
    dec_batch, dec_seq, _ = x_sample.shape
    n_meta = meta_tokens.shape[0]
    n_buckets, n_heads = rel_bias.shape
    head_dim = cache_k.shape[-1]
    vd = cache_v.shape[-1]
    page = cache_k.shape[2]
    depth = mlp_w_up.shape[0]
    assert d == n_heads * vd and vd == 2 * head_dim
    alpha = (2 * depth) ** 0.25

    mp = n_batch * seq
    n_meta_rows = n_batch * n_meta
    n_sample_rows = dec_batch * dec_seq
    assert n_meta_rows + n_sample_rows <= SMALL_ROWS and seq % FLASH_TQ == 0
    assert n_meta % V7X_SUBLANES == 0 and dec_seq % V7X_SUBLANES == 0
    meta0, samp0 = mp, mp + n_meta_rows
    mt = mp + SMALL_ROWS

    x = jnp.concatenate(
        [x_prompt.reshape(mp, d)] + [meta_tokens.astype(F32)] * n_batch
        + [x_sample.reshape(n_sample_rows, d), jnp.zeros((mt - samp0 - n_sample_rows, d), F32)], axis=0)

    far = _far_distance(n_buckets)
    tq, ck, nps = FLASH_TQ, FLASH_CK, DECODE_PAGES
    ratio = tq // ck
    assert ck + 1 >= far and n_meta + tq - (n_meta - 1) >= far and (nps - 1) * page + 1 >= far
    qq = np.arange(tq)[None, :]
    near_idx = np.stack([_bucket_tile(qq - (t - 1) * ck - np.arange(ck)[:, None], n_buckets)
                         for t in range(ratio + 1)])
    meta_idx = _bucket_tile(n_meta + qq - np.arange(n_meta)[:, None], n_buckets)
    mm_idx = _bucket_tile(np.arange(n_meta)[:, None] - np.arange(n_meta)[None, :], n_buckets)
    step_of_row = (np.arange(2 * dec_seq) % dec_seq)[:, None]
    past_token = (np.arange(nps * page * n_heads) // n_heads)[None, :]
    past_idx = _bucket_tile(nps * page + step_of_row - past_token, n_buckets)
    new_cols = np.arange(2 * dec_seq)[None, :]
    new_idx = np.where(new_cols < dec_seq, _bucket_tile(step_of_row - new_cols, n_buckets), -1).astype(np.int32)

    bias_near = _bias_expand(near_idx.reshape((ratio + 1) * ck, tq), rel_bias).reshape(
        n_heads, ratio + 1, ck, tq)
    bias_meta = _bias_expand(meta_idx, rel_bias)
    bias_mm = _bias_expand(mm_idx, rel_bias)

    def decode_rows(tile):
        lanes = tile.shape[-1]
        return tile.reshape(n_heads, 2, dec_seq, lanes).transpose(1, 0, 2, 3).reshape(2 * n_heads * dec_seq, lanes)

    bias_past = decode_rows(_bias_expand(past_idx, rel_bias))
    bias_new = decode_rows(_bias_expand(new_idx, rel_bias))

    def conv_history(state):
        h1 = jnp.zeros((dec_batch, dec_seq, d), F32).at[:, 0].set(state[:, 1])
        h2 = jnp.zeros((dec_batch, dec_seq, d), F32).at[:, 0].set(state[:, 0]).at[:, 1].set(state[:, 1])
        pad_lo = jnp.zeros((n_meta_rows, d), F32)
        pad_hi = jnp.zeros((SMALL_ROWS - n_meta_rows - n_sample_rows, d), F32)
        cat = lambda h: jnp.concatenate([pad_lo, h.reshape(n_sample_rows, d), pad_hi], axis=0)
        return cat(h1), cat(h2)

    subln3 = attn_subln.reshape(attn_subln.shape[0], 1, vd)
    kv_layers, c_p, c_s = [], [], []
    for i in range(depth):
        j = i // N_MIXERS
        g_mix, b_mix = ln_mix_g[i][None], ln_mix_b[i][None]
        if i % N_MIXERS == 0:
            lam0 = _lambda_init(i)
            qkv16, kv32 = _qkv_proj(x, attn_w_qkv, j, head_dim ** -0.5 * LOG2_E)
            o16, o_meta = _flash_prompt(qkv16, bias_near, bias_meta, bias_mm, attn_lambda, subln3, j, lam0,
                                        n_batch=n_batch, seq=seq, n_heads=n_heads, head_dim=head_dim,
                                        n_meta=n_meta)
            q32 = qkv16[samp0:samp0 + n_sample_rows, :d].astype(F32)
            o_samp = _decode_attn(q32, kv32, cache_k, cache_v, page_table, bias_past, bias_new, attn_lambda,
                                  subln3, j, lam0, new_row0=samp0, n_heads=n_heads, head_dim=head_dim)
            o_small = jnp.concatenate(
                [o_meta, o_samp, jnp.zeros((mt - samp0 - n_sample_rows, d), F32)], axis=0).astype(BF16)
            x = _outproj_ln(o16, o_small, attn_w_o, j, x, g_mix, b_mix, alpha)
            kv_layers.append(kv32)
        else:
            bg, u = _conv_in(x, conv_w_in, j)
            hist1, hist2 = conv_history(state_conv[j])
            z16 = _conv_gate(u, bg, hist1, hist2, conv_w, j, seq=seq, n_batch=n_batch, n_meta=n_meta,
                             dec_seq=dec_seq, n_sample_rows=n_sample_rows)
            x = _outproj_ln(z16, None, conv_w_out, j, x, g_mix, b_mix, alpha)
            c_p.append(jnp.stack([u[(b + 1) * seq - 2:(b + 1) * seq] for b in range(n_batch)]))
            c_s.append(u[samp0:samp0 + n_sample_rows].reshape(dec_batch, dec_seq, d)[:, dec_seq - 2:])
        last = i == depth - 1
        x = _mlp_ln(x, mlp_w_up, mlp_w_down, i, ln_mlp_g[i][None], ln_mlp_b[i][None], alpha,
                    n_main=mp if last else None)

    y_main, y_small = x
    y_prompt = y_main.reshape(n_batch, seq, d)
    y_sample = y_small[n_meta_rows:n_meta_rows + n_sample_rows].reshape(dec_batch, dec_seq, d)

    def new_cache(col0, tail):
        cols = slice(col0, col0 + d)
        prompt = jnp.concatenate(
            [jnp.concatenate(
                [jnp.concatenate([kv[meta0 + b * n_meta:meta0 + (b + 1) * n_meta, cols].reshape((1, 1, n_meta) + tail),
                                  kv[b * seq:(b + 1) * seq, cols].reshape((1, 1, seq) + tail)], axis=2)
                 for b in range(n_batch)], axis=1)
             for kv in kv_layers], axis=0)
        sample = jnp.stack([kv[samp0:samp0 + n_sample_rows, cols].reshape((dec_batch, dec_seq) + tail)
                            for kv in kv_layers])
        return prompt, sample

    k_p, k_s = new_cache(0, (n_heads, 2, head_dim))
    v_p, v_s = new_cache(d, (n_heads, vd))
    return (y_prompt, y_sample, k_p, v_p, jnp.stack(c_p), k_s, v_s, jnp.stack(c_s))
```

```python
import functools
import math

import numpy as np
import jax
import jax.numpy as jnp
from jax import lax
from jax.experimental import pallas as pl
from jax.experimental.pallas import tpu as pltpu

F32 = jnp.float32
BF16 = jnp.bfloat16

MAX_DISTANCE = 128
LN_EPS = 1e-5
SUBLN_EPS = 1e-5
N_MIXERS = 2
MASK_VALUE = -1e30
LOG2_E = math.log2(math.e)

V7X_LANES = 128
V7X_SUBLANES = 8
V7X_VMEM_LIMIT_BYTES = 60 * 1024 * 1024

SMALL_ROWS = 128
FLASH_TQ = 512
FLASH_CK = 512
FLASH_HEADS = 2
FLASH_SUM_ROWS = 16
DECODE_PAGES = 8
DECODE_GROUP = 2


def _lambda_init(layer):
    return 0.8 - 0.6 * math.exp(-0.3 * layer)


def _pick_tile(total, target, multiple):
    best = None
    for t in range(multiple, min(total, target) + 1, multiple):
        if total % t == 0:
            best = t
    assert best is not None, (total, target, multiple)
    return best


def _cparams(sem):
    return pltpu.CompilerParams(dimension_semantics=sem, vmem_limit_bytes=V7X_VMEM_LIMIT_BYTES)


def _bucket_of_distance(n, n_buckets):
    n = np.asarray(n, np.int64)
    max_exact = n_buckets // 2
    nf = np.maximum(n, 1).astype(np.float64)
    large = max_exact + (np.log(nf / max_exact) / math.log(MAX_DISTANCE / max_exact)
                         * (n_buckets - max_exact)).astype(np.int64)
    large = np.minimum(large, n_buckets - 1)
    return np.where(n < max_exact, n, large).astype(np.int32)


def _far_distance(n_buckets):
    tab = _bucket_of_distance(np.arange(4 * MAX_DISTANCE), n_buckets)
    not_last = np.nonzero(tab != n_buckets - 1)[0]
    return int(not_last[-1]) + 1


def _bucket_tile(dist, n_buckets):
    dist = np.asarray(dist, np.int64)
    return np.where(dist < 0, -1, _bucket_of_distance(np.maximum(dist, 0), n_buckets)).astype(np.int32)


def _bias_expand_kernel(rb_ref, idx_ref, o_ref, *, n_buckets):
    h = pl.program_id(0)
    idx = idx_ref[...]
    far = rb_ref[n_buckets - 1, h]
    out = jnp.zeros(idx.shape, F32)
    for b in range(n_buckets - 1):
        out = jnp.where(idx == b, (rb_ref[b, h] - far) * LOG2_E, out)
    o_ref[...] = jnp.where(idx < 0, MASK_VALUE, out)


def _bias_expand(idx_np, rel_bias):
    rows, cols = idx_np.shape
    n_buckets, n_heads = rel_bias.shape
    return pl.pallas_call(
        functools.partial(_bias_expand_kernel, n_buckets=n_buckets),
        grid=(n_heads,),
        in_specs=[pl.BlockSpec(memory_space=pltpu.SMEM),
                  pl.BlockSpec((rows, cols), lambda h: (0, 0))],
        out_specs=pl.BlockSpec((None, rows, cols), lambda h: (h, 0, 0)),
        out_shape=jax.ShapeDtypeStruct((n_heads, rows, cols), F32),
        compiler_params=_cparams(("arbitrary",)),
        name="bias_expand",
    )(rel_bias, jnp.asarray(idx_np))


def _qkv_kernel(x_ref, w_ref, o16_ref, kv32_ref, xb_ref, *, n_q_tiles, q_scale):
    j = pl.program_id(1)

    @pl.when(j == 0)
    def _():
        xb_ref[...] = x_ref[...].astype(BF16)

    acc = jnp.dot(xb_ref[...], w_ref[...].astype(BF16), preferred_element_type=F32)
    scale = jnp.where(j < n_q_tiles, q_scale, 1.0).astype(F32)
    o16_ref[...] = (acc * scale).astype(BF16)

    @pl.when(j >= n_q_tiles)
    def _():
        kv32_ref[...] = acc


def _qkv_proj(x, w_qkv, layer, q_scale):
    m, d = x.shape
    tm = _pick_tile(m, 1040, 16)
    tn = 512
    n_q_tiles = d // tn
    return pl.pallas_call(
        functools.partial(_qkv_kernel, n_q_tiles=n_q_tiles, q_scale=q_scale),
        grid=(m // tm, 3 * d // tn),
        in_specs=[pl.BlockSpec((tm, d), lambda i, j: (i, 0)),
                  pl.BlockSpec((None, d, tn), lambda i, j: (layer, 0, j))],
        out_specs=[pl.BlockSpec((tm, tn), lambda i, j: (i, j)),
                   pl.BlockSpec((tm, tn), lambda i, j: (i, jnp.maximum(j - n_q_tiles, 0)))],
        out_shape=[jax.ShapeDtypeStruct((m, 3 * d), BF16),
                   jax.ShapeDtypeStruct((m, 2 * d), F32)],
        scratch_shapes=[pltpu.VMEM((tm, d), BF16)],
        compiler_params=_cparams(("arbitrary", "arbitrary")),
        name="qkv_proj",
    )(x, w_qkv)


def _convin_kernel(x_ref, wb_ref, wc_ref, wh_ref, bg_ref, u_ref, xb_ref):
    @pl.when(pl.program_id(1) == 0)
    def _():
        xb_ref[...] = x_ref[...].astype(BF16)

    xb = xb_ref[...]
    bg_ref[...] = jnp.dot(xb, wb_ref[...].astype(BF16), preferred_element_type=F32)
    cg = jnp.dot(xb, wc_ref[...].astype(BF16), preferred_element_type=F32)
    hh = jnp.dot(xb, wh_ref[...].astype(BF16), preferred_element_type=F32)
    u_ref[...] = cg * hh


def _conv_in(x, w_in, layer):
    m, d = x.shape
    tm = _pick_tile(m, 1040, 16)
    tn = 256
    nt = d // tn
    wspec = lambda part: pl.BlockSpec((None, d, tn), lambda i, j: (layer, 0, part * nt + j))
    return pl.pallas_call(
        _convin_kernel,
        grid=(m // tm, nt),
        in_specs=[pl.BlockSpec((tm, d), lambda i, j: (i, 0)), wspec(0), wspec(1), wspec(2)],
        out_specs=[pl.BlockSpec((tm, tn), lambda i, j: (i, j)),
                   pl.BlockSpec((tm, tn), lambda i, j: (i, j))],
        out_shape=[jax.ShapeDtypeStruct((m, d), F32), jax.ShapeDtypeStruct((m, d), F32)],
        scratch_shapes=[pltpu.VMEM((tm, d), BF16)],
        compiler_params=_cparams(("arbitrary", "arbitrary")),
        name="conv_in",
    )(x, w_in, w_in, w_in)


def _convgate_kernel(u_ref, bg_ref, halo_ref, meta_ref, h1_ref, h2_ref, w_ref, z_ref, *,
                     seq_blocks, n_prompt_blocks, n_batch, n_meta, dec_seq, n_sample_rows):
    i = pl.program_id(0)
    u = u_ref[...]
    rows = u.shape[0]
    r = lax.broadcasted_iota(jnp.int32, (rows, 1), 0)

    halo = halo_ref[...]
    prev2, prev1 = halo[V7X_SUBLANES - 2:V7X_SUBLANES - 1], halo[V7X_SUBLANES - 1:V7X_SUBLANES]
    meta = meta_ref[...]
    for b in range(n_batch):
        starts_batch = i == b * seq_blocks
        prev2 = jnp.where(starts_batch, meta[(b + 1) * n_meta - 2:(b + 1) * n_meta - 1], prev2)
        prev1 = jnp.where(starts_batch, meta[(b + 1) * n_meta - 1:(b + 1) * n_meta], prev1)

    p1 = jnp.where(r == 0, prev1, pltpu.roll(u, 1, axis=0))
    p2 = jnp.where(r == 0, prev2, jnp.where(r == 1, prev1, pltpu.roll(u, 2, axis=0)))

    small = i == n_prompt_blocks
    n_meta_rows = n_batch * n_meta
    in_meta = r < n_meta_rows
    in_sample = jnp.logical_and(r >= n_meta_rows, r < n_meta_rows + n_sample_rows)
    pos = jnp.where(in_meta, r % n_meta, (r - n_meta_rows) % dec_seq)
    restart = jnp.logical_or(in_meta, in_sample)
    m1 = jnp.logical_and(small, jnp.logical_and(restart, pos == 0))
    m2 = jnp.logical_and(small, jnp.logical_and(restart, pos <= 1))
    p1 = jnp.where(m1, h1_ref[...], p1)
    p2 = jnp.where(m2, h2_ref[...], p2)

    w = w_ref[...]
    conv = w[0:1] * p2 + w[1:2] * p1 + w[2:3] * u
    z_ref[...] = (bg_ref[...] * conv).astype(BF16)


def _conv_gate(u, bg, hist1, hist2, w_conv, layer, *, seq, n_batch, n_meta, dec_seq, n_sample_rows):
    m, d = u.shape
    rows = SMALL_ROWS
    n_prompt_blocks = n_batch * seq // rows
    assert m == (n_prompt_blocks + 1) * rows and seq % rows == 0
    halo_per_block = rows // V7X_SUBLANES
    kern = functools.partial(_convgate_kernel, seq_blocks=seq // rows, n_prompt_blocks=n_prompt_blocks,
                             n_batch=n_batch, n_meta=n_meta, dec_seq=dec_seq, n_sample_rows=n_sample_rows)
    return pl.pallas_call(
        kern,
        grid=(m // rows,),
        in_specs=[pl.BlockSpec((rows, d), lambda i: (i, 0)),
                  pl.BlockSpec((rows, d), lambda i: (i, 0)),
                  pl.BlockSpec((V7X_SUBLANES, d), lambda i: (jnp.maximum(i * halo_per_block - 1, 0), 0)),
                  pl.BlockSpec((rows, d), lambda i: (n_prompt_blocks, 0)),
                  pl.BlockSpec((rows, d), lambda i: (0, 0)),
                  pl.BlockSpec((rows, d), lambda i: (0, 0)),
                  pl.BlockSpec((None, 3, d), lambda i: (layer, 0, 0))],
        out_specs=pl.BlockSpec((rows, d), lambda i: (i, 0)),
        out_shape=jax.ShapeDtypeStruct((m, d), BF16),
        compiler_params=_cparams(("arbitrary",)),
        name="conv_gate",
    )(u, bg, u, u, hist1, hist2, w_conv)


def _residual_layer_norm(x_ref, o_ref, g_ref, b_ref, alpha, rows_chunk):
    g = g_ref[...]
    b = b_ref[...]

    def body(c, carry):
        rs = pl.ds(pl.multiple_of(c * rows_chunk, rows_chunk), rows_chunk)
        y = alpha * x_ref[rs, :] + o_ref[rs, :]
        mu = jnp.mean(y, axis=-1, keepdims=True)
        yc = y - mu
        var = jnp.mean(yc * yc, axis=-1, keepdims=True)
        o_ref[rs, :] = yc * lax.rsqrt(var + LN_EPS) * g + b
        return carry

    lax.fori_loop(0, x_ref.shape[0] // rows_chunk, body, 0)


def _outproj_kernel(a_ref, *rest, alpha, rows_chunk, n_main_last):
    small_ref = rest[0] if n_main_last is not None else None
    w_ref, x_ref, g_ref, b_ref, o_ref = rest[-5:]
    i = pl.program_id(0)
    k = pl.program_id(1)
    last_tile = pl.num_programs(0) - 1

    @pl.when(k == 0)
    def _():
        o_ref[...] = jnp.zeros_like(o_ref)

    w = w_ref[...].astype(BF16)
    if small_ref is None:
        o_ref[...] += jnp.dot(a_ref[...], w, preferred_element_type=F32)
    else:
        @pl.when(i < last_tile)
        def _():
            o_ref[...] += jnp.dot(a_ref[...], w, preferred_element_type=F32)

        @pl.when(i == last_tile)
        def _():
            a = jnp.concatenate([a_ref[:n_main_last], small_ref[...]], axis=0)
            o_ref[...] += jnp.dot(a, w, preferred_element_type=F32)

    @pl.when(k == pl.num_programs(1) - 1)
    def _():
        _residual_layer_norm(x_ref, o_ref, g_ref, b_ref, alpha, rows_chunk)


def _outproj_ln(a16, a16_small, w, layer, x, g, b, alpha):
    m, d = x.shape
    tm = _pick_tile(m, 1040, 16)
    tk = 512
    rows_chunk = _pick_tile(tm, 208, V7X_SUBLANES)
    if a16_small is None:
        assert a16.shape[0] == m
        n_main_last, small_args, small_specs = None, [], []
    else:
        n_small = a16_small.shape[0]
        assert a16.shape[0] + n_small == m and n_small <= tm
        n_main_last = tm - n_small
        small_args = [a16_small]
        small_specs = [pl.BlockSpec((n_small, tk), lambda i, k: (0, k))]
    return pl.pallas_call(
        functools.partial(_outproj_kernel, alpha=alpha, rows_chunk=rows_chunk, n_main_last=n_main_last),
        grid=(m // tm, d // tk),
        in_specs=[pl.BlockSpec((tm, tk), lambda i, k: (i, k))] + small_specs + [
                  pl.BlockSpec((None, tk, d), lambda i, k: (layer, k, 0)),
                  pl.BlockSpec((tm, d), lambda i, k: (i, 0)),
                  pl.BlockSpec((1, d), lambda i, k: (0, 0)),
                  pl.BlockSpec((1, d), lambda i, k: (0, 0))],
        out_specs=pl.BlockSpec((tm, d), lambda i, k: (i, 0)),
        out_shape=jax.ShapeDtypeStruct((m, d), F32),
        compiler_params=_cparams(("arbitrary", "arbitrary")),
        name="outproj_ln",
    )(a16, *small_args, w, x, g, b)


def _mlp_kernel(x_ref, wu_ref, wd_ref, g_ref, b_ref, o_ref, *rest, alpha, rows_chunk, n_main_last):
    small_ref = rest[0] if n_main_last is not None else None
    xb_ref = rest[-1]
    f = pl.program_id(1)

    @pl.when(f == 0)
    def _():
        xb_ref[...] = x_ref[...].astype(BF16)
        o_ref[...] = jnp.zeros_like(o_ref)

    h = jnp.dot(xb_ref[...], wu_ref[...].astype(BF16), preferred_element_type=F32)
    h = jnp.square(jnp.maximum(h, 0.0)).astype(BF16)
    o_ref[...] += jnp.dot(h, wd_ref[...].astype(BF16), preferred_element_type=F32)

    @pl.when(f == pl.num_programs(1) - 1)
    def _():
        _residual_layer_norm(x_ref, o_ref, g_ref, b_ref, alpha, rows_chunk)
        if small_ref is not None:
            @pl.when(pl.program_id(0) == pl.num_programs(0) - 1)
            def _():
                small_ref[...] = o_ref[n_main_last:, :]


def _mlp_ln(x, w_up, w_down, layer, g, b, alpha, n_main=None):
    m, d = x.shape
    d_ff = w_up.shape[-1]
    tm = _pick_tile(m, 832, 16)
    tf = 512
    rows_chunk = _pick_tile(tm, 208, V7X_SUBLANES)
    out_specs = pl.BlockSpec((tm, d), lambda i, f: (i, 0))
    out_shape = jax.ShapeDtypeStruct((m, d), F32)
    n_main_last = None
    if n_main is not None:
        n_small = m - n_main
        assert 0 < n_small <= tm
        n_main_last = tm - n_small
        out_specs = [out_specs, pl.BlockSpec((n_small, d), lambda i, f: (0, 0))]
        out_shape = [jax.ShapeDtypeStruct((n_main, d), F32), jax.ShapeDtypeStruct((n_small, d), F32)]
    return pl.pallas_call(
        functools.partial(_mlp_kernel, alpha=alpha, rows_chunk=rows_chunk, n_main_last=n_main_last),
        grid=(m // tm, d_ff // tf),
        in_specs=[pl.BlockSpec((tm, d), lambda i, f: (i, 0)),
                  pl.BlockSpec((None, d, tf), lambda i, f: (layer, 0, f)),
                  pl.BlockSpec((None, tf, d), lambda i, f: (layer, f, 0)),
                  pl.BlockSpec((1, d), lambda i, f: (0, 0)),
                  pl.BlockSpec((1, d), lambda i, f: (0, 0))],
        out_specs=out_specs,
        out_shape=out_shape,
        scratch_shapes=[pltpu.VMEM((tm, d), BF16)],
        compiler_params=_cparams(("arbitrary", "arbitrary")),
        name="mlp_ln",
    )(x, w_up, w_down, g, b)


def _diff_lambda(lam_ref, lam_init):
    lp = lam_ref[...]
    a = jnp.sum(lp[0:1] * lp[1:2], axis=-1, keepdims=True)
    b = jnp.sum(lp[2:3] * lp[3:4], axis=-1, keepdims=True)
    return jnp.exp(a) - jnp.exp(b) + lam_init


def _sub_layer_norm(o, g_ref, lam_init):
    return o * lax.rsqrt(jnp.mean(o * o, axis=-1, keepdims=True) + SUBLN_EPS) * (g_ref[...] * (1.0 - lam_init))


def _dot_nt(a, b):
    return lax.dot_general(a, b, (((1,), (1,)), ((), ())), preferred_element_type=F32)


def _flash_kernel(q_ref, k_ref, v_ref, km_ref, vm_ref, qm_ref, bn_ref, bm_ref, bmm_ref, lam_ref, g_ref,
                  o_ref, om_ref, vt_sc, s0_sc, s1_sc, m_sc, acc_sc, *, head_dim, lam_init):
    qi = pl.program_id(2)
    tq = q_ref.shape[0]
    n_local, _, ck, _ = bn_ref.shape
    ratio = tq // ck
    n_chunks = k_ref.shape[0] // ck
    n_meta = km_ref.shape[0]
    vd = 2 * head_dim
    heads = range(n_local)
    lam = _diff_lambda(lam_ref, lam_init)
    cols = lambda hh: slice(hh * vd, (hh + 1) * vd)

    @pl.when(qi == 0)
    def _():
        extra = vt_sc.shape[2] - vd
        ones_rows = (lax.broadcasted_iota(jnp.int32, (extra, ck), 0) == 0).astype(BF16)
        for hh in heads:
            def body(c, carry):
                rs = pl.ds(pl.multiple_of(c * ck, ck), ck)
                vt_sc[hh, c, :vd] = v_ref[rs, cols(hh)].astype(F32).T.astype(BF16)
                vt_sc[hh, c, vd:] = ones_rows
                return carry

            lax.fori_loop(0, n_chunks, body, 0)
            vm_pad = jnp.concatenate([vm_ref[:, cols(hh)].astype(F32), jnp.zeros((ck - n_meta, vd), F32)], axis=0)
            vt_sc[hh, n_chunks, :vd] = vm_pad.T.astype(BF16)
            vt_sc[hh, n_chunks, vd:] = ones_rows

    def scores_t(hh, keys, bias_t):
        parts = [_dot_nt(keys[:, m * head_dim:(m + 1) * head_dim],
                         q_ref[:, hh * vd + m * head_dim:hh * vd + (m + 1) * head_dim]) for m in range(2)]
        if bias_t is not None:
            parts = [p + bias_t for p in parts]
        return jnp.concatenate(parts, axis=1)

    for hh in heads:
        s = scores_t(hh, km_ref[:, cols(hh)], jnp.where(qi == 0, bm_ref[hh], 0.0))
        m0 = jnp.max(s, axis=0, keepdims=True)
        p = jnp.exp2(s - m0)
        m_sc[hh] = m0
        p_pad = jnp.concatenate([p, jnp.zeros((V7X_LANES - n_meta, 2 * tq), F32)], axis=0).astype(BF16)
        acc_sc[hh] = jnp.dot(vt_sc[hh, n_chunks][:, :V7X_LANES], p_pad, preferred_element_type=F32)

    assert ratio == 1
    s_bufs = (s0_sc, s1_sc)

    def scores_into(c, tile, s_buf):
        for hh in heads:
            kb = k_ref[pl.ds(pl.multiple_of(c * ck, ck), ck), cols(hh)]
            s_buf[hh] = scores_t(hh, kb, None if tile is None else bn_ref[hh, tile])

    def update_from(c, s_buf):
        for hh in heads:
            s = s_buf[hh]
            m_old = m_sc[hh]
            m_new = jnp.maximum(m_old, jnp.max(s, axis=0, keepdims=True))
            alpha = jnp.exp2(m_old - m_new)
            p = jnp.exp2(s - m_new)
            acc_sc[hh] = alpha * acc_sc[hh] + jnp.dot(vt_sc[hh, c], p.astype(BF16), preferred_element_type=F32)
            m_sc[hh] = m_new

    def by_parity(c, fn):
        for par in range(2):
            @pl.when(jnp.bitwise_and(c, 1) == par)
            def _():
                fn(s_bufs[par], s_bufs[1 - par])

    def advance(c, next_tile):
        def fn(cur, nxt):
            scores_into(c + 1, next_tile, nxt)
            update_from(c, cur)
        by_parity(c, fn)

    @pl.when(qi == 0)
    def _():
        scores_into(0, 1, s_bufs[0])

    @pl.when(qi == 1)
    def _():
        scores_into(0, 0, s_bufs[0])

    @pl.when(qi >= 2)
    def _():
        scores_into(0, None, s_bufs[0])

        def far_body(c, carry):
            advance(c, None)
            return carry

        lax.fori_loop(0, qi - 2, far_body, 0)
        advance(qi - 2, 0)

    @pl.when(qi >= 1)
    def _():
        advance(qi - 1, 1)

    by_parity(qi, lambda cur, nxt: update_from(qi, cur))

    for hh in heads:
        acc = acc_sc[hh]
        on = acc[:vd] * (1.0 / acc[vd:vd + 1])
        o_t = on[:, :tq] - lam * on[:, tq:]
        o_t = o_t * lax.rsqrt(jnp.mean(o_t * o_t, axis=0, keepdims=True) + SUBLN_EPS)
        o_ref[:, cols(hh)] = (o_t.T * (g_ref[...] * (1.0 - lam_init))).astype(BF16)

    @pl.when(qi == 0)
    def _():
        for hh in heads:
            qm = qm_ref[:, cols(hh)]
            km = km_ref[:, cols(hh)]
            sm = jnp.concatenate([_dot_nt(qm[:, m * head_dim:(m + 1) * head_dim],
                                          km[:, m * head_dim:(m + 1) * head_dim]) + bmm_ref[hh]
                                  for m in range(2)], axis=0)
            pm = jnp.exp2(sm - jnp.max(sm, axis=-1, keepdims=True))
            onm = (jnp.dot(pm.astype(BF16), vm_ref[:, cols(hh)], preferred_element_type=F32)
                   * (1.0 / jnp.sum(pm, axis=-1, keepdims=True)))
            om_ref[:, cols(hh)] = _sub_layer_norm(onm[:n_meta] - lam * onm[n_meta:], g_ref, lam_init)


def _flash_prompt(qkv16, bias_near, bias_meta, bias_mm, lam_params, subln, layer, lam_init, *,
                  n_batch, seq, n_heads, head_dim, n_meta):
    m, d3 = qkv16.shape
    d = d3 // 3
    vd = 2 * head_dim
    tq = FLASH_TQ
    hp = FLASH_HEADS
    assert n_heads % hp == 0
    ng = n_heads // hp
    nq = seq // tq
    mp = n_batch * seq
    meta_blk = mp // n_meta
    kern = functools.partial(_flash_kernel, head_dim=head_dim, lam_init=lam_init)
    return pl.pallas_call(
        kern,
        grid=(n_batch, ng, nq),
        in_specs=[pl.BlockSpec((tq, hp * vd), lambda b, h, qi: (b * nq + qi, h)),
                  pl.BlockSpec((seq, hp * vd), lambda b, h, qi: (b, ng + h)),
                  pl.BlockSpec((seq, hp * vd), lambda b, h, qi: (b, 2 * ng + h)),
                  pl.BlockSpec((n_meta, hp * vd), lambda b, h, qi: (meta_blk + b, ng + h)),
                  pl.BlockSpec((n_meta, hp * vd), lambda b, h, qi: (meta_blk + b, 2 * ng + h)),
                  pl.BlockSpec((n_meta, hp * vd), lambda b, h, qi: (meta_blk + b, h)),
                  pl.BlockSpec((hp,) + bias_near.shape[1:], lambda b, h, qi: (h, 0, 0, 0)),
                  pl.BlockSpec((hp,) + bias_meta.shape[1:], lambda b, h, qi: (h, 0, 0)),
                  pl.BlockSpec((hp,) + bias_mm.shape[1:], lambda b, h, qi: (h, 0, 0)),
                  pl.BlockSpec((None, 4, head_dim), lambda b, h, qi: (layer, 0, 0)),
                  pl.BlockSpec((None, 1, vd), lambda b, h, qi: (layer, 0, 0))],
        out_specs=[pl.BlockSpec((tq, hp * vd), lambda b, h, qi: (b * nq + qi, h)),
                   pl.BlockSpec((n_meta, hp * vd), lambda b, h, qi: (b, h))],
        out_shape=[jax.ShapeDtypeStruct((mp, d), BF16),
                   jax.ShapeDtypeStruct((n_batch * n_meta, d), F32)],
        scratch_shapes=[pltpu.VMEM((hp, seq // FLASH_CK + 1, vd + FLASH_SUM_ROWS, FLASH_CK), BF16),
                        pltpu.VMEM((hp, FLASH_CK, 2 * tq), F32), pltpu.VMEM((hp, FLASH_CK, 2 * tq), F32),
                        pltpu.VMEM((hp, 1, 2 * tq), F32),
                        pltpu.VMEM((hp, vd + FLASH_SUM_ROWS, 2 * tq), F32)],
        compiler_params=_cparams(("arbitrary", "arbitrary", "arbitrary")),
        name="flash_prompt",
    )(qkv16, qkv16, qkv16, qkv16, qkv16, qkv16, bias_near, bias_meta, bias_mm, lam_params, subln)


def _decode_kernel(pt_ref, q_ref, kn_ref, vn_ref, *rest, n_heads, head_dim, n_pages_step, lam_init):
    kp_refs = rest[:n_pages_step]
    vp_refs = rest[n_pages_step:2 * n_pages_step]
    bd_ref, bnw_ref, lam_ref, g_ref, o_ref, qc_sc, hm_sc, m_sc, l_sc, acc_sc = rest[2 * n_pages_step:]
    st = pl.program_id(1)
    last = pl.num_programs(1) - 1
    dec_seq, d = q_ref.shape
    vd = 2 * head_dim
    map_rows = n_heads * dec_seq
    n_groups = 2 * n_heads

    @pl.when(st == 0)
    def _():
        q = q_ref[...]
        for mp in range(2):
            qc_sc[mp] = jnp.concatenate(
                [q[:, (2 * h + mp) * head_dim:(2 * h + mp + 1) * head_dim] for h in range(n_heads)],
                axis=0).astype(BF16)
        r = lax.broadcasted_iota(jnp.int32, hm_sc.shape, 0)
        c = lax.broadcasted_iota(jnp.int32, hm_sc.shape, 1)
        hm_sc[...] = jnp.where((r // dec_seq) % n_heads == c % n_heads, 0.0, MASK_VALUE)

        qt = jnp.concatenate([q] * n_groups, axis=0)
        grp = lax.broadcasted_iota(jnp.int32, qt.shape, 0) // dec_seq
        col = lax.broadcasted_iota(jnp.int32, qt.shape, 1) // head_dim
        qb = jnp.where(col == (grp % n_heads) * 2 + grp // n_heads, qt, 0.0).astype(BF16)
        pad = jnp.zeros((dec_seq, d), F32)
        kn = jnp.concatenate([kn_ref[...], pad], axis=0).astype(BF16)
        vn = jnp.concatenate([vn_ref[...], pad], axis=0).astype(BF16)
        s = _dot_nt(qb, kn) + bnw_ref[...]
        m0 = jnp.max(s, axis=-1, keepdims=True)
        p = jnp.exp2(s - m0)
        m_sc[...] = m0
        l_sc[...] = jnp.sum(p, axis=-1, keepdims=True)
        full = jnp.dot(p.astype(BF16), vn, preferred_element_type=F32)
        acc_sc[...] = jnp.concatenate(
            [full[g * dec_seq:(g + 1) * dec_seq, (g % n_heads) * vd:(g % n_heads + 1) * vd]
             for g in range(n_groups)], axis=0)

    th_rows = vp_refs[0].shape[0]
    lanes = DECODE_GROUP * th_rows
    head_mask = jnp.tile(hm_sc[...], (1, lanes // hm_sc.shape[1]))

    def scores(g):
        pages = kp_refs[g * DECODE_GROUP:(g + 1) * DECODE_GROUP]
        s = jnp.concatenate(
            [_dot_nt(qc_sc[mp], jnp.concatenate([kp[pl.ds(mp, th_rows, stride=2), :] for kp in pages],
                                                axis=0).astype(BF16)) for mp in range(2)], axis=0)
        return s + head_mask + jnp.where(st == last, bd_ref[:, g * lanes:(g + 1) * lanes], 0.0)

    n_groups_step = n_pages_step // DECODE_GROUP
    group_scores = [scores(g) for g in range(n_groups_step)]
    for g in range(n_groups_step):
        s = group_scores[g]
        v16 = jnp.concatenate([vp[...].astype(BF16) for vp in vp_refs[g * DECODE_GROUP:(g + 1) * DECODE_GROUP]],
                              axis=0)
        m_old = m_sc[...]
        m_new = jnp.maximum(m_old, jnp.max(s, axis=-1, keepdims=True))
        alpha = jnp.exp2(m_old - m_new)
        p = jnp.exp2(s - m_new)
        l_sc[...] = alpha * l_sc[...] + jnp.sum(p, axis=-1, keepdims=True)
        acc_sc[...] = alpha * acc_sc[...] + jnp.dot(p.astype(BF16), v16, preferred_element_type=F32)
        m_sc[...] = m_new

    @pl.when(st == last)
    def _():
        lam = _diff_lambda(lam_ref, lam_init)
        on = acc_sc[...] * (1.0 / l_sc[...])
        for h in range(n_heads):
            oh = (on[h * dec_seq:(h + 1) * dec_seq]
                  - lam * on[map_rows + h * dec_seq:map_rows + (h + 1) * dec_seq])
            o_ref[:, h * vd:(h + 1) * vd] = _sub_layer_norm(oh, g_ref, lam_init)


def _decode_attn(q32, kv32, cache_k, cache_v, page_table, bias_past, bias_new, lam_params, subln, layer,
                 lam_init, *, new_row0, n_heads, head_dim):
    n_rows, d = q32.shape
    dec_batch, n_pages = page_table.shape
    dec_seq = n_rows // dec_batch
    n_attn, n_pool, page = cache_k.shape[:3]
    ck = cache_k.reshape(n_attn, n_pool, page * 2 * n_heads, head_dim)
    cv = cache_v.reshape(n_attn, n_pool, page * n_heads, 2 * head_dim)
    nps = DECODE_PAGES
    assert n_pages % nps == 0 and nps % DECODE_GROUP == 0 and new_row0 % dec_seq == 0
    n_steps = n_pages // nps
    vd = 2 * head_dim
    rows = 2 * n_heads * dec_seq
    new_blk = new_row0 // dec_seq

    def page_spec(t, arr):
        return pl.BlockSpec((None, None) + arr.shape[2:],
                            lambda bd, st, pt: (layer, pt[bd * n_pages + st * nps + t], 0, 0))

    grid_spec = pltpu.PrefetchScalarGridSpec(
        num_scalar_prefetch=1,
        grid=(dec_batch, n_steps),
        in_specs=[pl.BlockSpec((dec_seq, d), lambda bd, st, pt: (bd, 0)),
                  pl.BlockSpec((dec_seq, d), lambda bd, st, pt: (new_blk + bd, 0)),
                  pl.BlockSpec((dec_seq, d), lambda bd, st, pt: (new_blk + bd, 1))]
                 + [page_spec(t, ck) for t in range(nps)] + [page_spec(t, cv) for t in range(nps)]
                 + [pl.BlockSpec(bias_past.shape, lambda bd, st, pt: (0, 0)),
                    pl.BlockSpec(bias_new.shape, lambda bd, st, pt: (0, 0)),
                    pl.BlockSpec((None, 4, head_dim), lambda bd, st, pt: (layer, 0, 0)),
                    pl.BlockSpec((None, 1, vd), lambda bd, st, pt: (layer, 0, 0))],
        out_specs=pl.BlockSpec((dec_seq, d), lambda bd, st, pt: (bd, 0)),
        scratch_shapes=[pltpu.VMEM((2, rows // 2, head_dim), BF16), pltpu.VMEM((rows, V7X_LANES), F32),
                        pltpu.VMEM((rows, 1), F32), pltpu.VMEM((rows, 1), F32),
                        pltpu.VMEM((rows, vd), F32)],
    )
    kern = functools.partial(_decode_kernel, n_heads=n_heads, head_dim=head_dim, n_pages_step=nps,
                             lam_init=lam_init)
    return pl.pallas_call(
        kern,
        grid_spec=grid_spec,
        out_shape=jax.ShapeDtypeStruct((n_rows, d), F32),
        compiler_params=_cparams(("arbitrary", "arbitrary")),
        name="decode_attn",
    )(page_table.reshape(-1), q32, kv32, kv32, *([ck] * nps), *([cv] * nps),
      bias_past, bias_new, lam_params, subln)


def kernel(x_prompt, x_sample, cache_k, cache_v, state_conv, page_table, meta_tokens, rel_bias,
           attn_w_qkv, attn_lambda, attn_subln, attn_w_o, conv_w_in, conv_w, conv_w_out,
           mlp_w_up, mlp_w_down, ln_mix_g, ln_mix_b, ln_mlp_g, ln_mlp_b):
    n_batch, seq, d = x_prompt.shape
    dec_batch, dec_seq, _ = x_sample.shape
    n_meta = meta_tokens.shape[0]
    n_buckets, n_heads = rel_bias.shape
    head_dim = cache_k.shape[-1]
    vd = cache_v.shape[-1]
    page = cache_k.shape[2]
    depth = mlp_w_up.shape[0]
    assert d == n_heads * vd and vd == 2 * head_dim
    alpha = (2 * depth) ** 0.25

    mp = n_batch * seq
    n_meta_rows = n_batch * n_meta
    n_sample_rows = dec_batch * dec_seq
    assert n_meta_rows + n_sample_rows <= SMALL_ROWS and seq % FLASH_TQ == 0
    assert n_meta % V7X_SUBLANES == 0 and dec_seq % V7X_SUBLANES == 0
    meta0, samp0 = mp, mp + n_meta_rows
    mt = mp + SMALL_ROWS

    x = jnp.concatenate(
        [x_prompt.reshape(mp, d)] + [meta_tokens.astype(F32)] * n_batch
        + [x_sample.reshape(n_sample_rows, d), jnp.zeros((mt - samp0 - n_sample_rows, d), F32)], axis=0)

    far = _far_distance(n_buckets)
    tq, ck, nps = FLASH_TQ, FLASH_CK, DECODE_PAGES
    ratio = tq // ck
    assert ck + 1 >= far and n_meta + tq - (n_meta - 1) >= far and (nps - 1) * page + 1 >= far
    qq = np.arange(tq)[None, :]
    near_idx = np.stack([_bucket_tile(qq - (t - 1) * ck - np.arange(ck)[:, None], n_buckets)
                         for t in range(ratio + 1)])
    meta_idx = _bucket_tile(n_meta + qq - np.arange(n_meta)[:, None], n_buckets)
    mm_idx = _bucket_tile(np.arange(n_meta)[:, None] - np.arange(n_meta)[None, :], n_buckets)
    step_of_row = (np.arange(2 * dec_seq) % dec_seq)[:, None]
    past_token = (np.arange(nps * page * n_heads) // n_heads)[None, :]
    past_idx = _bucket_tile(nps * page + step_of_row - past_token, n_buckets)
    new_cols = np.arange(2 * dec_seq)[None, :]
    new_idx = np.where(new_cols < dec_seq, _bucket_tile(step_of_row - new_cols, n_buckets), -1).astype(np.int32)

    bias_near = _bias_expand(near_idx.reshape((ratio + 1) * ck, tq), rel_bias).reshape(
        n_heads, ratio + 1, ck, tq)
    bias_meta = _bias_expand(meta_idx, rel_bias)
    bias_mm = _bias_expand(mm_idx, rel_bias)

    def decode_rows(tile):
        lanes = tile.shape[-1]
        return tile.reshape(n_heads, 2, dec_seq, lanes).transpose(1, 0, 2, 3).reshape(2 * n_heads * dec_seq, lanes)

    bias_past = decode_rows(_bias_expand(past_idx, rel_bias))
    bias_new = decode_rows(_bias_expand(new_idx, rel_bias))

    def conv_history(state):
        h1 = jnp.zeros((dec_batch, dec_seq, d), F32).at[:, 0].set(state[:, 1])
        h2 = jnp.zeros((dec_batch, dec_seq, d), F32).at[:, 0].set(state[:, 0]).at[:, 1].set(state[:, 1])
        pad_lo = jnp.zeros((n_meta_rows, d), F32)
        pad_hi = jnp.zeros((SMALL_ROWS - n_meta_rows - n_sample_rows, d), F32)
        cat = lambda h: jnp.concatenate([pad_lo, h.reshape(n_sample_rows, d), pad_hi], axis=0)
        return cat(h1), cat(h2)

    subln3 = attn_subln.reshape(attn_subln.shape[0], 1, vd)
    kv_layers, c_p, c_s = [], [], []
    for i in range(depth):
        j = i // N_MIXERS
        g_mix, b_mix = ln_mix_g[i][None], ln_mix_b[i][None]
        if i % N_MIXERS == 0:
            lam0 = _lambda_init(i)
            qkv16, kv32 = _qkv_proj(x, attn_w_qkv, j, head_dim ** -0.5 * LOG2_E)
            o16, o_meta = _flash_prompt(qkv16, bias_near, bias_meta, bias_mm, attn_lambda, subln3, j, lam0,
                                        n_batch=n_batch, seq=seq, n_heads=n_heads, head_dim=head_dim,
                                        n_meta=n_meta)
            q32 = qkv16[samp0:samp0 + n_sample_rows, :d].astype(F32)
            o_samp = _decode_attn(q32, kv32, cache_k, cache_v, page_table, bias_past, bias_new, attn_lambda,
                                  subln3, j, lam0, new_row0=samp0, n_heads=n_heads, head_dim=head_dim)
            o_small = jnp.concatenate(
                [o_meta, o_samp, jnp.zeros((mt - samp0 - n_sample_rows, d), F32)], axis=0).astype(BF16)
            x = _outproj_ln(o16, o_small, attn_w_o, j, x, g_mix, b_mix, alpha)
            kv_layers.append(kv32)
        else:
            bg, u = _conv_in(x, conv_w_in, j)
            hist1, hist2 = conv_history(state_conv[j])
            z16 = _conv_gate(u, bg, hist1, hist2, conv_w, j, seq=seq, n_batch=n_batch, n_meta=n_meta,
                             dec_seq=dec_seq, n_sample_rows=n_sample_rows)
            x = _outproj_ln(z16, None, conv_w_out, j, x, g_mix, b_mix, alpha)
            c_p.append(jnp.stack([u[(b + 1) * seq - 2:(b + 1) * seq] for b in range(n_batch)]))
            c_s.append(u[samp0:samp0 + n_sample_rows].reshape(dec_batch, dec_seq, d)[:, dec_seq - 2:])
        last = i == depth - 1
        x = _mlp_ln(x, mlp_w_up, mlp_w_down, i, ln_mlp_g[i][None], ln_mlp_b[i][None], alpha,
                    n_main=mp if last else None)

    y_main, y_small = x
    y_prompt = y_main.reshape(n_batch, seq, d)
    y_sample = y_small[n_meta_rows:n_meta_rows + n_sample_rows].reshape(dec_batch, dec_seq, d)

    def new_cache(col0, tail):
        cols = slice(col0, col0 + d)
        prompt = jnp.concatenate(
            [jnp.concatenate(
                [jnp.concatenate([kv[meta0 + b * n_meta:meta0 + (b + 1) * n_meta, cols].reshape((1, 1, n_meta) + tail),
                                  kv[b * seq:(b + 1) * seq, cols].reshape((1, 1, seq) + tail)], axis=2)
                 for b in range(n_batch)], axis=1)
             for kv in kv_layers], axis=0)
        sample = jnp.stack([kv[samp0:samp0 + n_sample_rows, cols].reshape((dec_batch, dec_seq) + tail)
                            for kv in kv_layers])
        return prompt, sample

    k_p, k_s = new_cache(0, (n_heads, 2, head_dim))
    v_p, v_s = new_cache(d, (n_heads, vd))
    return (y_prompt, y_sample, k_p, v_p, jnp.stack(c_p), k_s, v_s, jnp.stack(c_s))
```

```python
import functools
import math

import numpy as np
import jax
import jax.numpy as jnp
from jax import lax
from jax.experimental import pallas as pl
from jax.experimental.pallas import tpu as pltpu

F32 = jnp.float32
BF16 = jnp.bfloat16

MAX_DISTANCE = 128
LN_EPS = 1e-5
SUBLN_EPS = 1e-5
N_MIXERS = 2
MASK_VALUE = -1e30
LOG2_E = math.log2(math.e)

V7X_LANES = 128
V7X_SUBLANES = 8
V7X_VMEM_LIMIT_BYTES = 60 * 1024 * 1024

SMALL_ROWS = 128
FLASH_TQ = 512
FLASH_CK = 512
FLASH_HEADS = 2
FLASH_SUM_ROWS = 16
DECODE_PAGES = 8
DECODE_GROUP = 2


def _lambda_init(layer):
    return 0.8 - 0.6 * math.exp(-0.3 * layer)


def _pick_tile(total, target, multiple):
    best = None
    for t in range(multiple, min(total, target) + 1, multiple):
        if total % t == 0:
            best = t
    assert best is not None, (total, target, multiple)
    return best


def _cparams(sem):
    return pltpu.CompilerParams(dimension_semantics=sem, vmem_limit_bytes=V7X_VMEM_LIMIT_BYTES)


def _bucket_of_distance(n, n_buckets):
    n = np.asarray(n, np.int64)
    max_exact = n_buckets // 2
    nf = np.maximum(n, 1).astype(np.float64)
    large = max_exact + (np.log(nf / max_exact) / math.log(MAX_DISTANCE / max_exact)
                         * (n_buckets - max_exact)).astype(np.int64)
    large = np.minimum(large, n_buckets - 1)
    return np.where(n < max_exact, n, large).astype(np.int32)


def _far_distance(n_buckets):
    tab = _bucket_of_distance(np.arange(4 * MAX_DISTANCE), n_buckets)
    not_last = np.nonzero(tab != n_buckets - 1)[0]
    return int(not_last[-1]) + 1


def _bucket_tile(dist, n_buckets):
    dist = np.asarray(dist, np.int64)
    return np.where(dist < 0, -1, _bucket_of_distance(np.maximum(dist, 0), n_buckets)).astype(np.int32)


def _bias_expand_kernel(rb_ref, idx_ref, o_ref, *, n_buckets):
    h = pl.program_id(0)
    idx = idx_ref[...]
    far = rb_ref[n_buckets - 1, h]
    out = jnp.zeros(idx.shape, F32)
    for b in range(n_buckets - 1):
        out = jnp.where(idx == b, (rb_ref[b, h] - far) * LOG2_E, out)
    o_ref[...] = jnp.where(idx < 0, MASK_VALUE, out)


def _bias_expand(idx_np, rel_bias):
    rows, cols = idx_np.shape
    n_buckets, n_heads = rel_bias.shape
    return pl.pallas_call(
        functools.partial(_bias_expand_kernel, n_buckets=n_buckets),
        grid=(n_heads,),
        in_specs=[pl.BlockSpec(memory_space=pltpu.SMEM),
                  pl.BlockSpec((rows, cols), lambda h: (0, 0))],
        out_specs=pl.BlockSpec((None, rows, cols), lambda h: (h, 0, 0)),
        out_shape=jax.ShapeDtypeStruct((n_heads, rows, cols), F32),
        compiler_params=_cparams(("arbitrary",)),
        name="bias_expand",
    )(rel_bias, jnp.asarray(idx_np))


def _qkv_kernel(x_ref, w_ref, o16_ref, kv32_ref, xb_ref, *, n_q_tiles, q_scale):
    j = pl.program_id(1)

    @pl.when(j == 0)
    def _():
        xb_ref[...] = x_ref[...].astype(BF16)

    acc = jnp.dot(xb_ref[...], w_ref[...].astype(BF16), preferred_element_type=F32)
    scale = jnp.where(j < n_q_tiles, q_scale, 1.0).astype(F32)
    o16_ref[...] = (acc * scale).astype(BF16)

    @pl.when(j >= n_q_tiles)
    def _():
        kv32_ref[...] = acc


def _qkv_proj(x, w_qkv, layer, q_scale):
    m, d = x.shape
    tm = _pick_tile(m, 1040, 16)
    tn = 512
    n_q_tiles = d // tn
    return pl.pallas_call(
        functools.partial(_qkv_kernel, n_q_tiles=n_q_tiles, q_scale=q_scale),
        grid=(m // tm, 3 * d // tn),
        in_specs=[pl.BlockSpec((tm, d), lambda i, j: (i, 0)),
                  pl.BlockSpec((None, d, tn), lambda i, j: (layer, 0, j))],
        out_specs=[pl.BlockSpec((tm, tn), lambda i, j: (i, j)),
                   pl.BlockSpec((tm, tn), lambda i, j: (i, jnp.maximum(j - n_q_tiles, 0)))],
        out_shape=[jax.ShapeDtypeStruct((m, 3 * d), BF16),
                   jax.ShapeDtypeStruct((m, 2 * d), F32)],
        scratch_shapes=[pltpu.VMEM((tm, d), BF16)],
        compiler_params=_cparams(("arbitrary", "arbitrary")),
        name="qkv_proj",
    )(x, w_qkv)


def _convin_kernel(x_ref, wb_ref, wc_ref, wh_ref, bg_ref, u_ref, xb_ref):
    @pl.when(pl.program_id(1) == 0)
    def _():
        xb_ref[...] = x_ref[...].astype(BF16)

    xb = xb_ref[...]
    bg_ref[...] = jnp.dot(xb, wb_ref[...].astype(BF16), preferred_element_type=F32)
    cg = jnp.dot(xb, wc_ref[...].astype(BF16), preferred_element_type=F32)
    hh = jnp.dot(xb, wh_ref[...].astype(BF16), preferred_element_type=F32)
    u_ref[...] = cg * hh


def _conv_in(x, w_in, layer):
    m, d = x.shape
    tm = _pick_tile(m, 1040, 16)
    tn = 256
    nt = d // tn
    wspec = lambda part: pl.BlockSpec((None, d, tn), lambda i, j: (layer, 0, part * nt + j))
    return pl.pallas_call(
        _convin_kernel,
        grid=(m // tm, nt),
        in_specs=[pl.BlockSpec((tm, d), lambda i, j: (i, 0)), wspec(0), wspec(1), wspec(2)],
        out_specs=[pl.BlockSpec((tm, tn), lambda i, j: (i, j)),
                   pl.BlockSpec((tm, tn), lambda i, j: (i, j))],
        out_shape=[jax.ShapeDtypeStruct((m, d), F32), jax.ShapeDtypeStruct((m, d), F32)],
        scratch_shapes=[pltpu.VMEM((tm, d), BF16)],
        compiler_params=_cparams(("arbitrary", "arbitrary")),
        name="conv_in",
    )(x, w_in, w_in, w_in)


def _convgate_kernel(u_ref, bg_ref, halo_ref, meta_ref, h1_ref, h2_ref, w_ref, z_ref, *,
                     seq_blocks, n_prompt_blocks, n_batch, n_meta, dec_seq, n_sample_rows):
    i = pl.program_id(0)
    u = u_ref[...]
    rows = u.shape[0]
    r = lax.broadcasted_iota(jnp.int32, (rows, 1), 0)

    halo = halo_ref[...]
    prev2, prev1 = halo[V7X_SUBLANES - 2:V7X_SUBLANES - 1], halo[V7X_SUBLANES - 1:V7X_SUBLANES]
    meta = meta_ref[...]
    for b in range(n_batch):
        starts_batch = i == b * seq_blocks
        prev2 = jnp.where(starts_batch, meta[(b + 1) * n_meta - 2:(b + 1) * n_meta - 1], prev2)
        prev1 = jnp.where(starts_batch, meta[(b + 1) * n_meta - 1:(b + 1) * n_meta], prev1)

    p1 = jnp.where(r == 0, prev1, pltpu.roll(u, 1, axis=0))
    p2 = jnp.where(r == 0, prev2, jnp.where(r == 1, prev1, pltpu.roll(u, 2, axis=0)))

    small = i == n_prompt_blocks
    n_meta_rows = n_batch * n_meta
    in_meta = r < n_meta_rows
    in_sample = jnp.logical_and(r >= n_meta_rows, r < n_meta_rows + n_sample_rows)
    pos = jnp.where(in_meta, r % n_meta, (r - n_meta_rows) % dec_seq)
    restart = jnp.logical_or(in_meta, in_sample)
    m1 = jnp.logical_and(small, jnp.logical_and(restart, pos == 0))
    m2 = jnp.logical_and(small, jnp.logical_and(restart, pos <= 1))
    p1 = jnp.where(m1, h1_ref[...], p1)
    p2 = jnp.where(m2, h2_ref[...], p2)

    w = w_ref[...]
    conv = w[0:1] * p2 + w[1:2] * p1 + w[2:3] * u
    z_ref[...] = (bg_ref[...] * conv).astype(BF16)


def _conv_gate(u, bg, hist1, hist2, w_conv, layer, *, seq, n_batch, n_meta, dec_seq, n_sample_rows):
    m, d = u.shape
    rows = SMALL_ROWS
    n_prompt_blocks = n_batch * seq // rows
    assert m == (n_prompt_blocks + 1) * rows and seq % rows == 0
    halo_per_block = rows // V7X_SUBLANES
    kern = functools.partial(_convgate_kernel, seq_blocks=seq // rows, n_prompt_blocks=n_prompt_blocks,
                             n_batch=n_batch, n_meta=n_meta, dec_seq=dec_seq, n_sample_rows=n_sample_rows)
    return pl.pallas_call(
        kern,
        grid=(m // rows,),
        in_specs=[pl.BlockSpec((rows, d), lambda i: (i, 0)),
                  pl.BlockSpec((rows, d), lambda i: (i, 0)),
                  pl.BlockSpec((V7X_SUBLANES, d), lambda i: (jnp.maximum(i * halo_per_block - 1, 0), 0)),
                  pl.BlockSpec((rows, d), lambda i: (n_prompt_blocks, 0)),
                  pl.BlockSpec((rows, d), lambda i: (0, 0)),
                  pl.BlockSpec((rows, d), lambda i: (0, 0)),
                  pl.BlockSpec((None, 3, d), lambda i: (layer, 0, 0))],
        out_specs=pl.BlockSpec((rows, d), lambda i: (i, 0)),
        out_shape=jax.ShapeDtypeStruct((m, d), BF16),
        compiler_params=_cparams(("arbitrary",)),
        name="conv_gate",
    )(u, bg, u, u, hist1, hist2, w_conv)


def _residual_layer_norm(x_ref, o_ref, g_ref, b_ref, alpha, rows_chunk):
    g = g_ref[...]
    b = b_ref[...]

    def body(c, carry):
        rs = pl.ds(pl.multiple_of(c * rows_chunk, rows_chunk), rows_chunk)
        y = alpha * x_ref[rs, :] + o_ref[rs, :]
        mu = jnp.mean(y, axis=-1, keepdims=True)
        yc = y - mu
        var = jnp.mean(yc * yc, axis=-1, keepdims=True)
        o_ref[rs, :] = yc * lax.rsqrt(var + LN_EPS) * g + b
        return carry

    lax.fori_loop(0, x_ref.shape[0] // rows_chunk, body, 0)


def _outproj_kernel(a_ref, *rest, alpha, rows_chunk, n_main_last):
    small_ref = rest[0] if n_main_last is not None else None
    w_ref, x_ref, g_ref, b_ref, o_ref = rest[-5:]
    i = pl.program_id(0)
    k = pl.program_id(1)
    last_tile = pl.num_programs(0) - 1

    @pl.when(k == 0)
    def _():
        o_ref[...] = jnp.zeros_like(o_ref)

    w = w_ref[...].astype(BF16)
    if small_ref is None:
        o_ref[...] += jnp.dot(a_ref[...], w, preferred_element_type=F32)
    else:
        @pl.when(i < last_tile)
        def _():
            o_ref[...] += jnp.dot(a_ref[...], w, preferred_element_type=F32)

        @pl.when(i == last_tile)
        def _():
            a = jnp.concatenate([a_ref[:n_main_last], small_ref[...]], axis=0)
            o_ref[...] += jnp.dot(a, w, preferred_element_type=F32)

    @pl.when(k == pl.num_programs(1) - 1)
    def _():
        _residual_layer_norm(x_ref, o_ref, g_ref, b_ref, alpha, rows_chunk)


def _outproj_ln(a16, a16_small, w, layer, x, g, b, alpha):
    m, d = x.shape
    tm = _pick_tile(m, 1040, 16)
    tk = 512
    rows_chunk = _pick_tile(tm, 208, V7X_SUBLANES)
    if a16_small is None:
        assert a16.shape[0] == m
        n_main_last, small_args, small_specs = None, [], []
    else:
        n_small = a16_small.shape[0]
        assert a16.shape[0] + n_small == m and n_small <= tm
        n_main_last = tm - n_small
        small_args = [a16_small]
        small_specs = [pl.BlockSpec((n_small, tk), lambda i, k: (0, k))]
    return pl.pallas_call(
        functools.partial(_outproj_kernel, alpha=alpha, rows_chunk=rows_chunk, n_main_last=n_main_last),
        grid=(m // tm, d // tk),
        in_specs=[pl.BlockSpec((tm, tk), lambda i, k: (i, k))] + small_specs + [
                  pl.BlockSpec((None, tk, d), lambda i, k: (layer, k, 0)),
                  pl.BlockSpec((tm, d), lambda i, k: (i, 0)),
                  pl.BlockSpec((1, d), lambda i, k: (0, 0)),
                  pl.BlockSpec((1, d), lambda i, k: (0, 0))],
        out_specs=pl.BlockSpec((tm, d), lambda i, k: (i, 0)),
        out_shape=jax.ShapeDtypeStruct((m, d), F32),
        compiler_params=_cparams(("arbitrary", "arbitrary")),
        name="outproj_ln",
    )(a16, *small_args, w, x, g, b)


def _mlp_kernel(x_ref, wu_ref, wd_ref, g_ref, b_ref, o_ref, *rest, alpha, rows_chunk, n_main_last):
    small_ref = rest[0] if n_main_last is not None else None
    xb_ref = rest[-1]
    f = pl.program_id(1)

    @pl.when(f == 0)
    def _():
        xb_ref[...] = x_ref[...].astype(BF16)
        o_ref[...] = jnp.zeros_like(o_ref)

    h = jnp.dot(xb_ref[...], wu_ref[...].astype(BF16), preferred_element_type=F32)
    h = jnp.square(jnp.maximum(h, 0.0)).astype(BF16)
    o_ref[...] += jnp.dot(h, wd_ref[...].astype(BF16), preferred_element_type=F32)

    @pl.when(f == pl.num_programs(1) - 1)
    def _():
        _residual_layer_norm(x_ref, o_ref, g_ref, b_ref, alpha, rows_chunk)
        if small_ref is not None:
            @pl.when(pl.program_id(0) == pl.num_programs(0) - 1)
            def _():
                small_ref[...] = o_ref[n_main_last:, :]


def _mlp_ln(x, w_up, w_down, layer, g, b, alpha, n_main=None):
    m, d = x.shape
    d_ff = w_up.shape[-1]
    tm = _pick_tile(m, 832, 16)
    tf = 512
    rows_chunk = _pick_tile(tm, 208, V7X_SUBLANES)
    out_specs = pl.BlockSpec((tm, d), lambda i, f: (i, 0))
    out_shape = jax.ShapeDtypeStruct((m, d), F32)
    n_main_last = None
    if n_main is not None:
        n_small = m - n_main
        assert 0 < n_small <= tm
        n_main_last = tm - n_small
        out_specs = [out_specs, pl.BlockSpec((n_small, d), lambda i, f: (0, 0))]
        out_shape = [jax.ShapeDtypeStruct((n_main, d), F32), jax.ShapeDtypeStruct((n_small, d), F32)]
    return pl.pallas_call(
        functools.partial(_mlp_kernel, alpha=alpha, rows_chunk=rows_chunk, n_main_last=n_main_last),
        grid=(m // tm, d_ff // tf),
        in_specs=[pl.BlockSpec((tm, d), lambda i, f: (i, 0)),
                  pl.BlockSpec((None, d, tf), lambda i, f: (layer, 0, f)),
                  pl.BlockSpec((None, tf, d), lambda i, f: (layer, f, 0)),
                  pl.BlockSpec((1, d), lambda i, f: (0, 0)),
                  pl.BlockSpec((1, d), lambda i, f: (0, 0))],
        out_specs=out_specs,
        out_shape=out_shape,
        scratch_shapes=[pltpu.VMEM((tm, d), BF16)],
        compiler_params=_cparams(("arbitrary", "arbitrary")),
        name="mlp_ln",
    )(x, w_up, w_down, g, b)


def _diff_lambda(lam_ref, lam_init):
    lp = lam_ref[...]
    a = jnp.sum(lp[0:1] * lp[1:2], axis=-1, keepdims=True)
    b = jnp.sum(lp[2:3] * lp[3:4], axis=-1, keepdims=True)
    return jnp.exp(a) - jnp.exp(b) + lam_init


def _sub_layer_norm(o, g_ref, lam_init):
    return o * lax.rsqrt(jnp.mean(o * o, axis=-1, keepdims=True) + SUBLN_EPS) * (g_ref[...] * (1.0 - lam_init))


def _dot_nt(a, b):
    return lax.dot_general(a, b, (((1,), (1,)), ((), ())), preferred_element_type=F32)


def _flash_kernel(q_ref, k_ref, v_ref, km_ref, vm_ref, qm_ref, bn_ref, bm_ref, bmm_ref, lam_ref, g_ref,
                  o_ref, om_ref, vt_sc, s0_sc, s1_sc, m_sc, acc_sc, *, head_dim, lam_init):
    qi = pl.program_id(2)
    tq = q_ref.shape[0]
    n_local, _, ck, _ = bn_ref.shape
    ratio = tq // ck
    n_chunks = k_ref.shape[0] // ck
    n_meta = km_ref.shape[0]
    vd = 2 * head_dim
    heads = range(n_local)
    lam = _diff_lambda(lam_ref, lam_init)
    cols = lambda hh: slice(hh * vd, (hh + 1) * vd)

    @pl.when(qi == 0)
    def _():
        extra = vt_sc.shape[2] - vd
        ones_rows = (lax.broadcasted_iota(jnp.int32, (extra, ck), 0) == 0).astype(BF16)
        for hh in heads:
            def body(c, carry):
                rs = pl.ds(pl.multiple_of(c * ck, ck), ck)
                vt_sc[hh, c, :vd] = v_ref[rs, cols(hh)].astype(F32).T.astype(BF16)
                vt_sc[hh, c, vd:] = ones_rows
                return carry

            lax.fori_loop(0, n_chunks, body, 0)
            vm_pad = jnp.concatenate([vm_ref[:, cols(hh)].astype(F32), jnp.zeros((ck - n_meta, vd), F32)], axis=0)
            vt_sc[hh, n_chunks, :vd] = vm_pad.T.astype(BF16)
            vt_sc[hh, n_chunks, vd:] = ones_rows

    def scores_t(hh, keys, bias_t):
        parts = [_dot_nt(keys[:, m * head_dim:(m + 1) * head_dim],
                         q_ref[:, hh * vd + m * head_dim:hh * vd + (m + 1) * head_dim]) for m in range(2)]
        if bias_t is not None:
            parts = [p + bias_t for p in parts]
        return jnp.concatenate(parts, axis=1)

    for hh in heads:
        s = scores_t(hh, km_ref[:, cols(hh)], jnp.where(qi == 0, bm_ref[hh], 0.0))
        m0 = jnp.max(s, axis=0, keepdims=True)
        p = jnp.exp2(s - m0)
        m_sc[hh] = m0
        p_pad = jnp.concatenate([p, jnp.zeros((V7X_LANES - n_meta, 2 * tq), F32)], axis=0).astype(BF16)
        acc_sc[hh] = jnp.dot(vt_sc[hh, n_chunks][:, :V7X_LANES], p_pad, preferred_element_type=F32)

    assert ratio == 1
    s_bufs = (s0_sc, s1_sc)

    def scores_into(c, tile, s_buf):
        for hh in heads:
            kb = k_ref[pl.ds(pl.multiple_of(c * ck, ck), ck), cols(hh)]
            s_buf[hh] = scores_t(hh, kb, None if tile is None else bn_ref[hh, tile])

    def update_from(c, s_buf):
        for hh in heads:
            s = s_buf[hh]
            m_old = m_sc[hh]
            m_new = jnp.maximum(m_old, jnp.max(s, axis=0, keepdims=True))
            alpha = jnp.exp2(m_old - m_new)
            p = jnp.exp2(s - m_new)
            acc_sc[hh] = alpha * acc_sc[hh] + jnp.dot(vt_sc[hh, c], p.astype(BF16), preferred_element_type=F32)
            m_sc[hh] = m_new

    def by_parity(c, fn):
        for par in range(2):
            @pl.when(jnp.bitwise_and(c, 1) == par)
            def _():
                fn(s_bufs[par], s_bufs[1 - par])

    def advance(c, next_tile):
        def fn(cur, nxt):
            scores_into(c + 1, next_tile, nxt)
            update_from(c, cur)
        by_parity(c, fn)

    @pl.when(qi == 0)
    def _():
        scores_into(0, 1, s_bufs[0])

    @pl.when(qi == 1)
    def _():
        scores_into(0, 0, s_bufs[0])

    @pl.when(qi >= 2)
    def _():
        scores_into(0, None, s_bufs[0])

        def far_body(c, carry):
            advance(c, None)
            return carry

        lax.fori_loop(0, qi - 2, far_body, 0)
        advance(qi - 2, 0)

    @pl.when(qi >= 1)
    def _():
        advance(qi - 1, 1)

    by_parity(qi, lambda cur, nxt: update_from(qi, cur))

    for hh in heads:
        acc = acc_sc[hh]
        on = acc[:vd] * (1.0 / acc[vd:vd + 1])
        o_t = on[:, :tq] - lam * on[:, tq:]
        o_t = o_t * lax.rsqrt(jnp.mean(o_t * o_t, axis=0, keepdims=True) + SUBLN_EPS)
        o_ref[:, cols(hh)] = (o_t.T * (g_ref[...] * (1.0 - lam_init))).astype(BF16)

    @pl.when(qi == 0)
    def _():
        for hh in heads:
            qm = qm_ref[:, cols(hh)]
            km = km_ref[:, cols(hh)]
            sm = jnp.concatenate([_dot_nt(qm[:, m * head_dim:(m + 1) * head_dim],
                                          km[:, m * head_dim:(m + 1) * head_dim]) + bmm_ref[hh]
                                  for m in range(2)], axis=0)
            pm = jnp.exp2(sm - jnp.max(sm, axis=-1, keepdims=True))
            onm = (jnp.dot(pm.astype(BF16), vm_ref[:, cols(hh)], preferred_element_type=F32)
                   * (1.0 / jnp.sum(pm, axis=-1, keepdims=True)))
            om_ref[:, cols(hh)] = _sub_layer_norm(onm[:n_meta] - lam * onm[n_meta:], g_ref, lam_init)


def _flash_prompt(qkv16, bias_near, bias_meta, bias_mm, lam_params, subln, layer, lam_init, *,
                  n_batch, seq, n_heads, head_dim, n_meta):
    m, d3 = qkv16.shape
    d = d3 // 3
    vd = 2 * head_dim
    tq = FLASH_TQ
    hp = FLASH_HEADS
    assert n_heads % hp == 0
    ng = n_heads // hp
    nq = seq // tq
    mp = n_batch * seq
    meta_blk = mp // n_meta
    kern = functools.partial(_flash_kernel, head_dim=head_dim, lam_init=lam_init)
    return pl.pallas_call(
        kern,
        grid=(n_batch, ng, nq),
        in_specs=[pl.BlockSpec((tq, hp * vd), lambda b, h, qi: (b * nq + qi, h)),
                  pl.BlockSpec((seq, hp * vd), lambda b, h, qi: (b, ng + h)),
                  pl.BlockSpec((seq, hp * vd), lambda b, h, qi: (b, 2 * ng + h)),
                  pl.BlockSpec((n_meta, hp * vd), lambda b, h, qi: (meta_blk + b, ng + h)),
                  pl.BlockSpec((n_meta, hp * vd), lambda b, h, qi: (meta_blk + b, 2 * ng + h)),
                  pl.BlockSpec((n_meta, hp * vd), lambda b, h, qi: (meta_blk + b, h)),
                  pl.BlockSpec((hp,) + bias_near.shape[1:], lambda b, h, qi: (h, 0, 0, 0)),
                  pl.BlockSpec((hp,) + bias_meta.shape[1:], lambda b, h, qi: (h, 0, 0)),
                  pl.BlockSpec((hp,) + bias_mm.shape[1:], lambda b, h, qi: (h, 0, 0)),
                  pl.BlockSpec((None, 4, head_dim), lambda b, h, qi: (layer, 0, 0)),
                  pl.BlockSpec((None, 1, vd), lambda b, h, qi: (layer, 0, 0))],
        out_specs=[pl.BlockSpec((tq, hp * vd), lambda b, h, qi: (b * nq + qi, h)),
                   pl.BlockSpec((n_meta, hp * vd), lambda b, h, qi: (b, h))],
        out_shape=[jax.ShapeDtypeStruct((mp, d), BF16),
                   jax.ShapeDtypeStruct((n_batch * n_meta, d), F32)],
        scratch_shapes=[pltpu.VMEM((hp, seq // FLASH_CK + 1, vd + FLASH_SUM_ROWS, FLASH_CK), BF16),
                        pltpu.VMEM((hp, FLASH_CK, 2 * tq), F32), pltpu.VMEM((hp, FLASH_CK, 2 * tq), F32),
                        pltpu.VMEM((hp, 1, 2 * tq), F32),
                        pltpu.VMEM((hp, vd + FLASH_SUM_ROWS, 2 * tq), F32)],
        compiler_params=_cparams(("arbitrary", "arbitrary", "arbitrary")),
        name="flash_prompt",
    )(qkv16, qkv16, qkv16, qkv16, qkv16, qkv16, bias_near, bias_meta, bias_mm, lam_params, subln)


def _decode_kernel(pt_ref, q_ref, kn_ref, vn_ref, *rest, n_heads, head_dim, n_pages_step, lam_init):
    kp_refs = rest[:n_pages_step]
    vp_refs = rest[n_pages_step:2 * n_pages_step]
    bd_ref, bnw_ref, lam_ref, g_ref, o_ref, qc_sc, hm_sc, m_sc, l_sc, acc_sc = rest[2 * n_pages_step:]
    st = pl.program_id(1)
    last = pl.num_programs(1) - 1
    dec_seq, d = q_ref.shape
    vd = 2 * head_dim
    map_rows = n_heads * dec_seq
    n_groups = 2 * n_heads

    @pl.when(st == 0)
    def _():
        q = q_ref[...]
        for mp in range(2):
            qc_sc[mp] = jnp.concatenate(
                [q[:, (2 * h + mp) * head_dim:(2 * h + mp + 1) * head_dim] for h in range(n_heads)],
                axis=0).astype(BF16)
        r = lax.broadcasted_iota(jnp.int32, hm_sc.shape, 0)
        c = lax.broadcasted_iota(jnp.int32, hm_sc.shape, 1)
        hm_sc[...] = jnp.where((r // dec_seq) % n_heads == c % n_heads, 0.0, MASK_VALUE)

        qt = jnp.concatenate([q] * n_groups, axis=0)
        grp = lax.broadcasted_iota(jnp.int32, qt.shape, 0) // dec_seq
        col = lax.broadcasted_iota(jnp.int32, qt.shape, 1) // head_dim
        qb = jnp.where(col == (grp % n_heads) * 2 + grp // n_heads, qt, 0.0).astype(BF16)
        pad = jnp.zeros((dec_seq, d), F32)
        kn = jnp.concatenate([kn_ref[...], pad], axis=0).astype(BF16)
        vn = jnp.concatenate([vn_ref[...], pad], axis=0).astype(BF16)
        s = _dot_nt(qb, kn) + bnw_ref[...]
        m0 = jnp.max(s, axis=-1, keepdims=True)
        p = jnp.exp2(s - m0)
        m_sc[...] = m0
        l_sc[...] = jnp.sum(p, axis=-1, keepdims=True)
        full = jnp.dot(p.astype(BF16), vn, preferred_element_type=F32)
        acc_sc[...] = jnp.concatenate(
            [full[g * dec_seq:(g + 1) * dec_seq, (g % n_heads) * vd:(g % n_heads + 1) * vd]
             for g in range(n_groups)], axis=0)

    th_rows = vp_refs[0].shape[0]
    lanes = DECODE_GROUP * th_rows
    head_mask = jnp.tile(hm_sc[...], (1, lanes // hm_sc.shape[1]))

    def scores(g):
        pages = kp_refs[g * DECODE_GROUP:(g + 1) * DECODE_GROUP]
        s = jnp.concatenate(
            [_dot_nt(qc_sc[mp], jnp.concatenate([kp[pl.ds(mp, th_rows, stride=2), :] for kp in pages],
                                                axis=0).astype(BF16)) for mp in range(2)], axis=0)
        return s + head_mask + jnp.where(st == last, bd_ref[:, g * lanes:(g + 1) * lanes], 0.0)

    n_groups_step = n_pages_step // DECODE_GROUP
    group_scores = [scores(g) for g in range(n_groups_step)]
    for g in range(n_groups_step):
        s = group_scores[g]
        v16 = jnp.concatenate([vp[...].astype(BF16) for vp in vp_refs[g * DECODE_GROUP:(g + 1) * DECODE_GROUP]],
                              axis=0)
        m_old = m_sc[...]
        m_new = jnp.maximum(m_old, jnp.max(s, axis=-1, keepdims=True))
        alpha = jnp.exp2(m_old - m_new)
        p = jnp.exp2(s - m_new)
        l_sc[...] = alpha * l_sc[...] + jnp.sum(p, axis=-1, keepdims=True)
        acc_sc[...] = alpha * acc_sc[...] + jnp.dot(p.astype(BF16), v16, preferred_element_type=F32)
        m_sc[...] = m_new

    @pl.when(st == last)
    def _():
        lam = _diff_lambda(lam_ref, lam_init)
        on = acc_sc[...] * (1.0 / l_sc[...])
        for h in range(n_heads):
            oh = (on[h * dec_seq:(h + 1) * dec_seq]
                  - lam * on[map_rows + h * dec_seq:map_rows + (h + 1) * dec_seq])
            o_ref[:, h * vd:(h + 1) * vd] = _sub_layer_norm(oh, g_ref, lam_init)


def _decode_attn(q32, kv32, cache_k, cache_v, page_table, bias_past, bias_new, lam_params, subln, layer,
                 lam_init, *, new_row0, n_heads, head_dim):
    n_rows, d = q32.shape
    dec_batch, n_pages = page_table.shape
    dec_seq = n_rows // dec_batch
    n_attn, n_pool, page = cache_k.shape[:3]
    ck = cache_k.reshape(n_attn, n_pool, page * 2 * n_heads, head_dim)
    cv = cache_v.reshape(n_attn, n_pool, page * n_heads, 2 * head_dim)
    nps = DECODE_PAGES
    assert n_pages % nps == 0 and nps % DECODE_GROUP == 0 and new_row0 % dec_seq == 0
    n_steps = n_pages // nps
    vd = 2 * head_dim
    rows = 2 * n_heads * dec_seq
    new_blk = new_row0 // dec_seq

    def page_spec(t, arr):
        return pl.BlockSpec((None, None) + arr.shape[2:],
                            lambda bd, st, pt: (layer, pt[bd * n_pages + st * nps + t], 0, 0))

    grid_spec = pltpu.PrefetchScalarGridSpec(
        num_scalar_prefetch=1,
        grid=(dec_batch, n_steps),
        in_specs=[pl.BlockSpec((dec_seq, d), lambda bd, st, pt: (bd, 0)),
                  pl.BlockSpec((dec_seq, d), lambda bd, st, pt: (new_blk + bd, 0)),
                  pl.BlockSpec((dec_seq, d), lambda bd, st, pt: (new_blk + bd, 1))]
                 + [page_spec(t, ck) for t in range(nps)] + [page_spec(t, cv) for t in range(nps)]
                 + [pl.BlockSpec(bias_past.shape, lambda bd, st, pt: (0, 0)),
                    pl.BlockSpec(bias_new.shape, lambda bd, st, pt: (0, 0)),
                    pl.BlockSpec((None, 4, head_dim), lambda bd, st, pt: (layer, 0, 0)),
                    pl.BlockSpec((None, 1, vd), lambda bd, st, pt: (layer, 0, 0))],
        out_specs=pl.BlockSpec((dec_seq, d), lambda bd, st, pt: (bd, 0)),
        scratch_shapes=[pltpu.VMEM((2, rows // 2, head_dim), BF16), pltpu.VMEM((rows, V7X_LANES), F32),
                        pltpu.VMEM((rows, 1), F32), pltpu.VMEM((rows, 1), F32),
                        pltpu.VMEM((rows, vd), F32)],
    )
    kern = functools.partial(_decode_kernel, n_heads=n_heads, head_dim=head_dim, n_pages_step=nps,
                             lam_init=lam_init)
    return pl.pallas_call(
        kern,
        grid_spec=grid_spec,
        out_shape=jax.ShapeDtypeStruct((n_rows, d), F32),
        compiler_params=_cparams(("arbitrary", "arbitrary")),
        name="decode_attn",
    )(page_table.reshape(-1), q32, kv32, kv32, *([ck] * nps), *([cv] * nps),
      bias_past, bias_new, lam_params, subln)


def kernel(x_prompt, x_sample, cache_k, cache_v, state_conv, page_table, meta_tokens, rel_bias,
           attn_w_qkv, attn_lambda, attn_subln, attn_w_o, conv_w_in, conv_w, conv_w_out,
           mlp_w_up, mlp_w_down, ln_mix_g, ln_mix_b, ln_mlp_g, ln_mlp_b):
    n_batch, seq, d = x_prompt.shape
    dec_batch, dec_seq, _ = x_sample.shape
    n_meta = meta_tokens.shape[0]
    n_buckets, n_heads = rel_bias.shape
    head_dim = cache_k.shape[-1]
    vd = cache_v.shape[-1]
    page = cache_k.shape[2]
    depth = mlp_w_up.shape[0]
    assert d == n_heads * vd and vd == 2 * head_dim
    alpha = (2 * depth) ** 0.25

    mp = n_batch * seq
    n_meta_rows = n_batch * n_meta
    n_sample_rows = dec_batch * dec_seq
    assert n_meta_rows + n_sample_rows <= SMALL_ROWS and seq % FLASH_TQ == 0
    assert n_meta % V7X_SUBLANES == 0 and dec_seq % V7X_SUBLANES == 0
    meta0, samp0 = mp, mp + n_meta_rows
    mt = mp + SMALL_ROWS

    x = jnp.concatenate(
        [x_prompt.reshape(mp, d)] + [meta_tokens.astype(F32)] * n_batch
        + [x_sample.reshape(n_sample_rows, d), jnp.zeros((mt - samp0 - n_sample_rows, d), F32)], axis=0)

    far = _far_distance(n_buckets)
    tq, ck, nps = FLASH_TQ, FLASH_CK, DECODE_PAGES
    ratio = tq // ck
    assert ck + 1 >= far and n_meta + tq - (n_meta - 1) >= far and (nps - 1) * page + 1 >= far
    qq = np.arange(tq)[None, :]
    near_idx = np.stack([_bucket_tile(qq - (t - 1) * ck - np.arange(ck)[:, None], n_buckets)
                         for t in range(ratio + 1)])
    meta_idx = _bucket_tile(n_meta + qq - np.arange(n_meta)[:, None], n_buckets)
    mm_idx = _bucket_tile(np.arange(n_meta)[:, None] - np.arange(n_meta)[None, :], n_buckets)
    step_of_row = (np.arange(2 * dec_seq) % dec_seq)[:, None]
    past_token = (np.arange(nps * page * n_heads) // n_heads)[None, :]
    past_idx = _bucket_tile(nps * page + step_of_row - past_token, n_buckets)
    new_cols = np.arange(2 * dec_seq)[None, :]
    new_idx = np.where(new_cols < dec_seq, _bucket_tile(step_of_row - new_cols, n_buckets), -1).astype(np.int32)

    bias_near = _bias_expand(near_idx.reshape((ratio + 1) * ck, tq), rel_bias).reshape(
        n_heads, ratio + 1, ck, tq)
    bias_meta = _bias_expand(meta_idx, rel_bias)
    bias_mm = _bias_expand(mm_idx, rel_bias)

    def decode_rows(tile):
        lanes = tile.shape[-1]
        return tile.reshape(n_heads, 2, dec_seq, lanes).transpose(1, 0, 2, 3).reshape(2 * n_heads * dec_seq, lanes)

    bias_past = decode_rows(_bias_expand(past_idx, rel_bias))
    bias_new = decode_rows(_bias_expand(new_idx, rel_bias))

    def conv_history(state):
        h1 = jnp.zeros((dec_batch, dec_seq, d), F32).at[:, 0].set(state[:, 1])
        h2 = jnp.zeros((dec_batch, dec_seq, d), F32).at[:, 0].set(state[:, 0]).at[:, 1].set(state[:, 1])
        pad_lo = jnp.zeros((n_meta_rows, d), F32)
        pad_hi = jnp.zeros((SMALL_ROWS - n_meta_rows - n_sample_rows, d), F32)
        cat = lambda h: jnp.concatenate([pad_lo, h.reshape(n_sample_rows, d), pad_hi], axis=0)
        return cat(h1), cat(h2)

    subln3 = attn_subln.reshape(attn_subln.shape[0], 1, vd)
    kv_layers, c_p, c_s = [], [], []
    for i in range(depth):
        j = i // N_MIXERS
        g_mix, b_mix = ln_mix_g[i][None], ln_mix_b[i][None]
        if i % N_MIXERS == 0:
            lam0 = _lambda_init(i)
            qkv16, kv32 = _qkv_proj(x, attn_w_qkv, j, head_dim ** -0.5 * LOG2_E)
            o16, o_meta = _flash_prompt(qkv16, bias_near, bias_meta, bias_mm, attn_lambda, subln3, j, lam0,
                                        n_batch=n_batch, seq=seq, n_heads=n_heads, head_dim=head_dim,
                                        n_meta=n_meta)
            q32 = qkv16[samp0:samp0 + n_sample_rows, :d].astype(F32)
            o_samp = _decode_attn(q32, kv32, cache_k, cache_v, page_table, bias_past, bias_new, attn_lambda,
                                  subln3, j, lam0, new_row0=samp0, n_heads=n_heads, head_dim=head_dim)
            o_small = jnp.concatenate(
                [o_meta, o_samp, jnp.zeros((mt - samp0 - n_sample_rows, d), F32)], axis=0).astype(BF16)
            x = _outproj_ln(o16, o_small, attn_w_o, j, x, g_mix, b_mix, alpha)
            kv_layers.append(kv32)
        else:
            bg, u = _conv_in(x, conv_w_in, j)
            hist1, hist2 = conv_history(state_conv[j])
            z16 = _conv_gate(u, bg, hist1, hist2, conv_w, j, seq=seq, n_batch=n_batch, n_meta=n_meta,
                             dec_seq=dec_seq, n_sample_rows=n_sample_rows)
            x = _outproj_ln(z16, None, conv_w_out, j, x, g_mix, b_mix, alpha)
            c_p.append(jnp.stack([u[(b + 1) * seq - 2:(b + 1) * seq] for b in range(n_batch)]))
            c_s.append(u[samp0:samp0 + n_sample_rows].reshape(dec_batch, dec_seq, d)[:, dec_seq - 2:])
        last = i == depth - 1
        x = _mlp_ln(x, mlp_w_up, mlp_w_down, i, ln_mlp_g[i][None], ln_mlp_b[i][None], alpha,
                    n_main=mp if last else None)

    y_main, y_small = x
    y_prompt = y_main.reshape(n_batch, seq, d)
    y_sample = y_small[n_meta_rows:n_meta_rows + n_sample_rows].reshape(dec_batch, dec_seq, d)

    def new_cache(col0, tail):
        cols = slice(col0, col0 + d)
        pieces = []
        for kv in kv_layers:
            for b in range(n_batch):
                pieces.append(kv[meta0 + b * n_meta:meta0 + (b + 1) * n_meta, cols])
                pieces.append(kv[b * seq:(b + 1) * seq, cols])
        prompt = jnp.concatenate(pieces, axis=0).reshape((len(kv_layers), n_batch, n_meta + seq) + tail)
        sample = jnp.stack([kv[samp0:samp0 + n_sample_rows, cols].reshape((dec_batch, dec_seq) + tail)
                            for kv in kv_layers])
        return prompt, sample

    k_p, k_s = new_cache(0, (n_heads, 2, head_dim))
    v_p, v_s = new_cache(d, (n_heads, vd))
    return (y_prompt, y_sample, k_p, v_p, jnp.stack(c_p), k_s, v_s, jnp.stack(c_s))
```

```python
import functools
import math

import numpy as np
import jax
import jax.numpy as jnp
from jax import lax
from jax.experimental import pallas as pl
from jax.experimental.pallas import tpu as pltpu

F32 = jnp.float32
BF16 = jnp.bfloat16

MAX_DISTANCE = 128
LN_EPS = 1e-5
SUBLN_EPS = 1e-5
N_MIXERS = 2
MASK_VALUE = -1e30
LOG2_E = math.log2(math.e)

V7X_LANES = 128
V7X_SUBLANES = 8
V7X_VMEM_LIMIT_BYTES = 60 * 1024 * 1024

SMALL_ROWS = 128
FLASH_TQ = 512
FLASH_CK = 512
FLASH_HEADS = 2
FLASH_SUM_ROWS = 16
DECODE_PAGES = 8
DECODE_GROUP = 2

V7X_BF16_SUBLANES = 16
PROJ_ROWS = 1040
MLP_ROWS = 832
QKV_COLS = 512
CONV_IN_COLS = 256
OUTPROJ_K = 512
MLP_FF = 512
LN_ROWS = 208


def _lambda_init(layer):
    return 0.8 - 0.6 * math.exp(-0.3 * layer)


def _pick_tile(total, target, multiple):
    best = None
    for t in range(multiple, min(total, target) + 1, multiple):
        if total % t == 0:
            best = t
    assert best is not None, (total, target, multiple)
    return best


def _cparams(sem):
    return pltpu.CompilerParams(dimension_semantics=sem, vmem_limit_bytes=V7X_VMEM_LIMIT_BYTES)


def _bucket_of_distance(n, n_buckets):
    n = np.asarray(n, np.int64)
    max_exact = n_buckets // 2
    nf = np.maximum(n, 1).astype(np.float64)
    large = max_exact + (np.log(nf / max_exact) / math.log(MAX_DISTANCE / max_exact)
                         * (n_buckets - max_exact)).astype(np.int64)
    large = np.minimum(large, n_buckets - 1)
    return np.where(n < max_exact, n, large).astype(np.int32)


def _far_distance(n_buckets):
    tab = _bucket_of_distance(np.arange(4 * MAX_DISTANCE), n_buckets)
    not_last = np.nonzero(tab != n_buckets - 1)[0]
    return int(not_last[-1]) + 1


def _bucket_tile(dist, n_buckets):
    dist = np.asarray(dist, np.int64)
    return np.where(dist < 0, -1, _bucket_of_distance(np.maximum(dist, 0), n_buckets)).astype(np.int32)


def _bias_expand_kernel(rb_ref, idx_ref, o_ref, *, n_buckets):
    h = pl.program_id(0)
    idx = idx_ref[...]
    far = rb_ref[n_buckets - 1, h]
    out = jnp.zeros(idx.shape, F32)
    for b in range(n_buckets - 1):
        out = jnp.where(idx == b, (rb_ref[b, h] - far) * LOG2_E, out)
    o_ref[...] = jnp.where(idx < 0, MASK_VALUE, out)


def _bias_expand(idx_np, rel_bias):
    rows, cols = idx_np.shape
    n_buckets, n_heads = rel_bias.shape
    return pl.pallas_call(
        functools.partial(_bias_expand_kernel, n_buckets=n_buckets),
        grid=(n_heads,),
        in_specs=[pl.BlockSpec(memory_space=pltpu.SMEM),
                  pl.BlockSpec((rows, cols), lambda h: (0, 0))],
        out_specs=pl.BlockSpec((None, rows, cols), lambda h: (h, 0, 0)),
        out_shape=jax.ShapeDtypeStruct((n_heads, rows, cols), F32),
        compiler_params=_cparams(("arbitrary",)),
        name="bias_expand",
    )(rel_bias, jnp.asarray(idx_np))


def _qkv_kernel(x_ref, w_ref, o16_ref, kv32_ref, xb_ref, *, n_q_tiles, q_scale):
    j = pl.program_id(1)

    @pl.when(j == 0)
    def _():
        xb_ref[...] = x_ref[...].astype(BF16)

    acc = jnp.dot(xb_ref[...], w_ref[...].astype(BF16), preferred_element_type=F32)
    scale = jnp.where(j < n_q_tiles, q_scale, 1.0).astype(F32)
    o16_ref[...] = (acc * scale).astype(BF16)

    @pl.when(j >= n_q_tiles)
    def _():
        kv32_ref[...] = acc


def _qkv_proj(x, w_qkv, layer, q_scale):
    m, d = x.shape
    tm = _pick_tile(m, PROJ_ROWS, V7X_BF16_SUBLANES)
    tn = QKV_COLS
    n_q_tiles = d // tn
    return pl.pallas_call(
        functools.partial(_qkv_kernel, n_q_tiles=n_q_tiles, q_scale=q_scale),
        grid=(m // tm, 3 * d // tn),
        in_specs=[pl.BlockSpec((tm, d), lambda i, j: (i, 0)),
                  pl.BlockSpec((None, d, tn), lambda i, j: (layer, 0, j))],
        out_specs=[pl.BlockSpec((tm, tn), lambda i, j: (i, j)),
                   pl.BlockSpec((tm, tn), lambda i, j: (i, jnp.maximum(j - n_q_tiles, 0)))],
        out_shape=[jax.ShapeDtypeStruct((m, 3 * d), BF16),
                   jax.ShapeDtypeStruct((m, 2 * d), F32)],
        scratch_shapes=[pltpu.VMEM((tm, d), BF16)],
        compiler_params=_cparams(("arbitrary", "arbitrary")),
        name="qkv_proj",
    )(x, w_qkv)


def _convin_kernel(x_ref, wb_ref, wc_ref, wh_ref, bg_ref, u_ref, xb_ref):
    @pl.when(pl.program_id(1) == 0)
    def _():
        xb_ref[...] = x_ref[...].astype(BF16)

    xb = xb_ref[...]
    bg_ref[...] = jnp.dot(xb, wb_ref[...].astype(BF16), preferred_element_type=F32)
    cg = jnp.dot(xb, wc_ref[...].astype(BF16), preferred_element_type=F32)
    hh = jnp.dot(xb, wh_ref[...].astype(BF16), preferred_element_type=F32)
    u_ref[...] = cg * hh


def _conv_in(x, w_in, layer):
    m, d = x.shape
    tm = _pick_tile(m, PROJ_ROWS, V7X_BF16_SUBLANES)
    tn = CONV_IN_COLS
    nt = d // tn
    wspec = lambda part: pl.BlockSpec((None, d, tn), lambda i, j: (layer, 0, part * nt + j))
    return pl.pallas_call(
        _convin_kernel,
        grid=(m // tm, nt),
        in_specs=[pl.BlockSpec((tm, d), lambda i, j: (i, 0)), wspec(0), wspec(1), wspec(2)],
        out_specs=[pl.BlockSpec((tm, tn), lambda i, j: (i, j)),
                   pl.BlockSpec((tm, tn), lambda i, j: (i, j))],
        out_shape=[jax.ShapeDtypeStruct((m, d), F32), jax.ShapeDtypeStruct((m, d), F32)],
        scratch_shapes=[pltpu.VMEM((tm, d), BF16)],
        compiler_params=_cparams(("arbitrary", "arbitrary")),
        name="conv_in",
    )(x, w_in, w_in, w_in)


def _convgate_kernel(u_ref, bg_ref, halo_ref, meta_ref, h1_ref, h2_ref, w_ref, z_ref, *,
                     seq_blocks, n_prompt_blocks, n_batch, n_meta, dec_seq, n_sample_rows):
    i = pl.program_id(0)
    u = u_ref[...]
    rows = u.shape[0]
    r = lax.broadcasted_iota(jnp.int32, (rows, 1), 0)

    halo = halo_ref[...]
    prev2, prev1 = halo[V7X_SUBLANES - 2:V7X_SUBLANES - 1], halo[V7X_SUBLANES - 1:V7X_SUBLANES]
    meta = meta_ref[...]
    for b in range(n_batch):
        starts_batch = i == b * seq_blocks
        prev2 = jnp.where(starts_batch, meta[(b + 1) * n_meta - 2:(b + 1) * n_meta - 1], prev2)
        prev1 = jnp.where(starts_batch, meta[(b + 1) * n_meta - 1:(b + 1) * n_meta], prev1)

    p1 = jnp.where(r == 0, prev1, pltpu.roll(u, 1, axis=0))
    p2 = jnp.where(r == 0, prev2, jnp.where(r == 1, prev1, pltpu.roll(u, 2, axis=0)))

    small = i == n_prompt_blocks
    n_meta_rows = n_batch * n_meta
    in_meta = r < n_meta_rows
    in_sample = jnp.logical_and(r >= n_meta_rows, r < n_meta_rows + n_sample_rows)
    pos = jnp.where(in_meta, r % n_meta, (r - n_meta_rows) % dec_seq)
    restart = jnp.logical_or(in_meta, in_sample)
    m1 = jnp.logical_and(small, jnp.logical_and(restart, pos == 0))
    m2 = jnp.logical_and(small, jnp.logical_and(restart, pos <= 1))
    p1 = jnp.where(m1, h1_ref[...], p1)
    p2 = jnp.where(m2, h2_ref[...], p2)

    w = w_ref[...]
    conv = w[0:1] * p2 + w[1:2] * p1 + w[2:3] * u
    z_ref[...] = (bg_ref[...] * conv).astype(BF16)


def _conv_gate(u, bg, hist1, hist2, w_conv, layer, *, seq, n_batch, n_meta, dec_seq, n_sample_rows):
    m, d = u.shape
    rows = SMALL_ROWS
    n_prompt_blocks = n_batch * seq // rows
    assert m == (n_prompt_blocks + 1) * rows and seq % rows == 0
    halo_per_block = rows // V7X_SUBLANES
    kern = functools.partial(_convgate_kernel, seq_blocks=seq // rows, n_prompt_blocks=n_prompt_blocks,
                             n_batch=n_batch, n_meta=n_meta, dec_seq=dec_seq, n_sample_rows=n_sample_rows)
    return pl.pallas_call(
        kern,
        grid=(m // rows,),
        in_specs=[pl.BlockSpec((rows, d), lambda i: (i, 0)),
                  pl.BlockSpec((rows, d), lambda i: (i, 0)),
                  pl.BlockSpec((V7X_SUBLANES, d), lambda i: (jnp.maximum(i * halo_per_block - 1, 0), 0)),
                  pl.BlockSpec((rows, d), lambda i: (n_prompt_blocks, 0)),
                  pl.BlockSpec((rows, d), lambda i: (0, 0)),
                  pl.BlockSpec((rows, d), lambda i: (0, 0)),
                  pl.BlockSpec((None, 3, d), lambda i: (layer, 0, 0))],
        out_specs=pl.BlockSpec((rows, d), lambda i: (i, 0)),
        out_shape=jax.ShapeDtypeStruct((m, d), BF16),
        compiler_params=_cparams(("arbitrary",)),
        name="conv_gate",
    )(u, bg, u, u, hist1, hist2, w_conv)


def _residual_layer_norm(x_ref, o_ref, g_ref, b_ref, alpha, rows_chunk):
    g = g_ref[...]
    b = b_ref[...]

    def body(c, carry):
        rs = pl.ds(pl.multiple_of(c * rows_chunk, rows_chunk), rows_chunk)
        y = alpha * x_ref[rs, :] + o_ref[rs, :]
        mu = jnp.mean(y, axis=-1, keepdims=True)
        yc = y - mu
        var = jnp.mean(yc * yc, axis=-1, keepdims=True)
        o_ref[rs, :] = yc * lax.rsqrt(var + LN_EPS) * g + b
        return carry

    lax.fori_loop(0, x_ref.shape[0] // rows_chunk, body, 0)


def _outproj_kernel(a_ref, *rest, alpha, rows_chunk, n_main_last):
    small_ref = rest[0] if n_main_last is not None else None
    w_ref, x_ref, g_ref, b_ref, o_ref = rest[-5:]
    i = pl.program_id(0)
    k = pl.program_id(1)
    last_tile = pl.num_programs(0) - 1

    @pl.when(k == 0)
    def _():
        o_ref[...] = jnp.zeros_like(o_ref)

    w = w_ref[...].astype(BF16)
    if small_ref is None:
        o_ref[...] += jnp.dot(a_ref[...], w, preferred_element_type=F32)
    else:
        @pl.when(i < last_tile)
        def _():
            o_ref[...] += jnp.dot(a_ref[...], w, preferred_element_type=F32)

        @pl.when(i == last_tile)
        def _():
            a = jnp.concatenate([a_ref[:n_main_last], small_ref[...]], axis=0)
            o_ref[...] += jnp.dot(a, w, preferred_element_type=F32)

    @pl.when(k == pl.num_programs(1) - 1)
    def _():
        _residual_layer_norm(x_ref, o_ref, g_ref, b_ref, alpha, rows_chunk)


def _outproj_ln(a16, a16_small, w, layer, x, g, b, alpha):
    m, d = x.shape
    tm = _pick_tile(m, PROJ_ROWS, V7X_BF16_SUBLANES)
    tk = OUTPROJ_K
    rows_chunk = _pick_tile(tm, LN_ROWS, V7X_SUBLANES)
    if a16_small is None:
        assert a16.shape[0] == m
        n_main_last, small_args, small_specs = None, [], []
    else:
        n_small = a16_small.shape[0]
        assert a16.shape[0] + n_small == m and n_small <= tm
        n_main_last = tm - n_small
        small_args = [a16_small]
        small_specs = [pl.BlockSpec((n_small, tk), lambda i, k: (0, k))]
    return pl.pallas_call(
        functools.partial(_outproj_kernel, alpha=alpha, rows_chunk=rows_chunk, n_main_last=n_main_last),
        grid=(m // tm, d // tk),
        in_specs=[pl.BlockSpec((tm, tk), lambda i, k: (i, k))] + small_specs + [
                  pl.BlockSpec((None, tk, d), lambda i, k: (layer, k, 0)),
                  pl.BlockSpec((tm, d), lambda i, k: (i, 0)),
                  pl.BlockSpec((1, d), lambda i, k: (0, 0)),
                  pl.BlockSpec((1, d), lambda i, k: (0, 0))],
        out_specs=pl.BlockSpec((tm, d), lambda i, k: (i, 0)),
        out_shape=jax.ShapeDtypeStruct((m, d), F32),
        compiler_params=_cparams(("arbitrary", "arbitrary")),
        name="outproj_ln",
    )(a16, *small_args, w, x, g, b)


def _mlp_kernel(x_ref, wu_ref, wd_ref, g_ref, b_ref, o_ref, *rest, alpha, rows_chunk, n_main_last):
    small_ref = rest[0] if n_main_last is not None else None
    xb_ref = rest[-1]
    f = pl.program_id(1)

    @pl.when(f == 0)
    def _():
        xb_ref[...] = x_ref[...].astype(BF16)
        o_ref[...] = jnp.zeros_like(o_ref)

    h = jnp.dot(xb_ref[...], wu_ref[...].astype(BF16), preferred_element_type=F32)
    h = jnp.square(jnp.maximum(h, 0.0)).astype(BF16)
    o_ref[...] += jnp.dot(h, wd_ref[...].astype(BF16), preferred_element_type=F32)

    @pl.when(f == pl.num_programs(1) - 1)
    def _():
        _residual_layer_norm(x_ref, o_ref, g_ref, b_ref, alpha, rows_chunk)
        if small_ref is not None:
            @pl.when(pl.program_id(0) == pl.num_programs(0) - 1)
            def _():
                small_ref[...] = o_ref[n_main_last:, :]


def _mlp_ln(x, w_up, w_down, layer, g, b, alpha, n_main=None):
    m, d = x.shape
    d_ff = w_up.shape[-1]
    tm = _pick_tile(m, MLP_ROWS, V7X_BF16_SUBLANES)
    tf = MLP_FF
    rows_chunk = _pick_tile(tm, LN_ROWS, V7X_SUBLANES)
    out_specs = pl.BlockSpec((tm, d), lambda i, f: (i, 0))
    out_shape = jax.ShapeDtypeStruct((m, d), F32)
    n_main_last = None
    if n_main is not None:
        n_small = m - n_main
        assert 0 < n_small <= tm
        n_main_last = tm - n_small
        out_specs = [out_specs, pl.BlockSpec((n_small, d), lambda i, f: (0, 0))]
        out_shape = [jax.ShapeDtypeStruct((n_main, d), F32), jax.ShapeDtypeStruct((n_small, d), F32)]
    return pl.pallas_call(
        functools.partial(_mlp_kernel, alpha=alpha, rows_chunk=rows_chunk, n_main_last=n_main_last),
        grid=(m // tm, d_ff // tf),
        in_specs=[pl.BlockSpec((tm, d), lambda i, f: (i, 0)),
                  pl.BlockSpec((None, d, tf), lambda i, f: (layer, 0, f)),
                  pl.BlockSpec((None, tf, d), lambda i, f: (layer, f, 0)),
                  pl.BlockSpec((1, d), lambda i, f: (0, 0)),
                  pl.BlockSpec((1, d), lambda i, f: (0, 0))],
        out_specs=out_specs,
        out_shape=out_shape,
        scratch_shapes=[pltpu.VMEM((tm, d), BF16)],
        compiler_params=_cparams(("arbitrary", "arbitrary")),
        name="mlp_ln",
    )(x, w_up, w_down, g, b)


def _diff_lambda(lam_ref, lam_init):
    lp = lam_ref[...]
    a = jnp.sum(lp[0:1] * lp[1:2], axis=-1, keepdims=True)
    b = jnp.sum(lp[2:3] * lp[3:4], axis=-1, keepdims=True)
    return jnp.exp(a) - jnp.exp(b) + lam_init


def _sub_layer_norm(o, g_ref, lam_init):
    return o * lax.rsqrt(jnp.mean(o * o, axis=-1, keepdims=True) + SUBLN_EPS) * (g_ref[...] * (1.0 - lam_init))


def _dot_nt(a, b):
    return lax.dot_general(a, b, (((1,), (1,)), ((), ())), preferred_element_type=F32)


def _flash_kernel(q_ref, k_ref, v_ref, km_ref, vm_ref, qm_ref, bn_ref, bm_ref, bmm_ref, lam_ref, g_ref,
                  o_ref, om_ref, vt_sc, s0_sc, s1_sc, m_sc, acc_sc, *, head_dim, lam_init):
    qi = pl.program_id(2)
    tq = q_ref.shape[0]
    n_local, _, ck, _ = bn_ref.shape
    ratio = tq // ck
    n_chunks = k_ref.shape[0] // ck
    n_meta = km_ref.shape[0]
    vd = 2 * head_dim
    heads = range(n_local)
    lam = _diff_lambda(lam_ref, lam_init)
    cols = lambda hh: slice(hh * vd, (hh + 1) * vd)

    @pl.when(qi == 0)
    def _():
        extra = vt_sc.shape[2] - vd
        ones_rows = (lax.broadcasted_iota(jnp.int32, (extra, ck), 0) == 0).astype(BF16)
        for hh in heads:
            def body(c, carry):
                rs = pl.ds(pl.multiple_of(c * ck, ck), ck)
                vt_sc[hh, c, :vd] = v_ref[rs, cols(hh)].astype(F32).T.astype(BF16)
                vt_sc[hh, c, vd:] = ones_rows
                return carry

            lax.fori_loop(0, n_chunks, body, 0)
            vm_pad = jnp.concatenate([vm_ref[:, cols(hh)].astype(F32), jnp.zeros((ck - n_meta, vd), F32)], axis=0)
            vt_sc[hh, n_chunks, :vd] = vm_pad.T.astype(BF16)
            vt_sc[hh, n_chunks, vd:] = ones_rows

    def scores_t(hh, keys, bias_t):
        parts = [_dot_nt(keys[:, m * head_dim:(m + 1) * head_dim],
                         q_ref[:, hh * vd + m * head_dim:hh * vd + (m + 1) * head_dim]) for m in range(2)]
        if bias_t is not None:
            parts = [p + bias_t for p in parts]
        return jnp.concatenate(parts, axis=1)

    for hh in heads:
        s = scores_t(hh, km_ref[:, cols(hh)], jnp.where(qi == 0, bm_ref[hh], 0.0))
        m0 = jnp.max(s, axis=0, keepdims=True)
        p = jnp.exp2(s - m0)
        m_sc[hh] = m0
        p_pad = jnp.concatenate([p, jnp.zeros((V7X_LANES - n_meta, 2 * tq), F32)], axis=0).astype(BF16)
        acc_sc[hh] = jnp.dot(vt_sc[hh, n_chunks][:, :V7X_LANES], p_pad, preferred_element_type=F32)

    assert ratio == 1
    s_bufs = (s0_sc, s1_sc)

    def scores_into(c, tile, s_buf):
        for hh in heads:
            kb = k_ref[pl.ds(pl.multiple_of(c * ck, ck), ck), cols(hh)]
            s_buf[hh] = scores_t(hh, kb, None if tile is None else bn_ref[hh, tile])

    def update_from(c, s_buf):
        for hh in heads:
            s = s_buf[hh]
            m_old = m_sc[hh]
            m_new = jnp.maximum(m_old, jnp.max(s, axis=0, keepdims=True))
            alpha = jnp.exp2(m_old - m_new)
            p = jnp.exp2(s - m_new)
            acc_sc[hh] = alpha * acc_sc[hh] + jnp.dot(vt_sc[hh, c], p.astype(BF16), preferred_element_type=F32)
            m_sc[hh] = m_new

    def by_parity(c, fn):
        for par in range(2):
            @pl.when(jnp.bitwise_and(c, 1) == par)
            def _():
                fn(s_bufs[par], s_bufs[1 - par])

    def advance(c, next_tile):
        def fn(cur, nxt):
            scores_into(c + 1, next_tile, nxt)
            update_from(c, cur)
        by_parity(c, fn)

    @pl.when(qi == 0)
    def _():
        scores_into(0, 1, s_bufs[0])

    @pl.when(qi == 1)
    def _():
        scores_into(0, 0, s_bufs[0])

    @pl.when(qi >= 2)
    def _():
        scores_into(0, None, s_bufs[0])

        def far_body(c, carry):
            advance(c, None)
            return carry

        lax.fori_loop(0, qi - 2, far_body, 0)
        advance(qi - 2, 0)

    @pl.when(qi >= 1)
    def _():
        advance(qi - 1, 1)

    by_parity(qi, lambda cur, nxt: update_from(qi, cur))

    for hh in heads:
        acc = acc_sc[hh]
        on = acc[:vd] * (1.0 / acc[vd:vd + 1])
        o_t = on[:, :tq] - lam * on[:, tq:]
        o_t = o_t * lax.rsqrt(jnp.mean(o_t * o_t, axis=0, keepdims=True) + SUBLN_EPS)
        o_ref[:, cols(hh)] = (o_t.T * (g_ref[...] * (1.0 - lam_init))).astype(BF16)

    @pl.when(qi == 0)
    def _():
        for hh in heads:
            qm = qm_ref[:, cols(hh)]
            km = km_ref[:, cols(hh)]
            sm = jnp.concatenate([_dot_nt(qm[:, m * head_dim:(m + 1) * head_dim],
                                          km[:, m * head_dim:(m + 1) * head_dim]) + bmm_ref[hh]
                                  for m in range(2)], axis=0)
            pm = jnp.exp2(sm - jnp.max(sm, axis=-1, keepdims=True))
            onm = (jnp.dot(pm.astype(BF16), vm_ref[:, cols(hh)], preferred_element_type=F32)
                   * (1.0 / jnp.sum(pm, axis=-1, keepdims=True)))
            om_ref[:, cols(hh)] = _sub_layer_norm(onm[:n_meta] - lam * onm[n_meta:], g_ref, lam_init)


def _flash_prompt(qkv16, bias_near, bias_meta, bias_mm, lam_params, subln, layer, lam_init, *,
                  n_batch, seq, n_heads, head_dim, n_meta):
    m, d3 = qkv16.shape
    d = d3 // 3
    vd = 2 * head_dim
    tq = FLASH_TQ
    hp = FLASH_HEADS
    assert n_heads % hp == 0
    ng = n_heads // hp
    nq = seq // tq
    mp = n_batch * seq
    meta_blk = mp // n_meta
    kern = functools.partial(_flash_kernel, head_dim=head_dim, lam_init=lam_init)
    return pl.pallas_call(
        kern,
        grid=(n_batch, ng, nq),
        in_specs=[pl.BlockSpec((tq, hp * vd), lambda b, h, qi: (b * nq + qi, h)),
                  pl.BlockSpec((seq, hp * vd), lambda b, h, qi: (b, ng + h)),
                  pl.BlockSpec((seq, hp * vd), lambda b, h, qi: (b, 2 * ng + h)),
                  pl.BlockSpec((n_meta, hp * vd), lambda b, h, qi: (meta_blk + b, ng + h)),
                  pl.BlockSpec((n_meta, hp * vd), lambda b, h, qi: (meta_blk + b, 2 * ng + h)),
                  pl.BlockSpec((n_meta, hp * vd), lambda b, h, qi: (meta_blk + b, h)),
                  pl.BlockSpec((hp,) + bias_near.shape[1:], lambda b, h, qi: (h, 0, 0, 0)),
                  pl.BlockSpec((hp,) + bias_meta.shape[1:], lambda b, h, qi: (h, 0, 0)),
                  pl.BlockSpec((hp,) + bias_mm.shape[1:], lambda b, h, qi: (h, 0, 0)),
                  pl.BlockSpec((None, 4, head_dim), lambda b, h, qi: (layer, 0, 0)),
                  pl.BlockSpec((None, 1, vd), lambda b, h, qi: (layer, 0, 0))],
        out_specs=[pl.BlockSpec((tq, hp * vd), lambda b, h, qi: (b * nq + qi, h)),
                   pl.BlockSpec((n_meta, hp * vd), lambda b, h, qi: (b, h))],
        out_shape=[jax.ShapeDtypeStruct((mp, d), BF16),
                   jax.ShapeDtypeStruct((n_batch * n_meta, d), F32)],
        scratch_shapes=[pltpu.VMEM((hp, seq // FLASH_CK + 1, vd + FLASH_SUM_ROWS, FLASH_CK), BF16),
                        pltpu.VMEM((hp, FLASH_CK, 2 * tq), F32), pltpu.VMEM((hp, FLASH_CK, 2 * tq), F32),
                        pltpu.VMEM((hp, 1, 2 * tq), F32),
                        pltpu.VMEM((hp, vd + FLASH_SUM_ROWS, 2 * tq), F32)],
        compiler_params=_cparams(("arbitrary", "arbitrary", "arbitrary")),
        name="flash_prompt",
    )(qkv16, qkv16, qkv16, qkv16, qkv16, qkv16, bias_near, bias_meta, bias_mm, lam_params, subln)


def _decode_kernel(pt_ref, q_ref, kn_ref, vn_ref, *rest, n_heads, head_dim, n_pages_step, lam_init):
    kp_refs = rest[:n_pages_step]
    vp_refs = rest[n_pages_step:2 * n_pages_step]
    bd_ref, bnw_ref, lam_ref, g_ref, o_ref, qc_sc, hm_sc, m_sc, l_sc, acc_sc = rest[2 * n_pages_step:]
    st = pl.program_id(1)
    last = pl.num_programs(1) - 1
    dec_seq, d = q_ref.shape
    vd = 2 * head_dim
    map_rows = n_heads * dec_seq
    n_groups = 2 * n_heads

    @pl.when(st == 0)
    def _():
        q = q_ref[...]
        for mp in range(2):
            qc_sc[mp] = jnp.concatenate(
                [q[:, (2 * h + mp) * head_dim:(2 * h + mp + 1) * head_dim] for h in range(n_heads)],
                axis=0).astype(BF16)
        r = lax.broadcasted_iota(jnp.int32, hm_sc.shape, 0)
        c = lax.broadcasted_iota(jnp.int32, hm_sc.shape, 1)
        hm_sc[...] = jnp.where((r // dec_seq) % n_heads == c % n_heads, 0.0, MASK_VALUE)

        qt = jnp.concatenate([q] * n_groups, axis=0)
        grp = lax.broadcasted_iota(jnp.int32, qt.shape, 0) // dec_seq
        col = lax.broadcasted_iota(jnp.int32, qt.shape, 1) // head_dim
        qb = jnp.where(col == (grp % n_heads) * 2 + grp // n_heads, qt, 0.0).astype(BF16)
        pad = jnp.zeros((dec_seq, d), F32)
        kn = jnp.concatenate([kn_ref[...], pad], axis=0).astype(BF16)
        vn = jnp.concatenate([vn_ref[...], pad], axis=0).astype(BF16)
        s = _dot_nt(qb, kn) + bnw_ref[...]
        m0 = jnp.max(s, axis=-1, keepdims=True)
        p = jnp.exp2(s - m0)
        m_sc[...] = m0
        l_sc[...] = jnp.sum(p, axis=-1, keepdims=True)
        full = jnp.dot(p.astype(BF16), vn, preferred_element_type=F32)
        acc_sc[...] = jnp.concatenate(
            [full[g * dec_seq:(g + 1) * dec_seq, (g % n_heads) * vd:(g % n_heads + 1) * vd]
             for g in range(n_groups)], axis=0)

    th_rows = vp_refs[0].shape[0]
    lanes = DECODE_GROUP * th_rows
    head_mask = jnp.tile(hm_sc[...], (1, lanes // hm_sc.shape[1]))

    def scores(g):
        pages = kp_refs[g * DECODE_GROUP:(g + 1) * DECODE_GROUP]
        s = jnp.concatenate(
            [_dot_nt(qc_sc[mp], jnp.concatenate([kp[pl.ds(mp, th_rows, stride=2), :] for kp in pages],
                                                axis=0).astype(BF16)) for mp in range(2)], axis=0)
        return s + head_mask + jnp.where(st == last, bd_ref[:, g * lanes:(g + 1) * lanes], 0.0)

    n_groups_step = n_pages_step // DECODE_GROUP
    group_scores = [scores(g) for g in range(n_groups_step)]
    for g in range(n_groups_step):
        s = group_scores[g]
        v16 = jnp.concatenate([vp[...].astype(BF16) for vp in vp_refs[g * DECODE_GROUP:(g + 1) * DECODE_GROUP]],
                              axis=0)
        m_old = m_sc[...]
        m_new = jnp.maximum(m_old, jnp.max(s, axis=-1, keepdims=True))
        alpha = jnp.exp2(m_old - m_new)
        p = jnp.exp2(s - m_new)
        l_sc[...] = alpha * l_sc[...] + jnp.sum(p, axis=-1, keepdims=True)
        acc_sc[...] = alpha * acc_sc[...] + jnp.dot(p.astype(BF16), v16, preferred_element_type=F32)
        m_sc[...] = m_new

    @pl.when(st == last)
    def _():
        lam = _diff_lambda(lam_ref, lam_init)
        on = acc_sc[...] * (1.0 / l_sc[...])
        for h in range(n_heads):
            oh = (on[h * dec_seq:(h + 1) * dec_seq]
                  - lam * on[map_rows + h * dec_seq:map_rows + (h + 1) * dec_seq])
            o_ref[:, h * vd:(h + 1) * vd] = _sub_layer_norm(oh, g_ref, lam_init)


def _decode_attn(q32, kv32, cache_k, cache_v, page_table, bias_past, bias_new, lam_params, subln, layer,
                 lam_init, *, new_row0, n_heads, head_dim):
    n_rows, d = q32.shape
    dec_batch, n_pages = page_table.shape
    dec_seq = n_rows // dec_batch
    n_attn, n_pool, page = cache_k.shape[:3]
    ck = cache_k.reshape(n_attn, n_pool, page * 2 * n_heads, head_dim)
    cv = cache_v.reshape(n_attn, n_pool, page * n_heads, 2 * head_dim)
    nps = DECODE_PAGES
    assert n_pages % nps == 0 and nps % DECODE_GROUP == 0 and new_row0 % dec_seq == 0
    n_steps = n_pages // nps
    vd = 2 * head_dim
    rows = 2 * n_heads * dec_seq
    new_blk = new_row0 // dec_seq

    def page_spec(t, arr):
        return pl.BlockSpec((None, None) + arr.shape[2:],
                            lambda bd, st, pt: (layer, pt[bd * n_pages + st * nps + t], 0, 0))

    grid_spec = pltpu.PrefetchScalarGridSpec(
        num_scalar_prefetch=1,
        grid=(dec_batch, n_steps),
        in_specs=[pl.BlockSpec((dec_seq, d), lambda bd, st, pt: (bd, 0)),
                  pl.BlockSpec((dec_seq, d), lambda bd, st, pt: (new_blk + bd, 0)),
                  pl.BlockSpec((dec_seq, d), lambda bd, st, pt: (new_blk + bd, 1))]
                 + [page_spec(t, ck) for t in range(nps)] + [page_spec(t, cv) for t in range(nps)]
                 + [pl.BlockSpec(bias_past.shape, lambda bd, st, pt: (0, 0)),
                    pl.BlockSpec(bias_new.shape, lambda bd, st, pt: (0, 0)),
                    pl.BlockSpec((None, 4, head_dim), lambda bd, st, pt: (layer, 0, 0)),
                    pl.BlockSpec((None, 1, vd), lambda bd, st, pt: (layer, 0, 0))],
        out_specs=pl.BlockSpec((dec_seq, d), lambda bd, st, pt: (bd, 0)),
        scratch_shapes=[pltpu.VMEM((2, rows // 2, head_dim), BF16), pltpu.VMEM((rows, V7X_LANES), F32),
                        pltpu.VMEM((rows, 1), F32), pltpu.VMEM((rows, 1), F32),
                        pltpu.VMEM((rows, vd), F32)],
    )
    kern = functools.partial(_decode_kernel, n_heads=n_heads, head_dim=head_dim, n_pages_step=nps,
                             lam_init=lam_init)
    return pl.pallas_call(
        kern,
        grid_spec=grid_spec,
        out_shape=jax.ShapeDtypeStruct((n_rows, d), F32),
        compiler_params=_cparams(("arbitrary", "arbitrary")),
        name="decode_attn",
    )(page_table.reshape(-1), q32, kv32, kv32, *([ck] * nps), *([cv] * nps),
      bias_past, bias_new, lam_params, subln)


def kernel(x_prompt, x_sample, cache_k, cache_v, state_conv, page_table, meta_tokens, rel_bias,
           attn_w_qkv, attn_lambda, attn_subln, attn_w_o, conv_w_in, conv_w, conv_w_out,
           mlp_w_up, mlp_w_down, ln_mix_g, ln_mix_b, ln_mlp_g, ln_mlp_b):
    n_batch, seq, d = x_prompt.shape
    dec_batch, dec_seq, _ = x_sample.shape
    n_meta = meta_tokens.shape[0]
    n_buckets, n_heads = rel_bias.shape
    head_dim = cache_k.shape[-1]
    vd = cache_v.shape[-1]
    page = cache_k.shape[2]
    depth = mlp_w_up.shape[0]
    assert d == n_heads * vd and vd == 2 * head_dim
    alpha = (2 * depth) ** 0.25

    mp = n_batch * seq
    n_meta_rows = n_batch * n_meta
    n_sample_rows = dec_batch * dec_seq
    assert n_meta_rows + n_sample_rows <= SMALL_ROWS and seq % FLASH_TQ == 0
    assert n_meta % V7X_SUBLANES == 0 and dec_seq % V7X_SUBLANES == 0
    meta0, samp0 = mp, mp + n_meta_rows
    mt = mp + SMALL_ROWS

    x = jnp.concatenate(
        [x_prompt.reshape(mp, d)] + [meta_tokens.astype(F32)] * n_batch
        + [x_sample.reshape(n_sample_rows, d), jnp.zeros((mt - samp0 - n_sample_rows, d), F32)], axis=0)

    far = _far_distance(n_buckets)
    tq, ck, nps = FLASH_TQ, FLASH_CK, DECODE_PAGES
    ratio = tq // ck
    assert ck + 1 >= far and n_meta + tq - (n_meta - 1) >= far and (nps - 1) * page + 1 >= far
    qq = np.arange(tq)[None, :]
    near_idx = np.stack([_bucket_tile(qq - (t - 1) * ck - np.arange(ck)[:, None], n_buckets)
                         for t in range(ratio + 1)])
    meta_idx = _bucket_tile(n_meta + qq - np.arange(n_meta)[:, None], n_buckets)
    mm_idx = _bucket_tile(np.arange(n_meta)[:, None] - np.arange(n_meta)[None, :], n_buckets)
    step_of_row = (np.arange(2 * dec_seq) % dec_seq)[:, None]
    past_token = (np.arange(nps * page * n_heads) // n_heads)[None, :]
    past_idx = _bucket_tile(nps * page + step_of_row - past_token, n_buckets)
    new_cols = np.arange(2 * dec_seq)[None, :]
    new_idx = np.where(new_cols < dec_seq, _bucket_tile(step_of_row - new_cols, n_buckets), -1).astype(np.int32)

    bias_near = _bias_expand(near_idx.reshape((ratio + 1) * ck, tq), rel_bias).reshape(
        n_heads, ratio + 1, ck, tq)
    bias_meta = _bias_expand(meta_idx, rel_bias)
    bias_mm = _bias_expand(mm_idx, rel_bias)

    def decode_rows(tile):
        lanes = tile.shape[-1]
        return tile.reshape(n_heads, 2, dec_seq, lanes).transpose(1, 0, 2, 3).reshape(2 * n_heads * dec_seq, lanes)

    bias_past = decode_rows(_bias_expand(past_idx, rel_bias))
    bias_new = decode_rows(_bias_expand(new_idx, rel_bias))

    def conv_history(state):
        h1 = jnp.zeros((dec_batch, dec_seq, d), F32).at[:, 0].set(state[:, 1])
        h2 = jnp.zeros((dec_batch, dec_seq, d), F32).at[:, 0].set(state[:, 0]).at[:, 1].set(state[:, 1])
        pad_lo = jnp.zeros((n_meta_rows, d), F32)
        pad_hi = jnp.zeros((SMALL_ROWS - n_meta_rows - n_sample_rows, d), F32)
        cat = lambda h: jnp.concatenate([pad_lo, h.reshape(n_sample_rows, d), pad_hi], axis=0)
        return cat(h1), cat(h2)

    subln3 = attn_subln.reshape(attn_subln.shape[0], 1, vd)
    kv_layers, c_p, c_s = [], [], []
    for i in range(depth):
        j = i // N_MIXERS
        g_mix, b_mix = ln_mix_g[i][None], ln_mix_b[i][None]
        if i % N_MIXERS == 0:
            lam0 = _lambda_init(i)
            qkv16, kv32 = _qkv_proj(x, attn_w_qkv, j, head_dim ** -0.5 * LOG2_E)
            o16, o_meta = _flash_prompt(qkv16, bias_near, bias_meta, bias_mm, attn_lambda, subln3, j, lam0,
                                        n_batch=n_batch, seq=seq, n_heads=n_heads, head_dim=head_dim,
                                        n_meta=n_meta)
            q32 = qkv16[samp0:samp0 + n_sample_rows, :d].astype(F32)
            o_samp = _decode_attn(q32, kv32, cache_k, cache_v, page_table, bias_past, bias_new, attn_lambda,
                                  subln3, j, lam0, new_row0=samp0, n_heads=n_heads, head_dim=head_dim)
            o_small = jnp.concatenate(
                [o_meta, o_samp, jnp.zeros((mt - samp0 - n_sample_rows, d), F32)], axis=0).astype(BF16)
            x = _outproj_ln(o16, o_small, attn_w_o, j, x, g_mix, b_mix, alpha)
            kv_layers.append(kv32)
        else:
            bg, u = _conv_in(x, conv_w_in, j)
            hist1, hist2 = conv_history(state_conv[j])
            z16 = _conv_gate(u, bg, hist1, hist2, conv_w, j, seq=seq, n_batch=n_batch, n_meta=n_meta,
                             dec_seq=dec_seq, n_sample_rows=n_sample_rows)
            x = _outproj_ln(z16, None, conv_w_out, j, x, g_mix, b_mix, alpha)
            c_p.append(jnp.stack([u[(b + 1) * seq - 2:(b + 1) * seq] for b in range(n_batch)]))
            c_s.append(u[samp0:samp0 + n_sample_rows].reshape(dec_batch, dec_seq, d)[:, dec_seq - 2:])
        last = i == depth - 1
        x = _mlp_ln(x, mlp_w_up, mlp_w_down, i, ln_mlp_g[i][None], ln_mlp_b[i][None], alpha,
                    n_main=mp if last else None)

    y_main, y_small = x
    y_prompt = y_main.reshape(n_batch, seq, d)
    y_sample = y_small[n_meta_rows:n_meta_rows + n_sample_rows].reshape(dec_batch, dec_seq, d)

    def new_cache(col0, tail):
        cols = slice(col0, col0 + d)
        pieces = []
        for kv in kv_layers:
            for b in range(n_batch):
                pieces.append(kv[meta0 + b * n_meta:meta0 + (b + 1) * n_meta, cols])
                pieces.append(kv[b * seq:(b + 1) * seq, cols])
        prompt = jnp.concatenate(pieces, axis=0).reshape((len(kv_layers), n_batch, n_meta + seq) + tail)
        sample = jnp.stack([kv[samp0:samp0 + n_sample_rows, cols].reshape((dec_batch, dec_seq) + tail)
                            for kv in kv_layers])
        return prompt, sample

    k_p, k_s = new_cache(0, (n_heads, 2, head_dim))
    v_p, v_s = new_cache(d, (n_heads, vd))
    return (y_prompt, y_sample, k_p, v_p, jnp.stack(c_p), k_s, v_s, jnp.stack(c_s))
```

```python
import functools
import math

import numpy as np
import jax
import jax.numpy as jnp
from jax import lax
from jax.experimental import pallas as pl
from jax.experimental.pallas import tpu as pltpu

F32 = jnp.float32
BF16 = jnp.bfloat16

MAX_DISTANCE = 128
LN_EPS = 1e-5
SUBLN_EPS = 1e-5
N_MIXERS = 2
MASK_VALUE = -1e30
LOG2_E = math.log2(math.e)

V7X_LANES = 128
V7X_SUBLANES = 8
V7X_VMEM_LIMIT_BYTES = 60 * 1024 * 1024

SMALL_ROWS = 128
FLASH_TQ = 512
FLASH_CK = 512
FLASH_HEADS = 2
FLASH_SUM_ROWS = 16
DECODE_PAGES = 8
DECODE_GROUP = 2

V7X_BF16_SUBLANES = 16
PROJ_ROWS = 832
OUTPROJ_ROWS = 1040
MLP_ROWS = 832
QKV_COLS = 1024
CONV_IN_COLS = 512
OUTPROJ_K = 512
MLP_FF = 512
LN_ROWS = 208


def _lambda_init(layer):
    return 0.8 - 0.6 * math.exp(-0.3 * layer)


def _pick_tile(total, target, multiple):
    best = None
    for t in range(multiple, min(total, target) + 1, multiple):
        if total % t == 0:
            best = t
    assert best is not None, (total, target, multiple)
    return best


def _cparams(sem):
    return pltpu.CompilerParams(dimension_semantics=sem, vmem_limit_bytes=V7X_VMEM_LIMIT_BYTES)


def _bucket_of_distance(n, n_buckets):
    n = np.asarray(n, np.int64)
    max_exact = n_buckets // 2
    nf = np.maximum(n, 1).astype(np.float64)
    large = max_exact + (np.log(nf / max_exact) / math.log(MAX_DISTANCE / max_exact)
                         * (n_buckets - max_exact)).astype(np.int64)
    large = np.minimum(large, n_buckets - 1)
    return np.where(n < max_exact, n, large).astype(np.int32)


def _far_distance(n_buckets):
    tab = _bucket_of_distance(np.arange(4 * MAX_DISTANCE), n_buckets)
    not_last = np.nonzero(tab != n_buckets - 1)[0]
    return int(not_last[-1]) + 1


def _bucket_tile(dist, n_buckets):
    dist = np.asarray(dist, np.int64)
    return np.where(dist < 0, -1, _bucket_of_distance(np.maximum(dist, 0), n_buckets)).astype(np.int32)


def _bias_expand_kernel(rb_ref, idx_ref, o_ref, *, n_buckets):
    h = pl.program_id(0)
    idx = idx_ref[...]
    far = rb_ref[n_buckets - 1, h]
    out = jnp.zeros(idx.shape, F32)
    for b in range(n_buckets - 1):
        out = jnp.where(idx == b, (rb_ref[b, h] - far) * LOG2_E, out)
    o_ref[...] = jnp.where(idx < 0, MASK_VALUE, out)


def _bias_expand(idx_np, rel_bias):
    rows, cols = idx_np.shape
    n_buckets, n_heads = rel_bias.shape
    return pl.pallas_call(
        functools.partial(_bias_expand_kernel, n_buckets=n_buckets),
        grid=(n_heads,),
        in_specs=[pl.BlockSpec(memory_space=pltpu.SMEM),
                  pl.BlockSpec((rows, cols), lambda h: (0, 0))],
        out_specs=pl.BlockSpec((None, rows, cols), lambda h: (h, 0, 0)),
        out_shape=jax.ShapeDtypeStruct((n_heads, rows, cols), F32),
        compiler_params=_cparams(("arbitrary",)),
        name="bias_expand",
    )(rel_bias, jnp.asarray(idx_np))


def _qkv_kernel(x_ref, w_ref, o16_ref, kv32_ref, xb_ref, *, n_q_tiles, q_scale):
    j = pl.program_id(1)

    @pl.when(j == 0)
    def _():
        xb_ref[...] = x_ref[...].astype(BF16)

    acc = jnp.dot(xb_ref[...], w_ref[...].astype(BF16), preferred_element_type=F32)
    scale = jnp.where(j < n_q_tiles, q_scale, 1.0).astype(F32)
    o16_ref[...] = (acc * scale).astype(BF16)

    @pl.when(j >= n_q_tiles)
    def _():
        kv32_ref[...] = acc


def _qkv_proj(x, w_qkv, layer, q_scale):
    m, d = x.shape
    tm = _pick_tile(m, PROJ_ROWS, V7X_BF16_SUBLANES)
    tn = QKV_COLS
    n_q_tiles = d // tn
    return pl.pallas_call(
        functools.partial(_qkv_kernel, n_q_tiles=n_q_tiles, q_scale=q_scale),
        grid=(m // tm, 3 * d // tn),
        in_specs=[pl.BlockSpec((tm, d), lambda i, j: (i, 0)),
                  pl.BlockSpec((None, d, tn), lambda i, j: (layer, 0, j))],
        out_specs=[pl.BlockSpec((tm, tn), lambda i, j: (i, j)),
                   pl.BlockSpec((tm, tn), lambda i, j: (i, jnp.maximum(j - n_q_tiles, 0)))],
        out_shape=[jax.ShapeDtypeStruct((m, 3 * d), BF16),
                   jax.ShapeDtypeStruct((m, 2 * d), F32)],
        scratch_shapes=[pltpu.VMEM((tm, d), BF16)],
        compiler_params=_cparams(("arbitrary", "arbitrary")),
        name="qkv_proj",
    )(x, w_qkv)


def _convin_kernel(x_ref, wb_ref, wc_ref, wh_ref, bg_ref, u_ref, xb_ref):
    @pl.when(pl.program_id(1) == 0)
    def _():
        xb_ref[...] = x_ref[...].astype(BF16)

    xb = xb_ref[...]
    bg_ref[...] = jnp.dot(xb, wb_ref[...].astype(BF16), preferred_element_type=F32)
    cg = jnp.dot(xb, wc_ref[...].astype(BF16), preferred_element_type=F32)
    hh = jnp.dot(xb, wh_ref[...].astype(BF16), preferred_element_type=F32)
    u_ref[...] = cg * hh


def _conv_in(x, w_in, layer):
    m, d = x.shape
    tm = _pick_tile(m, PROJ_ROWS, V7X_BF16_SUBLANES)
    tn = CONV_IN_COLS
    nt = d // tn
    wspec = lambda part: pl.BlockSpec((None, d, tn), lambda i, j: (layer, 0, part * nt + j))
    return pl.pallas_call(
        _convin_kernel,
        grid=(m // tm, nt),
        in_specs=[pl.BlockSpec((tm, d), lambda i, j: (i, 0)), wspec(0), wspec(1), wspec(2)],
        out_specs=[pl.BlockSpec((tm, tn), lambda i, j: (i, j)),
                   pl.BlockSpec((tm, tn), lambda i, j: (i, j))],
        out_shape=[jax.ShapeDtypeStruct((m, d), F32), jax.ShapeDtypeStruct((m, d), F32)],
        scratch_shapes=[pltpu.VMEM((tm, d), BF16)],
        compiler_params=_cparams(("arbitrary", "arbitrary")),
        name="conv_in",
    )(x, w_in, w_in, w_in)


def _convgate_kernel(u_ref, bg_ref, halo_ref, meta_ref, h1_ref, h2_ref, w_ref, z_ref, *,
                     seq_blocks, n_prompt_blocks, n_batch, n_meta, dec_seq, n_sample_rows):
    i = pl.program_id(0)
    u = u_ref[...]
    rows = u.shape[0]
    r = lax.broadcasted_iota(jnp.int32, (rows, 1), 0)

    halo = halo_ref[...]
    prev2, prev1 = halo[V7X_SUBLANES - 2:V7X_SUBLANES - 1], halo[V7X_SUBLANES - 1:V7X_SUBLANES]
    meta = meta_ref[...]
    for b in range(n_batch):
        starts_batch = i == b * seq_blocks
        prev2 = jnp.where(starts_batch, meta[(b + 1) * n_meta - 2:(b + 1) * n_meta - 1], prev2)
        prev1 = jnp.where(starts_batch, meta[(b + 1) * n_meta - 1:(b + 1) * n_meta], prev1)

    p1 = jnp.where(r == 0, prev1, pltpu.roll(u, 1, axis=0))
    p2 = jnp.where(r == 0, prev2, jnp.where(r == 1, prev1, pltpu.roll(u, 2, axis=0)))

    small = i == n_prompt_blocks
    n_meta_rows = n_batch * n_meta
    in_meta = r < n_meta_rows
    in_sample = jnp.logical_and(r >= n_meta_rows, r < n_meta_rows + n_sample_rows)
    pos = jnp.where(in_meta, r % n_meta, (r - n_meta_rows) % dec_seq)
    restart = jnp.logical_or(in_meta, in_sample)
    m1 = jnp.logical_and(small, jnp.logical_and(restart, pos == 0))
    m2 = jnp.logical_and(small, jnp.logical_and(restart, pos <= 1))
    p1 = jnp.where(m1, h1_ref[...], p1)
    p2 = jnp.where(m2, h2_ref[...], p2)

    w = w_ref[...]
    conv = w[0:1] * p2 + w[1:2] * p1 + w[2:3] * u
    z_ref[...] = (bg_ref[...] * conv).astype(BF16)


def _conv_gate(u, bg, hist1, hist2, w_conv, layer, *, seq, n_batch, n_meta, dec_seq, n_sample_rows):
    m, d = u.shape
    rows = SMALL_ROWS
    n_prompt_blocks = n_batch * seq // rows
    assert m == (n_prompt_blocks + 1) * rows and seq % rows == 0
    halo_per_block = rows // V7X_SUBLANES
    kern = functools.partial(_convgate_kernel, seq_blocks=seq // rows, n_prompt_blocks=n_prompt_blocks,
                             n_batch=n_batch, n_meta=n_meta, dec_seq=dec_seq, n_sample_rows=n_sample_rows)
    return pl.pallas_call(
        kern,
        grid=(m // rows,),
        in_specs=[pl.BlockSpec((rows, d), lambda i: (i, 0)),
                  pl.BlockSpec((rows, d), lambda i: (i, 0)),
                  pl.BlockSpec((V7X_SUBLANES, d), lambda i: (jnp.maximum(i * halo_per_block - 1, 0), 0)),
                  pl.BlockSpec((rows, d), lambda i: (n_prompt_blocks, 0)),
                  pl.BlockSpec((rows, d), lambda i: (0, 0)),
                  pl.BlockSpec((rows, d), lambda i: (0, 0)),
                  pl.BlockSpec((None, 3, d), lambda i: (layer, 0, 0))],
        out_specs=pl.BlockSpec((rows, d), lambda i: (i, 0)),
        out_shape=jax.ShapeDtypeStruct((m, d), BF16),
        compiler_params=_cparams(("arbitrary",)),
        name="conv_gate",
    )(u, bg, u, u, hist1, hist2, w_conv)


def _residual_layer_norm(x_ref, o_ref, g_ref, b_ref, alpha, rows_chunk):
    g = g_ref[...]
    b = b_ref[...]

    def body(c, carry):
        rs = pl.ds(pl.multiple_of(c * rows_chunk, rows_chunk), rows_chunk)
        y = alpha * x_ref[rs, :] + o_ref[rs, :]
        mu = jnp.mean(y, axis=-1, keepdims=True)
        yc = y - mu
        var = jnp.mean(yc * yc, axis=-1, keepdims=True)
        o_ref[rs, :] = yc * lax.rsqrt(var + LN_EPS) * g + b
        return carry

    lax.fori_loop(0, x_ref.shape[0] // rows_chunk, body, 0)


def _outproj_kernel(a_ref, *rest, alpha, rows_chunk, n_main_last):
    small_ref = rest[0] if n_main_last is not None else None
    w_ref, x_ref, g_ref, b_ref, o_ref = rest[-5:]
    i = pl.program_id(0)
    k = pl.program_id(1)
    last_tile = pl.num_programs(0) - 1

    @pl.when(k == 0)
    def _():
        o_ref[...] = jnp.zeros_like(o_ref)

    w = w_ref[...].astype(BF16)
    if small_ref is None:
        o_ref[...] += jnp.dot(a_ref[...], w, preferred_element_type=F32)
    else:
        @pl.when(i < last_tile)
        def _():
            o_ref[...] += jnp.dot(a_ref[...], w, preferred_element_type=F32)

        @pl.when(i == last_tile)
        def _():
            a = jnp.concatenate([a_ref[:n_main_last], small_ref[...]], axis=0)
            o_ref[...] += jnp.dot(a, w, preferred_element_type=F32)

    @pl.when(k == pl.num_programs(1) - 1)
    def _():
        _residual_layer_norm(x_ref, o_ref, g_ref, b_ref, alpha, rows_chunk)


def _outproj_ln(a16, a16_small, w, layer, x, g, b, alpha):
    m, d = x.shape
    tm = _pick_tile(m, OUTPROJ_ROWS, V7X_BF16_SUBLANES)
    tk = OUTPROJ_K
    rows_chunk = _pick_tile(tm, LN_ROWS, V7X_SUBLANES)
    if a16_small is None:
        assert a16.shape[0] == m
        n_main_last, small_args, small_specs = None, [], []
    else:
        n_small = a16_small.shape[0]
        assert a16.shape[0] + n_small == m and n_small <= tm
        n_main_last = tm - n_small
        small_args = [a16_small]
        small_specs = [pl.BlockSpec((n_small, tk), lambda i, k: (0, k))]
    return pl.pallas_call(
        functools.partial(_outproj_kernel, alpha=alpha, rows_chunk=rows_chunk, n_main_last=n_main_last),
        grid=(m // tm, d // tk),
        in_specs=[pl.BlockSpec((tm, tk), lambda i, k: (i, k))] + small_specs + [
                  pl.BlockSpec((None, tk, d), lambda i, k: (layer, k, 0)),
                  pl.BlockSpec((tm, d), lambda i, k: (i, 0)),
                  pl.BlockSpec((1, d), lambda i, k: (0, 0)),
                  pl.BlockSpec((1, d), lambda i, k: (0, 0))],
        out_specs=pl.BlockSpec((tm, d), lambda i, k: (i, 0)),
        out_shape=jax.ShapeDtypeStruct((m, d), F32),
        compiler_params=_cparams(("arbitrary", "arbitrary")),
        name="outproj_ln",
    )(a16, *small_args, w, x, g, b)


def _mlp_kernel(x_ref, wu_ref, wd_ref, g_ref, b_ref, o_ref, *rest, alpha, rows_chunk, n_main_last):
    small_ref = rest[0] if n_main_last is not None else None
    xb_ref = rest[-1]
    f = pl.program_id(1)

    @pl.when(f == 0)
    def _():
        xb_ref[...] = x_ref[...].astype(BF16)
        o_ref[...] = jnp.zeros_like(o_ref)

    h = jnp.dot(xb_ref[...], wu_ref[...].astype(BF16), preferred_element_type=F32)
    h = jnp.square(jnp.maximum(h, 0.0)).astype(BF16)
    o_ref[...] += jnp.dot(h, wd_ref[...].astype(BF16), preferred_element_type=F32)

    @pl.when(f == pl.num_programs(1) - 1)
    def _():
        _residual_layer_norm(x_ref, o_ref, g_ref, b_ref, alpha, rows_chunk)
        if small_ref is not None:
            @pl.when(pl.program_id(0) == pl.num_programs(0) - 1)
            def _():
                small_ref[...] = o_ref[n_main_last:, :]


def _mlp_ln(x, w_up, w_down, layer, g, b, alpha, n_main=None):
    m, d = x.shape
    d_ff = w_up.shape[-1]
    tm = _pick_tile(m, MLP_ROWS, V7X_BF16_SUBLANES)
    tf = MLP_FF
    rows_chunk = _pick_tile(tm, LN_ROWS, V7X_SUBLANES)
    out_specs = pl.BlockSpec((tm, d), lambda i, f: (i, 0))
    out_shape = jax.ShapeDtypeStruct((m, d), F32)
    n_main_last = None
    if n_main is not None:
        n_small = m - n_main
        assert 0 < n_small <= tm
        n_main_last = tm - n_small
        out_specs = [out_specs, pl.BlockSpec((n_small, d), lambda i, f: (0, 0))]
        out_shape = [jax.ShapeDtypeStruct((n_main, d), F32), jax.ShapeDtypeStruct((n_small, d), F32)]
    return pl.pallas_call(
        functools.partial(_mlp_kernel, alpha=alpha, rows_chunk=rows_chunk, n_main_last=n_main_last),
        grid=(m // tm, d_ff // tf),
        in_specs=[pl.BlockSpec((tm, d), lambda i, f: (i, 0)),
                  pl.BlockSpec((None, d, tf), lambda i, f: (layer, 0, f)),
                  pl.BlockSpec((None, tf, d), lambda i, f: (layer, f, 0)),
                  pl.BlockSpec((1, d), lambda i, f: (0, 0)),
                  pl.BlockSpec((1, d), lambda i, f: (0, 0))],
        out_specs=out_specs,
        out_shape=out_shape,
        scratch_shapes=[pltpu.VMEM((tm, d), BF16)],
        compiler_params=_cparams(("arbitrary", "arbitrary")),
        name="mlp_ln",
    )(x, w_up, w_down, g, b)


def _diff_lambda(lam_ref, lam_init):
    lp = lam_ref[...]
    a = jnp.sum(lp[0:1] * lp[1:2], axis=-1, keepdims=True)
    b = jnp.sum(lp[2:3] * lp[3:4], axis=-1, keepdims=True)
    return jnp.exp(a) - jnp.exp(b) + lam_init


def _sub_layer_norm(o, g_ref, lam_init):
    return o * lax.rsqrt(jnp.mean(o * o, axis=-1, keepdims=True) + SUBLN_EPS) * (g_ref[...] * (1.0 - lam_init))


def _dot_nt(a, b):
    return lax.dot_general(a, b, (((1,), (1,)), ((), ())), preferred_element_type=F32)


def _flash_kernel(q_ref, k_ref, v_ref, km_ref, vm_ref, qm_ref, bn_ref, bm_ref, bmm_ref, lam_ref, g_ref,
                  o_ref, om_ref, vt_sc, s0_sc, s1_sc, m_sc, acc_sc, *, head_dim, lam_init):
    qi = pl.program_id(2)
    tq = q_ref.shape[0]
    n_local, _, ck, _ = bn_ref.shape
    ratio = tq // ck
    n_chunks = k_ref.shape[0] // ck
    n_meta = km_ref.shape[0]
    vd = 2 * head_dim
    heads = range(n_local)
    lam = _diff_lambda(lam_ref, lam_init)
    cols = lambda hh: slice(hh * vd, (hh + 1) * vd)

    @pl.when(qi == 0)
    def _():
        extra = vt_sc.shape[2] - vd
        ones_rows = (lax.broadcasted_iota(jnp.int32, (extra, ck), 0) == 0).astype(BF16)
        for hh in heads:
            def body(c, carry):
                rs = pl.ds(pl.multiple_of(c * ck, ck), ck)
                vt_sc[hh, c, :vd] = v_ref[rs, cols(hh)].astype(F32).T.astype(BF16)
                vt_sc[hh, c, vd:] = ones_rows
                return carry

            lax.fori_loop(0, n_chunks, body, 0)
            vm_pad = jnp.concatenate([vm_ref[:, cols(hh)].astype(F32), jnp.zeros((ck - n_meta, vd), F32)], axis=0)
            vt_sc[hh, n_chunks, :vd] = vm_pad.T.astype(BF16)
            vt_sc[hh, n_chunks, vd:] = ones_rows

    def scores_t(hh, keys, bias_t):
        parts = [_dot_nt(keys[:, m * head_dim:(m + 1) * head_dim],
                         q_ref[:, hh * vd + m * head_dim:hh * vd + (m + 1) * head_dim]) for m in range(2)]
        if bias_t is not None:
            parts = [p + bias_t for p in parts]
        return jnp.concatenate(parts, axis=1)

    for hh in heads:
        s = scores_t(hh, km_ref[:, cols(hh)], jnp.where(qi == 0, bm_ref[hh], 0.0))
        m0 = jnp.max(s, axis=0, keepdims=True)
        p = jnp.exp2(s - m0)
        m_sc[hh] = m0
        p_pad = jnp.concatenate([p, jnp.zeros((V7X_LANES - n_meta, 2 * tq), F32)], axis=0).astype(BF16)
        acc_sc[hh] = jnp.dot(vt_sc[hh, n_chunks][:, :V7X_LANES], p_pad, preferred_element_type=F32)

    assert ratio == 1
    s_bufs = (s0_sc, s1_sc)

    def scores_into(c, tile, s_buf):
        for hh in heads:
            kb = k_ref[pl.ds(pl.multiple_of(c * ck, ck), ck), cols(hh)]
            s_buf[hh] = scores_t(hh, kb, None if tile is None else bn_ref[hh, tile])

    def update_from(c, s_buf):
        for hh in heads:
            s = s_buf[hh]
            m_old = m_sc[hh]
            m_new = jnp.maximum(m_old, jnp.max(s, axis=0, keepdims=True))
            alpha = jnp.exp2(m_old - m_new)
            p = jnp.exp2(s - m_new)
            acc_sc[hh] = alpha * acc_sc[hh] + jnp.dot(vt_sc[hh, c], p.astype(BF16), preferred_element_type=F32)
            m_sc[hh] = m_new

    def by_parity(c, fn):
        for par in range(2):
            @pl.when(jnp.bitwise_and(c, 1) == par)
            def _():
                fn(s_bufs[par], s_bufs[1 - par])

    def advance(c, next_tile):
        def fn(cur, nxt):
            scores_into(c + 1, next_tile, nxt)
            update_from(c, cur)
        by_parity(c, fn)

    @pl.when(qi == 0)
    def _():
        scores_into(0, 1, s_bufs[0])

    @pl.when(qi == 1)
    def _():
        scores_into(0, 0, s_bufs[0])

    @pl.when(qi >= 2)
    def _():
        scores_into(0, None, s_bufs[0])

        def far_body(c, carry):
            advance(c, None)
            return carry

        lax.fori_loop(0, qi - 2, far_body, 0)
        advance(qi - 2, 0)

    @pl.when(qi >= 1)
    def _():
        advance(qi - 1, 1)

    by_parity(qi, lambda cur, nxt: update_from(qi, cur))

    for hh in heads:
        acc = acc_sc[hh]
        on = acc[:vd] * (1.0 / acc[vd:vd + 1])
        o_t = on[:, :tq] - lam * on[:, tq:]
        o_t = o_t * lax.rsqrt(jnp.mean(o_t * o_t, axis=0, keepdims=True) + SUBLN_EPS)
        o_ref[:, cols(hh)] = (o_t.T * (g_ref[...] * (1.0 - lam_init))).astype(BF16)

    @pl.when(qi == 0)
    def _():
        for hh in heads:
            qm = qm_ref[:, cols(hh)]
            km = km_ref[:, cols(hh)]
            sm = jnp.concatenate([_dot_nt(qm[:, m * head_dim:(m + 1) * head_dim],
                                          km[:, m * head_dim:(m + 1) * head_dim]) + bmm_ref[hh]
                                  for m in range(2)], axis=0)
            pm = jnp.exp2(sm - jnp.max(sm, axis=-1, keepdims=True))
            onm = (jnp.dot(pm.astype(BF16), vm_ref[:, cols(hh)], preferred_element_type=F32)
                   * (1.0 / jnp.sum(pm, axis=-1, keepdims=True)))
            om_ref[:, cols(hh)] = _sub_layer_norm(onm[:n_meta] - lam * onm[n_meta:], g_ref, lam_init)


def _flash_prompt(qkv16, bias_near, bias_meta, bias_mm, lam_params, subln, layer, lam_init, *,
                  n_batch, seq, n_heads, head_dim, n_meta):
    m, d3 = qkv16.shape
    d = d3 // 3
    vd = 2 * head_dim
    tq = FLASH_TQ
    hp = FLASH_HEADS
    assert n_heads % hp == 0
    ng = n_heads // hp
    nq = seq // tq
    mp = n_batch * seq
    meta_blk = mp // n_meta
    kern = functools.partial(_flash_kernel, head_dim=head_dim, lam_init=lam_init)
    return pl.pallas_call(
        kern,
        grid=(n_batch, ng, nq),
        in_specs=[pl.BlockSpec((tq, hp * vd), lambda b, h, qi: (b * nq + qi, h)),
                  pl.BlockSpec((seq, hp * vd), lambda b, h, qi: (b, ng + h)),
                  pl.BlockSpec((seq, hp * vd), lambda b, h, qi: (b, 2 * ng + h)),
                  pl.BlockSpec((n_meta, hp * vd), lambda b, h, qi: (meta_blk + b, ng + h)),
                  pl.BlockSpec((n_meta, hp * vd), lambda b, h, qi: (meta_blk + b, 2 * ng + h)),
                  pl.BlockSpec((n_meta, hp * vd), lambda b, h, qi: (meta_blk + b, h)),
                  pl.BlockSpec((hp,) + bias_near.shape[1:], lambda b, h, qi: (h, 0, 0, 0)),
                  pl.BlockSpec((hp,) + bias_meta.shape[1:], lambda b, h, qi: (h, 0, 0)),
                  pl.BlockSpec((hp,) + bias_mm.shape[1:], lambda b, h, qi: (h, 0, 0)),
                  pl.BlockSpec((None, 4, head_dim), lambda b, h, qi: (layer, 0, 0)),
                  pl.BlockSpec((None, 1, vd), lambda b, h, qi: (layer, 0, 0))],
        out_specs=[pl.BlockSpec((tq, hp * vd), lambda b, h, qi: (b * nq + qi, h)),
                   pl.BlockSpec((n_meta, hp * vd), lambda b, h, qi: (b, h))],
        out_shape=[jax.ShapeDtypeStruct((mp, d), BF16),
                   jax.ShapeDtypeStruct((n_batch * n_meta, d), F32)],
        scratch_shapes=[pltpu.VMEM((hp, seq // FLASH_CK + 1, vd + FLASH_SUM_ROWS, FLASH_CK), BF16),
                        pltpu.VMEM((hp, FLASH_CK, 2 * tq), F32), pltpu.VMEM((hp, FLASH_CK, 2 * tq), F32),
                        pltpu.VMEM((hp, 1, 2 * tq), F32),
                        pltpu.VMEM((hp, vd + FLASH_SUM_ROWS, 2 * tq), F32)],
        compiler_params=_cparams(("arbitrary", "arbitrary", "arbitrary")),
        name="flash_prompt",
    )(qkv16, qkv16, qkv16, qkv16, qkv16, qkv16, bias_near, bias_meta, bias_mm, lam_params, subln)


def _decode_kernel(pt_ref, q_ref, kn_ref, vn_ref, *rest, n_heads, head_dim, n_pages_step, lam_init):
    kp_refs = rest[:n_pages_step]
    vp_refs = rest[n_pages_step:2 * n_pages_step]
    bd_ref, bnw_ref, lam_ref, g_ref, o_ref, qc_sc, hm_sc, m_sc, l_sc, acc_sc = rest[2 * n_pages_step:]
    st = pl.program_id(1)
    last = pl.num_programs(1) - 1
    dec_seq, d = q_ref.shape
    vd = 2 * head_dim
    map_rows = n_heads * dec_seq
    n_groups = 2 * n_heads

    @pl.when(st == 0)
    def _():
        q = q_ref[...]
        for mp in range(2):
            qc_sc[mp] = jnp.concatenate(
                [q[:, (2 * h + mp) * head_dim:(2 * h + mp + 1) * head_dim] for h in range(n_heads)],
                axis=0).astype(BF16)
        r = lax.broadcasted_iota(jnp.int32, hm_sc.shape, 0)
        c = lax.broadcasted_iota(jnp.int32, hm_sc.shape, 1)
        hm_sc[...] = jnp.where((r // dec_seq) % n_heads == c % n_heads, 0.0, MASK_VALUE)

        qt = jnp.concatenate([q] * n_groups, axis=0)
        grp = lax.broadcasted_iota(jnp.int32, qt.shape, 0) // dec_seq
        col = lax.broadcasted_iota(jnp.int32, qt.shape, 1) // head_dim
        qb = jnp.where(col == (grp % n_heads) * 2 + grp // n_heads, qt, 0.0).astype(BF16)
        pad = jnp.zeros((dec_seq, d), F32)
        kn = jnp.concatenate([kn_ref[...], pad], axis=0).astype(BF16)
        vn = jnp.concatenate([vn_ref[...], pad], axis=0).astype(BF16)
        s = _dot_nt(qb, kn) + bnw_ref[...]
        m0 = jnp.max(s, axis=-1, keepdims=True)
        p = jnp.exp2(s - m0)
        m_sc[...] = m0
        l_sc[...] = jnp.sum(p, axis=-1, keepdims=True)
        full = jnp.dot(p.astype(BF16), vn, preferred_element_type=F32)
        acc_sc[...] = jnp.concatenate(
            [full[g * dec_seq:(g + 1) * dec_seq, (g % n_heads) * vd:(g % n_heads + 1) * vd]
             for g in range(n_groups)], axis=0)

    th_rows = vp_refs[0].shape[0]
    lanes = DECODE_GROUP * th_rows
    head_mask = jnp.tile(hm_sc[...], (1, lanes // hm_sc.shape[1]))

    def scores(g):
        pages = kp_refs[g * DECODE_GROUP:(g + 1) * DECODE_GROUP]
        s = jnp.concatenate(
            [_dot_nt(qc_sc[mp], jnp.concatenate([kp[pl.ds(mp, th_rows, stride=2), :] for kp in pages],
                                                axis=0).astype(BF16)) for mp in range(2)], axis=0)
        return s + head_mask + jnp.where(st == last, bd_ref[:, g * lanes:(g + 1) * lanes], 0.0)

    n_groups_step = n_pages_step // DECODE_GROUP
    group_scores = [scores(g) for g in range(n_groups_step)]
    for g in range(n_groups_step):
        s = group_scores[g]
        v16 = jnp.concatenate([vp[...].astype(BF16) for vp in vp_refs[g * DECODE_GROUP:(g + 1) * DECODE_GROUP]],
                              axis=0)
        m_old = m_sc[...]
        m_new = jnp.maximum(m_old, jnp.max(s, axis=-1, keepdims=True))
        alpha = jnp.exp2(m_old - m_new)
        p = jnp.exp2(s - m_new)
        l_sc[...] = alpha * l_sc[...] + jnp.sum(p, axis=-1, keepdims=True)
        acc_sc[...] = alpha * acc_sc[...] + jnp.dot(p.astype(BF16), v16, preferred_element_type=F32)
        m_sc[...] = m_new

    @pl.when(st == last)
    def _():
        lam = _diff_lambda(lam_ref, lam_init)
        on = acc_sc[...] * (1.0 / l_sc[...])
        for h in range(n_heads):
            oh = (on[h * dec_seq:(h + 1) * dec_seq]
                  - lam * on[map_rows + h * dec_seq:map_rows + (h + 1) * dec_seq])
            o_ref[:, h * vd:(h + 1) * vd] = _sub_layer_norm(oh, g_ref, lam_init)


def _decode_attn(q32, kv32, cache_k, cache_v, page_table, bias_past, bias_new, lam_params, subln, layer,
                 lam_init, *, new_row0, n_heads, head_dim):
    n_rows, d = q32.shape
    dec_batch, n_pages = page_table.shape
    dec_seq = n_rows // dec_batch
    n_attn, n_pool, page = cache_k.shape[:3]
    ck = cache_k.reshape(n_attn, n_pool, page * 2 * n_heads, head_dim)
    cv = cache_v.reshape(n_attn, n_pool, page * n_heads, 2 * head_dim)
    nps = DECODE_PAGES
    assert n_pages % nps == 0 and nps % DECODE_GROUP == 0 and new_row0 % dec_seq == 0
    n_steps = n_pages // nps
    vd = 2 * head_dim
    rows = 2 * n_heads * dec_seq
    new_blk = new_row0 // dec_seq

    def page_spec(t, arr):
        return pl.BlockSpec((None, None) + arr.shape[2:],
                            lambda bd, st, pt: (layer, pt[bd * n_pages + st * nps + t], 0, 0))

    grid_spec = pltpu.PrefetchScalarGridSpec(
        num_scalar_prefetch=1,
        grid=(dec_batch, n_steps),
        in_specs=[pl.BlockSpec((dec_seq, d), lambda bd, st, pt: (bd, 0)),
                  pl.BlockSpec((dec_seq, d), lambda bd, st, pt: (new_blk + bd, 0)),
                  pl.BlockSpec((dec_seq, d), lambda bd, st, pt: (new_blk + bd, 1))]
                 + [page_spec(t, ck) for t in range(nps)] + [page_spec(t, cv) for t in range(nps)]
                 + [pl.BlockSpec(bias_past.shape, lambda bd, st, pt: (0, 0)),
                    pl.BlockSpec(bias_new.shape, lambda bd, st, pt: (0, 0)),
                    pl.BlockSpec((None, 4, head_dim), lambda bd, st, pt: (layer, 0, 0)),
                    pl.BlockSpec((None, 1, vd), lambda bd, st, pt: (layer, 0, 0))],
        out_specs=pl.BlockSpec((dec_seq, d), lambda bd, st, pt: (bd, 0)),
        scratch_shapes=[pltpu.VMEM((2, rows // 2, head_dim), BF16), pltpu.VMEM((rows, V7X_LANES), F32),
                        pltpu.VMEM((rows, 1), F32), pltpu.VMEM((rows, 1), F32),
                        pltpu.VMEM((rows, vd), F32)],
    )
    kern = functools.partial(_decode_kernel, n_heads=n_heads, head_dim=head_dim, n_pages_step=nps,
                             lam_init=lam_init)
    return pl.pallas_call(
        kern,
        grid_spec=grid_spec,
        out_shape=jax.ShapeDtypeStruct((n_rows, d), F32),
        compiler_params=_cparams(("arbitrary", "arbitrary")),
        name="decode_attn",
    )(page_table.reshape(-1), q32, kv32, kv32, *([ck] * nps), *([cv] * nps),
      bias_past, bias_new, lam_params, subln)


def kernel(x_prompt, x_sample, cache_k, cache_v, state_conv, page_table, meta_tokens, rel_bias,
           attn_w_qkv, attn_lambda, attn_subln, attn_w_o, conv_w_in, conv_w, conv_w_out,
           mlp_w_up, mlp_w_down, ln_mix_g, ln_mix_b, ln_mlp_g, ln_mlp_b):
    n_batch, seq, d = x_prompt.shape
    dec_batch, dec_seq, _ = x_sample.shape
    n_meta = meta_tokens.shape[0]
    n_buckets, n_heads = rel_bias.shape
    head_dim = cache_k.shape[-1]
    vd = cache_v.shape[-1]
    page = cache_k.shape[2]
    depth = mlp_w_up.shape[0]
    assert d == n_heads * vd and vd == 2 * head_dim
    alpha = (2 * depth) ** 0.25

    mp = n_batch * seq
    n_meta_rows = n_batch * n_meta
    n_sample_rows = dec_batch * dec_seq
    assert n_meta_rows + n_sample_rows <= SMALL_ROWS and seq % FLASH_TQ == 0
    assert n_meta % V7X_SUBLANES == 0 and dec_seq % V7X_SUBLANES == 0
    meta0, samp0 = mp, mp + n_meta_rows
    mt = mp + SMALL_ROWS

    x = jnp.concatenate(
        [x_prompt.reshape(mp, d)] + [meta_tokens.astype(F32)] * n_batch
        + [x_sample.reshape(n_sample_rows, d), jnp.zeros((mt - samp0 - n_sample_rows, d), F32)], axis=0)

    far = _far_distance(n_buckets)
    tq, ck, nps = FLASH_TQ, FLASH_CK, DECODE_PAGES
    ratio = tq // ck
    assert ck + 1 >= far and n_meta + tq - (n_meta - 1) >= far and (nps - 1) * page + 1 >= far
    qq = np.arange(tq)[None, :]
    near_idx = np.stack([_bucket_tile(qq - (t - 1) * ck - np.arange(ck)[:, None], n_buckets)
                         for t in range(ratio + 1)])
    meta_idx = _bucket_tile(n_meta + qq - np.arange(n_meta)[:, None], n_buckets)
    mm_idx = _bucket_tile(np.arange(n_meta)[:, None] - np.arange(n_meta)[None, :], n_buckets)
    step_of_row = (np.arange(2 * dec_seq) % dec_seq)[:, None]
    past_token = (np.arange(nps * page * n_heads) // n_heads)[None, :]
    past_idx = _bucket_tile(nps * page + step_of_row - past_token, n_buckets)
    new_cols = np.arange(2 * dec_seq)[None, :]
    new_idx = np.where(new_cols < dec_seq, _bucket_tile(step_of_row - new_cols, n_buckets), -1).astype(np.int32)

    bias_near = _bias_expand(near_idx.reshape((ratio + 1) * ck, tq), rel_bias).reshape(
        n_heads, ratio + 1, ck, tq)
    bias_meta = _bias_expand(meta_idx, rel_bias)
    bias_mm = _bias_expand(mm_idx, rel_bias)

    def decode_rows(tile):
        lanes = tile.shape[-1]
        return tile.reshape(n_heads, 2, dec_seq, lanes).transpose(1, 0, 2, 3).reshape(2 * n_heads * dec_seq, lanes)

    bias_past = decode_rows(_bias_expand(past_idx, rel_bias))
    bias_new = decode_rows(_bias_expand(new_idx, rel_bias))

    def conv_history(state):
        h1 = jnp.zeros((dec_batch, dec_seq, d), F32).at[:, 0].set(state[:, 1])
        h2 = jnp.zeros((dec_batch, dec_seq, d), F32).at[:, 0].set(state[:, 0]).at[:, 1].set(state[:, 1])
        pad_lo = jnp.zeros((n_meta_rows, d), F32)
        pad_hi = jnp.zeros((SMALL_ROWS - n_meta_rows - n_sample_rows, d), F32)
        cat = lambda h: jnp.concatenate([pad_lo, h.reshape(n_sample_rows, d), pad_hi], axis=0)
        return cat(h1), cat(h2)

    subln3 = attn_subln.reshape(attn_subln.shape[0], 1, vd)
    kv_layers, c_p, c_s = [], [], []
    for i in range(depth):
        j = i // N_MIXERS
        g_mix, b_mix = ln_mix_g[i][None], ln_mix_b[i][None]
        if i % N_MIXERS == 0:
            lam0 = _lambda_init(i)
            qkv16, kv32 = _qkv_proj(x, attn_w_qkv, j, head_dim ** -0.5 * LOG2_E)
            o16, o_meta = _flash_prompt(qkv16, bias_near, bias_meta, bias_mm, attn_lambda, subln3, j, lam0,
                                        n_batch=n_batch, seq=seq, n_heads=n_heads, head_dim=head_dim,
                                        n_meta=n_meta)
            q32 = qkv16[samp0:samp0 + n_sample_rows, :d].astype(F32)
            o_samp = _decode_attn(q32, kv32, cache_k, cache_v, page_table, bias_past, bias_new, attn_lambda,
                                  subln3, j, lam0, new_row0=samp0, n_heads=n_heads, head_dim=head_dim)
            o_small = jnp.concatenate(
                [o_meta, o_samp, jnp.zeros((mt - samp0 - n_sample_rows, d), F32)], axis=0).astype(BF16)
            x = _outproj_ln(o16, o_small, attn_w_o, j, x, g_mix, b_mix, alpha)
            kv_layers.append(kv32)
        else:
            bg, u = _conv_in(x, conv_w_in, j)
            hist1, hist2 = conv_history(state_conv[j])
            z16 = _conv_gate(u, bg, hist1, hist2, conv_w, j, seq=seq, n_batch=n_batch, n_meta=n_meta,
                             dec_seq=dec_seq, n_sample_rows=n_sample_rows)
            x = _outproj_ln(z16, None, conv_w_out, j, x, g_mix, b_mix, alpha)
            c_p.append(jnp.stack([u[(b + 1) * seq - 2:(b + 1) * seq] for b in range(n_batch)]))
            c_s.append(u[samp0:samp0 + n_sample_rows].reshape(dec_batch, dec_seq, d)[:, dec_seq - 2:])
        last = i == depth - 1
        x = _mlp_ln(x, mlp_w_up, mlp_w_down, i, ln_mlp_g[i][None], ln_mlp_b[i][None], alpha,
                    n_main=mp if last else None)

    y_main, y_small = x
    y_prompt = y_main.reshape(n_batch, seq, d)
    y_sample = y_small[n_meta_rows:n_meta_rows + n_sample_rows].reshape(dec_batch, dec_seq, d)

    def new_cache(col0, tail):
        cols = slice(col0, col0 + d)
        pieces = []
        for kv in kv_layers:
            for b in range(n_batch):
                pieces.append(kv[meta0 + b * n_meta:meta0 + (b + 1) * n_meta, cols])
                pieces.append(kv[b * seq:(b + 1) * seq, cols])
        prompt = jnp.concatenate(pieces, axis=0).reshape((len(kv_layers), n_batch, n_meta + seq) + tail)
        sample = jnp.stack([kv[samp0:samp0 + n_sample_rows, cols].reshape((dec_batch, dec_seq) + tail)
                            for kv in kv_layers])
        return prompt, sample

    k_p, k_s = new_cache(0, (n_heads, 2, head_dim))
    v_p, v_s = new_cache(d, (n_heads, vd))
    return (y_prompt, y_sample, k_p, v_p, jnp.stack(c_p), k_s, v_s, jnp.stack(c_s))
```

```python
import functools
import math

import numpy as np
import jax
import jax.numpy as jnp
from jax import lax
from jax.experimental import pallas as pl
from jax.experimental.pallas import tpu as pltpu

F32 = jnp.float32
BF16 = jnp.bfloat16

MAX_DISTANCE = 128
LN_EPS = 1e-5
SUBLN_EPS = 1e-5
N_MIXERS = 2
MASK_VALUE = -1e30
LOG2_E = math.log2(math.e)

V7X_LANES = 128
V7X_SUBLANES = 8
V7X_VMEM_LIMIT_BYTES = 60 * 1024 * 1024

SMALL_ROWS = 128
FLASH_TQ = 512
FLASH_CK = 512
FLASH_HEADS = 2
FLASH_SUM_ROWS = 16
DECODE_PAGES = 8
DECODE_GROUP = 2

V7X_BF16_SUBLANES = 16
PROJ_ROWS = 832
OUTPROJ_ROWS = 1040
MLP_ROWS = 832
QKV_COLS = 1024
CONV_IN_COLS = 512
OUTPROJ_K = 512
MLP_FF = 512
LN_ROWS = 208


def _lambda_init(layer):
    return 0.8 - 0.6 * math.exp(-0.3 * layer)


def _pick_tile(total, target, multiple):
    best = None
    for t in range(multiple, min(total, target) + 1, multiple):
        if total % t == 0:
            best = t
    assert best is not None, (total, target, multiple)
    return best


def _cparams(sem):
    return pltpu.CompilerParams(dimension_semantics=sem, vmem_limit_bytes=V7X_VMEM_LIMIT_BYTES)


def _bucket_of_distance(n, n_buckets):
    n = np.asarray(n, np.int64)
    max_exact = n_buckets // 2
    nf = np.maximum(n, 1).astype(np.float64)
    large = max_exact + (np.log(nf / max_exact) / math.log(MAX_DISTANCE / max_exact)
                         * (n_buckets - max_exact)).astype(np.int64)
    large = np.minimum(large, n_buckets - 1)
    return np.where(n < max_exact, n, large).astype(np.int32)


def _far_distance(n_buckets):
    tab = _bucket_of_distance(np.arange(4 * MAX_DISTANCE), n_buckets)
    not_last = np.nonzero(tab != n_buckets - 1)[0]
    return int(not_last[-1]) + 1


def _bucket_tile(dist, n_buckets):
    dist = np.asarray(dist, np.int64)
    return np.where(dist < 0, -1, _bucket_of_distance(np.maximum(dist, 0), n_buckets)).astype(np.int32)


def _bias_expand_kernel(rb_ref, idx_ref, o_ref, *, n_buckets):
    h = pl.program_id(0)
    idx = idx_ref[...]
    far = rb_ref[n_buckets - 1, h]
    out = jnp.zeros(idx.shape, F32)
    for b in range(n_buckets - 1):
        out = jnp.where(idx == b, (rb_ref[b, h] - far) * LOG2_E, out)
    o_ref[...] = jnp.where(idx < 0, MASK_VALUE, out)


def _bias_expand(idx_np, rel_bias):
    rows, cols = idx_np.shape
    n_buckets, n_heads = rel_bias.shape
    return pl.pallas_call(
        functools.partial(_bias_expand_kernel, n_buckets=n_buckets),
        grid=(n_heads,),
        in_specs=[pl.BlockSpec(memory_space=pltpu.SMEM),
                  pl.BlockSpec((rows, cols), lambda h: (0, 0))],
        out_specs=pl.BlockSpec((None, rows, cols), lambda h: (h, 0, 0)),
        out_shape=jax.ShapeDtypeStruct((n_heads, rows, cols), F32),
        compiler_params=_cparams(("arbitrary",)),
        name="bias_expand",
    )(rel_bias, jnp.asarray(idx_np))


def _qkv_kernel(x_ref, w_ref, o16_ref, kv32_ref, xb_ref, *, n_q_tiles, q_scale):
    j = pl.program_id(1)

    @pl.when(j == 0)
    def _():
        xb_ref[...] = x_ref[...].astype(BF16)

    acc = jnp.dot(xb_ref[...], w_ref[...].astype(BF16), preferred_element_type=F32)
    scale = jnp.where(j < n_q_tiles, q_scale, 1.0).astype(F32)
    o16_ref[...] = (acc * scale).astype(BF16)

    @pl.when(j >= n_q_tiles)
    def _():
        kv32_ref[...] = acc


def _qkv_proj(x, w_qkv, layer, q_scale):
    m, d = x.shape
    tm = _pick_tile(m, PROJ_ROWS, V7X_BF16_SUBLANES)
    tn = QKV_COLS
    n_q_tiles = d // tn
    return pl.pallas_call(
        functools.partial(_qkv_kernel, n_q_tiles=n_q_tiles, q_scale=q_scale),
        grid=(m // tm, 3 * d // tn),
        in_specs=[pl.BlockSpec((tm, d), lambda i, j: (i, 0)),
                  pl.BlockSpec((None, d, tn), lambda i, j: (layer, 0, j))],
        out_specs=[pl.BlockSpec((tm, tn), lambda i, j: (i, j)),
                   pl.BlockSpec((tm, tn), lambda i, j: (i, jnp.maximum(j - n_q_tiles, 0)))],
        out_shape=[jax.ShapeDtypeStruct((m, 3 * d), BF16),
                   jax.ShapeDtypeStruct((m, 2 * d), F32)],
        scratch_shapes=[pltpu.VMEM((tm, d), BF16)],
        compiler_params=_cparams(("arbitrary", "arbitrary")),
        name="qkv_proj",
    )(x, w_qkv)


def _convin_kernel(x_ref, wb_ref, wc_ref, wh_ref, bg_ref, u_ref, xb_ref):
    @pl.when(pl.program_id(1) == 0)
    def _():
        xb_ref[...] = x_ref[...].astype(BF16)

    xb = xb_ref[...]
    bg_ref[...] = jnp.dot(xb, wb_ref[...].astype(BF16), preferred_element_type=F32)
    cg = jnp.dot(xb, wc_ref[...].astype(BF16), preferred_element_type=F32)
    hh = jnp.dot(xb, wh_ref[...].astype(BF16), preferred_element_type=F32)
    u_ref[...] = cg * hh


def _conv_in(x, w_in, layer):
    m, d = x.shape
    tm = _pick_tile(m, PROJ_ROWS, V7X_BF16_SUBLANES)
    tn = CONV_IN_COLS
    nt = d // tn
    wspec = lambda part: pl.BlockSpec((None, d, tn), lambda i, j: (layer, 0, part * nt + j))
    return pl.pallas_call(
        _convin_kernel,
        grid=(m // tm, nt),
        in_specs=[pl.BlockSpec((tm, d), lambda i, j: (i, 0)), wspec(0), wspec(1), wspec(2)],
        out_specs=[pl.BlockSpec((tm, tn), lambda i, j: (i, j)),
                   pl.BlockSpec((tm, tn), lambda i, j: (i, j))],
        out_shape=[jax.ShapeDtypeStruct((m, d), F32), jax.ShapeDtypeStruct((m, d), F32)],
        scratch_shapes=[pltpu.VMEM((tm, d), BF16)],
        compiler_params=_cparams(("arbitrary", "arbitrary")),
        name="conv_in",
    )(x, w_in, w_in, w_in)


def _convgate_kernel(u_ref, bg_ref, halo_ref, meta_ref, h1_ref, h2_ref, w_ref, z_ref, *,
                     seq_blocks, n_prompt_blocks, n_batch, n_meta, dec_seq, n_sample_rows):
    i = pl.program_id(0)
    u = u_ref[...]
    rows = u.shape[0]
    r = lax.broadcasted_iota(jnp.int32, (rows, 1), 0)

    halo = halo_ref[...]
    prev2, prev1 = halo[V7X_SUBLANES - 2:V7X_SUBLANES - 1], halo[V7X_SUBLANES - 1:V7X_SUBLANES]
    meta = meta_ref[...]
    for b in range(n_batch):
        starts_batch = i == b * seq_blocks
        prev2 = jnp.where(starts_batch, meta[(b + 1) * n_meta - 2:(b + 1) * n_meta - 1], prev2)
        prev1 = jnp.where(starts_batch, meta[(b + 1) * n_meta - 1:(b + 1) * n_meta], prev1)

    p1 = jnp.where(r == 0, prev1, pltpu.roll(u, 1, axis=0))
    p2 = jnp.where(r == 0, prev2, jnp.where(r == 1, prev1, pltpu.roll(u, 2, axis=0)))

    small = i == n_prompt_blocks
    n_meta_rows = n_batch * n_meta
    in_meta = r < n_meta_rows
    in_sample = jnp.logical_and(r >= n_meta_rows, r < n_meta_rows + n_sample_rows)
    pos = jnp.where(in_meta, r % n_meta, (r - n_meta_rows) % dec_seq)
    restart = jnp.logical_or(in_meta, in_sample)
    m1 = jnp.logical_and(small, jnp.logical_and(restart, pos == 0))
    m2 = jnp.logical_and(small, jnp.logical_and(restart, pos <= 1))
    p1 = jnp.where(m1, h1_ref[...], p1)
    p2 = jnp.where(m2, h2_ref[...], p2)

    w = w_ref[...]
    conv = w[0:1] * p2 + w[1:2] * p1 + w[2:3] * u
    z_ref[...] = (bg_ref[...] * conv).astype(BF16)


def _conv_gate(u, bg, hist1, hist2, w_conv, layer, *, seq, n_batch, n_meta, dec_seq, n_sample_rows):
    m, d = u.shape
    rows = SMALL_ROWS
    n_prompt_blocks = n_batch * seq // rows
    assert m == (n_prompt_blocks + 1) * rows and seq % rows == 0
    halo_per_block = rows // V7X_SUBLANES
    kern = functools.partial(_convgate_kernel, seq_blocks=seq // rows, n_prompt_blocks=n_prompt_blocks,
                             n_batch=n_batch, n_meta=n_meta, dec_seq=dec_seq, n_sample_rows=n_sample_rows)
    return pl.pallas_call(
        kern,
        grid=(m // rows,),
        in_specs=[pl.BlockSpec((rows, d), lambda i: (i, 0)),
                  pl.BlockSpec((rows, d), lambda i: (i, 0)),
                  pl.BlockSpec((V7X_SUBLANES, d), lambda i: (jnp.maximum(i * halo_per_block - 1, 0), 0)),
                  pl.BlockSpec((rows, d), lambda i: (n_prompt_blocks, 0)),
                  pl.BlockSpec((rows, d), lambda i: (0, 0)),
                  pl.BlockSpec((rows, d), lambda i: (0, 0)),
                  pl.BlockSpec((None, 3, d), lambda i: (layer, 0, 0))],
        out_specs=pl.BlockSpec((rows, d), lambda i: (i, 0)),
        out_shape=jax.ShapeDtypeStruct((m, d), BF16),
        compiler_params=_cparams(("arbitrary",)),
        name="conv_gate",
    )(u, bg, u, u, hist1, hist2, w_conv)


def _residual_layer_norm(x_ref, o_ref, g_ref, b_ref, alpha, rows_chunk):
    g = g_ref[...]
    b = b_ref[...]

    def body(c, carry):
        rs = pl.ds(pl.multiple_of(c * rows_chunk, rows_chunk), rows_chunk)
        y = alpha * x_ref[rs, :] + o_ref[rs, :]
        mu = jnp.mean(y, axis=-1, keepdims=True)
        yc = y - mu
        var = jnp.mean(yc * yc, axis=-1, keepdims=True)
        o_ref[rs, :] = yc * lax.rsqrt(var + LN_EPS) * g + b
        return carry

    lax.fori_loop(0, x_ref.shape[0] // rows_chunk, body, 0)


def _outproj_kernel(a_ref, *rest, alpha, rows_chunk, n_main_last):
    small_ref = rest[0] if n_main_last is not None else None
    w_ref, x_ref, g_ref, b_ref, o_ref = rest[-5:]
    i = pl.program_id(0)
    k = pl.program_id(1)
    last_tile = pl.num_programs(0) - 1

    @pl.when(k == 0)
    def _():
        o_ref[...] = jnp.zeros_like(o_ref)

    w = w_ref[...].astype(BF16)
    if small_ref is None:
        o_ref[...] += jnp.dot(a_ref[...], w, preferred_element_type=F32)
    else:
        @pl.when(i < last_tile)
        def _():
            o_ref[...] += jnp.dot(a_ref[...], w, preferred_element_type=F32)

        @pl.when(i == last_tile)
        def _():
            a = jnp.concatenate([a_ref[:n_main_last], small_ref[...]], axis=0)
            o_ref[...] += jnp.dot(a, w, preferred_element_type=F32)

    @pl.when(k == pl.num_programs(1) - 1)
    def _():
        _residual_layer_norm(x_ref, o_ref, g_ref, b_ref, alpha, rows_chunk)


def _outproj_ln(a16, a16_small, w, layer, x, g, b, alpha):
    m, d = x.shape
    tm = _pick_tile(m, OUTPROJ_ROWS, V7X_BF16_SUBLANES)
    tk = OUTPROJ_K
    rows_chunk = _pick_tile(tm, LN_ROWS, V7X_SUBLANES)
    if a16_small is None:
        assert a16.shape[0] == m
        n_main_last, small_args, small_specs = None, [], []
    else:
        n_small = a16_small.shape[0]
        assert a16.shape[0] + n_small == m and n_small <= tm
        n_main_last = tm - n_small
        small_args = [a16_small]
        small_specs = [pl.BlockSpec((n_small, tk), lambda i, k: (0, k))]
    return pl.pallas_call(
        functools.partial(_outproj_kernel, alpha=alpha, rows_chunk=rows_chunk, n_main_last=n_main_last),
        grid=(m // tm, d // tk),
        in_specs=[pl.BlockSpec((tm, tk), lambda i, k: (i, k))] + small_specs + [
                  pl.BlockSpec((None, tk, d), lambda i, k: (layer, k, 0)),
                  pl.BlockSpec((tm, d), lambda i, k: (i, 0)),
                  pl.BlockSpec((1, d), lambda i, k: (0, 0)),
                  pl.BlockSpec((1, d), lambda i, k: (0, 0))],
        out_specs=pl.BlockSpec((tm, d), lambda i, k: (i, 0)),
        out_shape=jax.ShapeDtypeStruct((m, d), F32),
        compiler_params=_cparams(("arbitrary", "arbitrary")),
        name="outproj_ln",
    )(a16, *small_args, w, x, g, b)


def _mlp_kernel(x_ref, wu_ref, wd_ref, g_ref, b_ref, o_ref, *rest, alpha, rows_chunk, n_main_last):
    small_ref = rest[0] if n_main_last is not None else None
    xb_ref = rest[-1]
    f = pl.program_id(1)

    @pl.when(f == 0)
    def _():
        xb_ref[...] = x_ref[...].astype(BF16)
        o_ref[...] = jnp.zeros_like(o_ref)

    h = jnp.dot(xb_ref[...], wu_ref[...].astype(BF16), preferred_element_type=F32)
    h = jnp.square(jnp.maximum(h, 0.0)).astype(BF16)
    o_ref[...] += jnp.dot(h, wd_ref[...].astype(BF16), preferred_element_type=F32)

    @pl.when(f == pl.num_programs(1) - 1)
    def _():
        _residual_layer_norm(x_ref, o_ref, g_ref, b_ref, alpha, rows_chunk)
        if small_ref is not None:
            @pl.when(pl.program_id(0) == pl.num_programs(0) - 1)
            def _():
                small_ref[...] = o_ref[n_main_last:, :]


def _mlp_ln(x, w_up, w_down, layer, g, b, alpha, n_main=None):
    m, d = x.shape
    d_ff = w_up.shape[-1]
    tm = _pick_tile(m, MLP_ROWS, V7X_BF16_SUBLANES)
    tf = MLP_FF
    rows_chunk = _pick_tile(tm, LN_ROWS, V7X_SUBLANES)
    out_specs = pl.BlockSpec((tm, d), lambda i, f: (i, 0))
    out_shape = jax.ShapeDtypeStruct((m, d), F32)
    n_main_last = None
    if n_main is not None:
        n_small = m - n_main
        assert 0 < n_small <= tm
        n_main_last = tm - n_small
        out_specs = [out_specs, pl.BlockSpec((n_small, d), lambda i, f: (0, 0))]
        out_shape = [jax.ShapeDtypeStruct((n_main, d), F32), jax.ShapeDtypeStruct((n_small, d), F32)]
    return pl.pallas_call(
        functools.partial(_mlp_kernel, alpha=alpha, rows_chunk=rows_chunk, n_main_last=n_main_last),
        grid=(m // tm, d_ff // tf),
        in_specs=[pl.BlockSpec((tm, d), lambda i, f: (i, 0)),
                  pl.BlockSpec((None, d, tf), lambda i, f: (layer, 0, f)),
                  pl.BlockSpec((None, tf, d), lambda i, f: (layer, f, 0)),
                  pl.BlockSpec((1, d), lambda i, f: (0, 0)),
                  pl.BlockSpec((1, d), lambda i, f: (0, 0))],
        out_specs=out_specs,
        out_shape=out_shape,
        scratch_shapes=[pltpu.VMEM((tm, d), BF16)],
        compiler_params=_cparams(("arbitrary", "arbitrary")),
        name="mlp_ln",
    )(x, w_up, w_down, g, b)


def _diff_lambda(lam_ref, lam_init):
    lp = lam_ref[...]
    a = jnp.sum(lp[0:1] * lp[1:2], axis=-1, keepdims=True)
    b = jnp.sum(lp[2:3] * lp[3:4], axis=-1, keepdims=True)
    return jnp.exp(a) - jnp.exp(b) + lam_init


def _sub_layer_norm(o, g_ref, lam_init):
    return o * lax.rsqrt(jnp.mean(o * o, axis=-1, keepdims=True) + SUBLN_EPS) * (g_ref[...] * (1.0 - lam_init))


def _dot_nt(a, b):
    return lax.dot_general(a, b, (((1,), (1,)), ((), ())), preferred_element_type=F32)


def _flash_kernel(q_ref, k_ref, v_ref, km_ref, vm_ref, qm_ref, bn_ref, bm_ref, bmm_ref, lam_ref, g_ref,
                  o_ref, om_ref, vt_sc, s0_sc, s1_sc, m_sc, acc_sc, *, head_dim, lam_init):
    qi = pl.program_id(2)
    tq = q_ref.shape[0]
    n_local, _, ck, _ = bn_ref.shape
    ratio = tq // ck
    n_chunks = k_ref.shape[0] // ck
    n_meta = km_ref.shape[0]
    vd = 2 * head_dim
    heads = range(n_local)
    lam = _diff_lambda(lam_ref, lam_init)
    cols = lambda hh: slice(hh * vd, (hh + 1) * vd)

    @pl.when(qi == 0)
    def _():
        extra = vt_sc.shape[2] - vd
        ones_rows = (lax.broadcasted_iota(jnp.int32, (extra, ck), 0) == 0).astype(BF16)
        for hh in heads:
            def body(c, carry):
                rs = pl.ds(pl.multiple_of(c * ck, ck), ck)
                vt_sc[hh, c, :vd] = v_ref[rs, cols(hh)].astype(F32).T.astype(BF16)
                vt_sc[hh, c, vd:] = ones_rows
                return carry

            lax.fori_loop(0, n_chunks, body, 0)
            vm_pad = jnp.concatenate([vm_ref[:, cols(hh)].astype(F32), jnp.zeros((ck - n_meta, vd), F32)], axis=0)
            vt_sc[hh, n_chunks, :vd] = vm_pad.T.astype(BF16)
            vt_sc[hh, n_chunks, vd:] = ones_rows

    def scores_t(hh, keys, bias_t):
        parts = [_dot_nt(keys[:, m * head_dim:(m + 1) * head_dim],
                         q_ref[:, hh * vd + m * head_dim:hh * vd + (m + 1) * head_dim]) for m in range(2)]
        if bias_t is not None:
            parts = [p + bias_t for p in parts]
        return jnp.concatenate(parts, axis=1)

    for hh in heads:
        s = scores_t(hh, km_ref[:, cols(hh)], jnp.where(qi == 0, bm_ref[hh], 0.0))
        m0 = jnp.max(s, axis=0, keepdims=True)
        p = jnp.exp2(s - m0)
        m_sc[hh] = m0
        p_pad = jnp.concatenate([p, jnp.zeros((V7X_LANES - n_meta, 2 * tq), F32)], axis=0).astype(BF16)
        acc_sc[hh] = jnp.dot(vt_sc[hh, n_chunks][:, :V7X_LANES], p_pad, preferred_element_type=F32)

    assert ratio == 1
    s_bufs = (s0_sc, s1_sc)

    def scores_into(c, tile, s_buf):
        for hh in heads:
            kb = k_ref[pl.ds(pl.multiple_of(c * ck, ck), ck), cols(hh)]
            s_buf[hh] = scores_t(hh, kb, None if tile is None else bn_ref[hh, tile])

    def update_from(c, s_buf):
        for hh in heads:
            s = s_buf[hh]
            m_old = m_sc[hh]
            m_new = jnp.maximum(m_old, jnp.max(s, axis=0, keepdims=True))
            alpha = jnp.exp2(m_old - m_new)
            p = jnp.exp2(s - m_new)
            acc_sc[hh] = alpha * acc_sc[hh] + jnp.dot(vt_sc[hh, c], p.astype(BF16), preferred_element_type=F32)
            m_sc[hh] = m_new

    def by_parity(c, fn):
        for par in range(2):
            @pl.when(jnp.bitwise_and(c, 1) == par)
            def _():
                fn(s_bufs[par], s_bufs[1 - par])

    def advance(c, next_tile):
        def fn(cur, nxt):
            scores_into(c + 1, next_tile, nxt)
            update_from(c, cur)
        by_parity(c, fn)

    @pl.when(qi == 0)
    def _():
        scores_into(0, 1, s_bufs[0])

    @pl.when(qi == 1)
    def _():
        scores_into(0, 0, s_bufs[0])

    @pl.when(qi >= 2)
    def _():
        scores_into(0, None, s_bufs[0])

        def far_body(c, carry):
            advance(c, None)
            return carry

        lax.fori_loop(0, qi - 2, far_body, 0)
        advance(qi - 2, 0)

    @pl.when(qi >= 1)
    def _():
        advance(qi - 1, 1)

    by_parity(qi, lambda cur, nxt: update_from(qi, cur))

    for hh in heads:
        acc = acc_sc[hh]
        on = acc[:vd] * (1.0 / acc[vd:vd + 1])
        o_t = on[:, :tq] - lam * on[:, tq:]
        o_t = o_t * lax.rsqrt(jnp.mean(o_t * o_t, axis=0, keepdims=True) + SUBLN_EPS)
        o_ref[:, cols(hh)] = (o_t.T * (g_ref[...] * (1.0 - lam_init))).astype(BF16)

    @pl.when(qi == 0)
    def _():
        for hh in heads:
            qm = qm_ref[:, cols(hh)]
            km = km_ref[:, cols(hh)]
            sm = jnp.concatenate([_dot_nt(qm[:, m * head_dim:(m + 1) * head_dim],
                                          km[:, m * head_dim:(m + 1) * head_dim]) + bmm_ref[hh]
                                  for m in range(2)], axis=0)
            pm = jnp.exp2(sm - jnp.max(sm, axis=-1, keepdims=True))
            onm = (jnp.dot(pm.astype(BF16), vm_ref[:, cols(hh)], preferred_element_type=F32)
                   * (1.0 / jnp.sum(pm, axis=-1, keepdims=True)))
            om_ref[:, cols(hh)] = _sub_layer_norm(onm[:n_meta] - lam * onm[n_meta:], g_ref, lam_init)


def _flash_prompt(qkv16, bias_near, bias_meta, bias_mm, lam_params, subln, layer, lam_init, *,
                  n_batch, seq, n_heads, head_dim, n_meta):
    m, d3 = qkv16.shape
    d = d3 // 3
    vd = 2 * head_dim
    tq = FLASH_TQ
    hp = FLASH_HEADS
    assert n_heads % hp == 0
    ng = n_heads // hp
    nq = seq // tq
    mp = n_batch * seq
    meta_blk = mp // n_meta
    kern = functools.partial(_flash_kernel, head_dim=head_dim, lam_init=lam_init)
    return pl.pallas_call(
        kern,
        grid=(n_batch, ng, nq),
        in_specs=[pl.BlockSpec((tq, hp * vd), lambda b, h, qi: (b * nq + qi, h)),
                  pl.BlockSpec((seq, hp * vd), lambda b, h, qi: (b, ng + h)),
                  pl.BlockSpec((seq, hp * vd), lambda b, h, qi: (b, 2 * ng + h)),
                  pl.BlockSpec((n_meta, hp * vd), lambda b, h, qi: (meta_blk + b, ng + h)),
                  pl.BlockSpec((n_meta, hp * vd), lambda b, h, qi: (meta_blk + b, 2 * ng + h)),
                  pl.BlockSpec((n_meta, hp * vd), lambda b, h, qi: (meta_blk + b, h)),
                  pl.BlockSpec((hp,) + bias_near.shape[1:], lambda b, h, qi: (h, 0, 0, 0)),
                  pl.BlockSpec((hp,) + bias_meta.shape[1:], lambda b, h, qi: (h, 0, 0)),
                  pl.BlockSpec((hp,) + bias_mm.shape[1:], lambda b, h, qi: (h, 0, 0)),
                  pl.BlockSpec((None, 4, head_dim), lambda b, h, qi: (layer, 0, 0)),
                  pl.BlockSpec((None, 1, vd), lambda b, h, qi: (layer, 0, 0))],
        out_specs=[pl.BlockSpec((tq, hp * vd), lambda b, h, qi: (b * nq + qi, h)),
                   pl.BlockSpec((n_meta, hp * vd), lambda b, h, qi: (b, h))],
        out_shape=[jax.ShapeDtypeStruct((mp, d), BF16),
                   jax.ShapeDtypeStruct((n_batch * n_meta, d), F32)],
        scratch_shapes=[pltpu.VMEM((hp, seq // FLASH_CK + 1, vd + FLASH_SUM_ROWS, FLASH_CK), BF16),
                        pltpu.VMEM((hp, FLASH_CK, 2 * tq), F32), pltpu.VMEM((hp, FLASH_CK, 2 * tq), F32),
                        pltpu.VMEM((hp, 1, 2 * tq), F32),
                        pltpu.VMEM((hp, vd + FLASH_SUM_ROWS, 2 * tq), F32)],
        compiler_params=_cparams(("arbitrary", "arbitrary", "arbitrary")),
        name="flash_prompt",
    )(qkv16, qkv16, qkv16, qkv16, qkv16, qkv16, bias_near, bias_meta, bias_mm, lam_params, subln)


def _decode_kernel(pt_ref, q_ref, kn_ref, vn_ref, *rest, n_heads, head_dim, n_pages_step, lam_init):
    kp_refs = rest[:n_pages_step]
    vp_refs = rest[n_pages_step:2 * n_pages_step]
    bd_ref, bnw_ref, lam_ref, g_ref, o_ref, qc_sc, hm_sc, m_sc, l_sc, acc_sc = rest[2 * n_pages_step:]
    st = pl.program_id(1)
    last = pl.num_programs(1) - 1
    dec_seq, d = q_ref.shape
    vd = 2 * head_dim
    map_rows = n_heads * dec_seq
    n_groups = 2 * n_heads

    @pl.when(st == 0)
    def _():
        q = q_ref[...]
        for mp in range(2):
            qc_sc[mp] = jnp.concatenate(
                [q[:, (2 * h + mp) * head_dim:(2 * h + mp + 1) * head_dim] for h in range(n_heads)],
                axis=0).astype(BF16)
        r = lax.broadcasted_iota(jnp.int32, hm_sc.shape, 0)
        c = lax.broadcasted_iota(jnp.int32, hm_sc.shape, 1)
        hm_sc[...] = jnp.where((r // dec_seq) % n_heads == c % n_heads, 0.0, MASK_VALUE)

        qt = jnp.concatenate([q] * n_groups, axis=0)
        grp = lax.broadcasted_iota(jnp.int32, qt.shape, 0) // dec_seq
        col = lax.broadcasted_iota(jnp.int32, qt.shape, 1) // head_dim
        qb = jnp.where(col == (grp % n_heads) * 2 + grp // n_heads, qt, 0.0).astype(BF16)
        pad = jnp.zeros((dec_seq, d), F32)
        kn = jnp.concatenate([kn_ref[...], pad], axis=0).astype(BF16)
        vn = jnp.concatenate([vn_ref[...], pad], axis=0).astype(BF16)
        s = _dot_nt(qb, kn) + bnw_ref[...]
        m0 = jnp.max(s, axis=-1, keepdims=True)
        p = jnp.exp2(s - m0)
        m_sc[...] = m0
        l_sc[...] = jnp.sum(p, axis=-1, keepdims=True)
        full = jnp.dot(p.astype(BF16), vn, preferred_element_type=F32)
        acc_sc[...] = jnp.concatenate(
            [full[g * dec_seq:(g + 1) * dec_seq, (g % n_heads) * vd:(g % n_heads + 1) * vd]
             for g in range(n_groups)], axis=0)

    th_rows = vp_refs[0].shape[0]
    lanes = DECODE_GROUP * th_rows
    head_mask = jnp.tile(hm_sc[...], (1, lanes // hm_sc.shape[1]))

    def scores(g):
        pages = kp_refs[g * DECODE_GROUP:(g + 1) * DECODE_GROUP]
        s = jnp.concatenate(
            [_dot_nt(qc_sc[mp], jnp.concatenate([kp[pl.ds(mp, th_rows, stride=2), :] for kp in pages],
                                                axis=0).astype(BF16)) for mp in range(2)], axis=0)
        return s + head_mask + jnp.where(st == last, bd_ref[:, g * lanes:(g + 1) * lanes], 0.0)

    n_groups_step = n_pages_step // DECODE_GROUP
    group_scores = [scores(g) for g in range(n_groups_step)]
    for g in range(n_groups_step):
        s = group_scores[g]
        v16 = jnp.concatenate([vp[...].astype(BF16) for vp in vp_refs[g * DECODE_GROUP:(g + 1) * DECODE_GROUP]],
                              axis=0)
        m_old = m_sc[...]
        m_new = jnp.maximum(m_old, jnp.max(s, axis=-1, keepdims=True))
        alpha = jnp.exp2(m_old - m_new)
        p = jnp.exp2(s - m_new)
        l_sc[...] = alpha * l_sc[...] + jnp.sum(p, axis=-1, keepdims=True)
        acc_sc[...] = alpha * acc_sc[...] + jnp.dot(p.astype(BF16), v16, preferred_element_type=F32)
        m_sc[...] = m_new

    @pl.when(st == last)
    def _():
        lam = _diff_lambda(lam_ref, lam_init)
        on = acc_sc[...] * (1.0 / l_sc[...])
        for h in range(n_heads):
            oh = (on[h * dec_seq:(h + 1) * dec_seq]
                  - lam * on[map_rows + h * dec_seq:map_rows + (h + 1) * dec_seq])
            o_ref[:, h * vd:(h + 1) * vd] = _sub_layer_norm(oh, g_ref, lam_init)


def _decode_attn(q32, kv32, cache_k, cache_v, page_table, bias_past, bias_new, lam_params, subln, layer,
                 lam_init, *, new_row0, n_heads, head_dim):
    n_rows, d = q32.shape
    dec_batch, n_pages = page_table.shape
    dec_seq = n_rows // dec_batch
    n_attn, n_pool, page = cache_k.shape[:3]
    ck = cache_k.reshape(n_attn, n_pool, page * 2 * n_heads, head_dim)
    cv = cache_v.reshape(n_attn, n_pool, page * n_heads, 2 * head_dim)
    nps = DECODE_PAGES
    assert n_pages % nps == 0 and nps % DECODE_GROUP == 0 and new_row0 % dec_seq == 0
    n_steps = n_pages // nps
    vd = 2 * head_dim
    rows = 2 * n_heads * dec_seq
    new_blk = new_row0 // dec_seq

    def page_spec(t, arr):
        return pl.BlockSpec((None, None) + arr.shape[2:],
                            lambda bd, st, pt: (layer, pt[bd * n_pages + st * nps + t], 0, 0))

    grid_spec = pltpu.PrefetchScalarGridSpec(
        num_scalar_prefetch=1,
        grid=(dec_batch, n_steps),
        in_specs=[pl.BlockSpec((dec_seq, d), lambda bd, st, pt: (bd, 0)),
                  pl.BlockSpec((dec_seq, d), lambda bd, st, pt: (new_blk + bd, 0)),
                  pl.BlockSpec((dec_seq, d), lambda bd, st, pt: (new_blk + bd, 1))]
                 + [page_spec(t, ck) for t in range(nps)] + [page_spec(t, cv) for t in range(nps)]
                 + [pl.BlockSpec(bias_past.shape, lambda bd, st, pt: (0, 0)),
                    pl.BlockSpec(bias_new.shape, lambda bd, st, pt: (0, 0)),
                    pl.BlockSpec((None, 4, head_dim), lambda bd, st, pt: (layer, 0, 0)),
                    pl.BlockSpec((None, 1, vd), lambda bd, st, pt: (layer, 0, 0))],
        out_specs=pl.BlockSpec((dec_seq, d), lambda bd, st, pt: (bd, 0)),
        scratch_shapes=[pltpu.VMEM((2, rows // 2, head_dim), BF16), pltpu.VMEM((rows, V7X_LANES), F32),
                        pltpu.VMEM((rows, 1), F32), pltpu.VMEM((rows, 1), F32),
                        pltpu.VMEM((rows, vd), F32)],
    )
    kern = functools.partial(_decode_kernel, n_heads=n_heads, head_dim=head_dim, n_pages_step=nps,
                             lam_init=lam_init)
    return pl.pallas_call(
        kern,
        grid_spec=grid_spec,
        out_shape=jax.ShapeDtypeStruct((n_rows, d), F32),
        compiler_params=_cparams(("arbitrary", "arbitrary")),
        name="decode_attn",
    )(page_table.reshape(-1), q32, kv32, kv32, *([ck] * nps), *([cv] * nps),
      bias_past, bias_new, lam_params, subln)


def kernel(x_prompt, x_sample, cache_k, cache_v, state_conv, page_table, meta_tokens, rel_bias,
           attn_w_qkv, attn_lambda, attn_subln, attn_w_o, conv_w_in, conv_w, conv_w_out,
           mlp_w_up, mlp_w_down, ln_mix_g, ln_mix_b, ln_mlp_g, ln_mlp_b):
    n_batch, seq, d = x_prompt.shape
    dec_batch, dec_seq, _ = x_sample.shape
    n_meta = meta_tokens.shape[0]
    n_buckets, n_heads = rel_bias.shape
    head_dim = cache_k.shape[-1]
    vd = cache_v.shape[-1]
    page = cache_k.shape[2]
    depth = mlp_w_up.shape[0]
    assert d == n_heads * vd and vd == 2 * head_dim
    alpha = (2 * depth) ** 0.25

    mp = n_batch * seq
    n_meta_rows = n_batch * n_meta
    n_sample_rows = dec_batch * dec_seq
    assert n_meta_rows + n_sample_rows <= SMALL_ROWS and seq % FLASH_TQ == 0
    assert n_meta % V7X_SUBLANES == 0 and dec_seq % V7X_SUBLANES == 0
    meta0, samp0 = mp, mp + n_meta_rows
    mt = mp + SMALL_ROWS

    x = jnp.concatenate(
        [x_prompt.reshape(mp, d)] + [meta_tokens.astype(F32)] * n_batch
        + [x_sample.reshape(n_sample_rows, d), jnp.zeros((mt - samp0 - n_sample_rows, d), F32)], axis=0)

    far = _far_distance(n_buckets)
    tq, ck, nps = FLASH_TQ, FLASH_CK, DECODE_PAGES
    ratio = tq // ck
    assert ck + 1 >= far and n_meta + tq - (n_meta - 1) >= far and (nps - 1) * page + 1 >= far
    qq = np.arange(tq)[None, :]
    near_idx = np.stack([_bucket_tile(qq - (t - 1) * ck - np.arange(ck)[:, None], n_buckets)
                         for t in range(ratio + 1)])
    meta_idx = _bucket_tile(n_meta + qq - np.arange(n_meta)[:, None], n_buckets)
    mm_idx = _bucket_tile(np.arange(n_meta)[:, None] - np.arange(n_meta)[None, :], n_buckets)
    step_of_row = (np.arange(2 * dec_seq) % dec_seq)[:, None]
    past_token = (np.arange(nps * page * n_heads) // n_heads)[None, :]
    past_idx = _bucket_tile(nps * page + step_of_row - past_token, n_buckets)
    new_cols = np.arange(2 * dec_seq)[None, :]
    new_idx = np.where(new_cols < dec_seq, _bucket_tile(step_of_row - new_cols, n_buckets), -1).astype(np.int32)

    bias_near = _bias_expand(near_idx.reshape((ratio + 1) * ck, tq), rel_bias).reshape(
        n_heads, ratio + 1, ck, tq)
    bias_meta = _bias_expand(meta_idx, rel_bias)
    bias_mm = _bias_expand(mm_idx, rel_bias)

    def decode_rows(tile):
        lanes = tile.shape[-1]
        return tile.reshape(n_heads, 2, dec_seq, lanes).transpose(1, 0, 2, 3).reshape(2 * n_heads * dec_seq, lanes)

    bias_past = decode_rows(_bias_expand(past_idx, rel_bias))
    bias_new = decode_rows(_bias_expand(new_idx, rel_bias))

    def conv_history(state):
        h1 = jnp.zeros((dec_batch, dec_seq, d), F32).at[:, 0].set(state[:, 1])
        h2 = jnp.zeros((dec_batch, dec_seq, d), F32).at[:, 0].set(state[:, 0]).at[:, 1].set(state[:, 1])
        pad_lo = jnp.zeros((n_meta_rows, d), F32)
        pad_hi = jnp.zeros((SMALL_ROWS - n_meta_rows - n_sample_rows, d), F32)
        cat = lambda h: jnp.concatenate([pad_lo, h.reshape(n_sample_rows, d), pad_hi], axis=0)
        return cat(h1), cat(h2)

    subln3 = attn_subln.reshape(attn_subln.shape[0], 1, vd)
    attn_w_qkv, conv_w_in, attn_w_o, conv_w_out = (
        w.astype(BF16) for w in (attn_w_qkv, conv_w_in, attn_w_o, conv_w_out))
    kv_layers, c_p, c_s = [], [], []
    for i in range(depth):
        j = i // N_MIXERS
        g_mix, b_mix = ln_mix_g[i][None], ln_mix_b[i][None]
        if i % N_MIXERS == 0:
            lam0 = _lambda_init(i)
            qkv16, kv32 = _qkv_proj(x, attn_w_qkv, j, head_dim ** -0.5 * LOG2_E)
            o16, o_meta = _flash_prompt(qkv16, bias_near, bias_meta, bias_mm, attn_lambda, subln3, j, lam0,
                                        n_batch=n_batch, seq=seq, n_heads=n_heads, head_dim=head_dim,
                                        n_meta=n_meta)
            q32 = qkv16[samp0:samp0 + n_sample_rows, :d].astype(F32)
            o_samp = _decode_attn(q32, kv32, cache_k, cache_v, page_table, bias_past, bias_new, attn_lambda,
                                  subln3, j, lam0, new_row0=samp0, n_heads=n_heads, head_dim=head_dim)
            o_small = jnp.concatenate(
                [o_meta, o_samp, jnp.zeros((mt - samp0 - n_sample_rows, d), F32)], axis=0).astype(BF16)
            x = _outproj_ln(o16, o_small, attn_w_o, j, x, g_mix, b_mix, alpha)
            kv_layers.append(kv32)
        else:
            bg, u = _conv_in(x, conv_w_in, j)
            hist1, hist2 = conv_history(state_conv[j])
            z16 = _conv_gate(u, bg, hist1, hist2, conv_w, j, seq=seq, n_batch=n_batch, n_meta=n_meta,
                             dec_seq=dec_seq, n_sample_rows=n_sample_rows)
            x = _outproj_ln(z16, None, conv_w_out, j, x, g_mix, b_mix, alpha)
            c_p.append(jnp.stack([u[(b + 1) * seq - 2:(b + 1) * seq] for b in range(n_batch)]))
            c_s.append(u[samp0:samp0 + n_sample_rows].reshape(dec_batch, dec_seq, d)[:, dec_seq - 2:])
        last = i == depth - 1
        x = _mlp_ln(x, mlp_w_up, mlp_w_down, i, ln_mlp_g[i][None], ln_mlp_b[i][None], alpha,
                    n_main=mp if last else None)

    y_main, y_small = x
    y_prompt = y_main.reshape(n_batch, seq, d)
    y_sample = y_small[n_meta_rows:n_meta_rows + n_sample_rows].reshape(dec_batch, dec_seq, d)

    def new_cache(col0, tail):
        cols = slice(col0, col0 + d)
        pieces = []
        for kv in kv_layers:
            for b in range(n_batch):
                pieces.append(kv[meta0 + b * n_meta:meta0 + (b + 1) * n_meta, cols])
                pieces.append(kv[b * seq:(b + 1) * seq, cols])
        prompt = jnp.concatenate(pieces, axis=0).reshape((len(kv_layers), n_batch, n_meta + seq) + tail)
        sample = jnp.stack([kv[samp0:samp0 + n_sample_rows, cols].reshape((dec_batch, dec_seq) + tail)
                            for kv in kv_layers])
        return prompt, sample

    k_p, k_s = new_cache(0, (n_heads, 2, head_dim))
    v_p, v_s = new_cache(d, (n_heads, vd))
    return (y_prompt, y_sample, k_p, v_p, jnp.stack(c_p), k_s, v_s, jnp.stack(c_s))
```
